```python
import jax, jax.numpy as jnp
from jax import lax
import numpy as np

D_MODEL = 1024
BATCH = 2
SEQ = 8192
DEPTH = 2
DEC_BATCH = 8
DEC_SEQ = 2048
PAST_LEN = 128

GRID_W = 64
EPS = 1e-6
RNN_WIDTH = D_MODEL
RNN_BLOCKS = 16
RNN_BLOCK_W = RNN_WIDTH // RNN_BLOCKS
CONV_W = 4
CONV_PAD_L = CONV_W // 2
CONV_PAD_R = CONV_W - 1 - CONV_PAD_L
LRU_C = 8.0
N_HEADS = 16
N_KV_HEADS = 4
HEAD_DIM = D_MODEL // N_HEADS
GROUP = N_HEADS // N_KV_HEADS
AXIS_DIM = HEAD_DIM // 2
ROPE_THETA = 10000.0
Q_BLOCK = 128
N_KEYS = 128
N_EXPERTS = N_KEYS * N_KEYS
PEER_HEADS = 8
PEER_TOPK = 16
PEER_QDIM = 256
PEER_HALF = PEER_QDIM // 2
TOKEN_BLOCK = 128

N_REC = (DEPTH + 1) // 2
N_ATT = DEPTH // 2

kernel_name = "hybrid_rglru_axialgqa_peer_encoder"


def rmsnorm(x, g):
    xf = x.astype(jnp.float32)
    y = xf * lax.rsqrt(jnp.mean(xf * xf, axis=-1, keepdims=True) + EPS)
    return (y * g.astype(jnp.float32)).astype(x.dtype)


def linear_scan_fwd(a, b):
    def combine(e1, e2):
        a1, b1 = e1
        a2, b2 = e2
        return a1 * a2, a2 * b1 + b2
    _, hs = lax.associative_scan(combine, (a, b), axis=1)
    return hs


def rglru_mixer(h, w_in, conv_w, conv_b, ga_w, ga_b, gx_w, gx_b, lam, w_out):
    B, S, _ = h.shape
    u = h @ w_in
    y_br, x_br = jnp.split(u, 2, axis=-1)
    y_br = jax.nn.gelu(y_br)
    xc = lax.conv_general_dilated(x_br, conv_w[:, None, :], window_strides=(1,),
                                  padding=[(CONV_PAD_L, CONV_PAD_R)],
                                  dimension_numbers=('NWC', 'WIO', 'NWC'),
                                  feature_group_count=RNN_WIDTH) + conv_b
    xb = xc.reshape(B, S, RNN_BLOCKS, RNN_BLOCK_W)
    xf = xc.astype(jnp.float32)
    h_sum = jnp.zeros((B, S, RNN_WIDTH), jnp.float32)
    for d in range(2):
        r = jax.nn.sigmoid((jnp.einsum('bsni,nij->bsnj', xb, ga_w[d]) + ga_b[d]).astype(jnp.float32)).reshape(B, S, RNN_WIDTH)
        ig = jax.nn.sigmoid((jnp.einsum('bsni,nij->bsnj', xb, gx_w[d]) + gx_b[d]).astype(jnp.float32)).reshape(B, S, RNN_WIDTH)
        log_a = -LRU_C * r * jax.nn.softplus(-lam[d].astype(jnp.float32))
        a = jnp.exp(log_a)
        b = jnp.sqrt(-jnp.expm1(2.0 * log_a)) * (ig * xf)
        if d == 0:
            hs = linear_scan_fwd(a, b)
        else:
            hs = jnp.flip(linear_scan_fwd(jnp.flip(a, 1), jnp.flip(b, 1)), 1)
        h_sum = h_sum + hs
    return (h_sum.astype(h.dtype) * y_br) @ w_out


def rope_rotate(x, ang):
    cos = jnp.cos(ang)[None, :, None, :].astype(x.dtype)
    sin = jnp.sin(ang)[None, :, None, :].astype(x.dtype)
    x1, x2 = jnp.split(x, 2, axis=-1)
    return jnp.concatenate([x1 * cos - x2 * sin, x2 * cos + x1 * sin], axis=-1)


def axial_rope(x, S):
    rows = S // GRID_W
    row = jnp.repeat(jnp.arange(rows, dtype=jnp.float32), GRID_W)
    col = jnp.tile(jnp.arange(GRID_W, dtype=jnp.float32), rows)
    inv = ROPE_THETA ** (-jnp.arange(0, AXIS_DIM, 2, dtype=jnp.float32) / AXIS_DIM)
    return jnp.concatenate([rope_rotate(x[..., :AXIS_DIM], row[:, None] * inv),
                            rope_rotate(x[..., AXIS_DIM:], col[:, None] * inv)], axis=-1)


def attention_mixer(h, w_qkv, q_g, k_g, w_o):
    B, S, _ = h.shape
    qkv = h @ w_qkv
    q, k, v = jnp.split(qkv, [N_HEADS * HEAD_DIM, (N_HEADS + N_KV_HEADS) * HEAD_DIM], axis=-1)
    q = rmsnorm(q.reshape(B, S, N_HEADS, HEAD_DIM), q_g)
    k = rmsnorm(k.reshape(B, S, N_KV_HEADS, HEAD_DIM), k_g)
    v = v.reshape(B, S, N_KV_HEADS, HEAD_DIM)
    q = axial_rope(q, S) * (HEAD_DIM ** -0.5)
    k = axial_rope(k, S)
    q = q.reshape(B, S // Q_BLOCK, Q_BLOCK, N_KV_HEADS, GROUP, HEAD_DIM).transpose(1, 0, 2, 3, 4, 5)

    def q_block(qi):
        s = jnp.einsum('bqgrd,bkgd->bgrqk', qi, k).astype(jnp.float32)
        p = jax.nn.softmax(s, axis=-1).astype(v.dtype)
        return jnp.einsum('bgrqk,bkgd->bqgrd', p, v)

    o = lax.map(q_block, q)
    o = o.transpose(1, 0, 2, 3, 4, 5).reshape(B, S, N_HEADS * HEAD_DIM)
    return o @ w_o


def peer_mixer(h, w_query, sub_keys, expert_u, expert_v):
    B, S, D = h.shape
    xt = h.reshape((B * S) // TOKEN_BLOCK, TOKEN_BLOCK, D)

    def token_block(xb):
        q = (xb @ w_query).reshape(TOKEN_BLOCK, PEER_HEADS, 2, PEER_HALF)
        s = jnp.einsum('thpc,pnc->thpn', q, sub_keys).astype(jnp.float32)
        s_top, i_top = lax.top_k(s, PEER_TOPK)
        cand = (s_top[:, :, 0, :, None] + s_top[:, :, 1, None, :]).reshape(TOKEN_BLOCK, PEER_HEADS, PEER_TOPK * PEER_TOPK)
        cand_id = (i_top[:, :, 0, :, None] * N_KEYS + i_top[:, :, 1, None, :]).reshape(TOKEN_BLOCK, PEER_HEADS, PEER_TOPK * PEER_TOPK)
        best, pos = lax.top_k(cand, PEER_TOPK)
        eid = jnp.take_along_axis(cand_id, pos, axis=-1)
        g = jax.nn.softmax(best, axis=-1)
        u = expert_u[eid]
        v = expert_v[eid]
        act = jax.nn.gelu(jnp.einsum('thkd,td->thk', u, xb).astype(jnp.float32))
        return jnp.einsum('thk,thkd->td', (g * act).astype(xb.dtype), v)

    return lax.map(token_block, xt).reshape(B, S, D)


def setup_inputs(seed: int = 0) -> dict:
    key = jax.random.key(seed)
    ks = iter(jax.random.split(key, 40))
    nrm = lambda shape, scale: jax.random.normal(next(ks), shape, jnp.float32) * scale
    D, W = D_MODEL, RNN_WIDTH
    a0 = jax.random.uniform(next(ks), (N_REC, 2, W), jnp.float32, 0.9, 0.999)
    p = a0 ** (1.0 / LRU_C)
    rec_lam = jnp.log(p) - jnp.log1p(-p)
    return {
        "x_prompt": nrm((BATCH, SEQ, D), 1.0),
        "x_sample": nrm((DEC_BATCH, DEC_SEQ, D), 1.0),
        "c_prompt": nrm((BATCH, D), 1.0),
        "c_sample": nrm((DEC_BATCH, D), 1.0),
        "ln_mix_g": 1.0 + nrm((DEPTH, D), 0.02),
        "ln_ffn_g": 1.0 + nrm((DEPTH, D), 0.02),
        "w_mod": nrm((DEPTH, D, 6 * D), D ** -0.5),
        "b_mod": nrm((DEPTH, 6 * D), 0.02),
        "rec_w_in": nrm((N_REC, D, 2 * W), D ** -0.5),
        "rec_conv_w": nrm((N_REC, CONV_W, W), CONV_W ** -0.5),
        "rec_conv_b": nrm((N_REC, W), 0.02),
        "rec_ga_w": nrm((N_REC, 2, RNN_BLOCKS, RNN_BLOCK_W, RNN_BLOCK_W), RNN_BLOCK_W ** -0.5),
        "rec_ga_b": nrm((N_REC, 2, RNN_BLOCKS, RNN_BLOCK_W), 0.02),
        "rec_gx_w": nrm((N_REC, 2, RNN_BLOCKS, RNN_BLOCK_W, RNN_BLOCK_W), RNN_BLOCK_W ** -0.5),
        "rec_gx_b": nrm((N_REC, 2, RNN_BLOCKS, RNN_BLOCK_W), 0.02),
        "rec_lam": rec_lam,
        "rec_w_out": nrm((N_REC, W, D), W ** -0.5),
        "att_w_qkv": nrm((N_ATT, D, (N_HEADS + 2 * N_KV_HEADS) * HEAD_DIM), D ** -0.5),
        "att_q_g": 1.0 + nrm((N_ATT, HEAD_DIM), 0.02),
        "att_k_g": 1.0 + nrm((N_ATT, HEAD_DIM), 0.02),
        "att_w_o": nrm((N_ATT, N_HEADS * HEAD_DIM, D), (N_HEADS * HEAD_DIM) ** -0.5),
        "peer_w_query": nrm((DEPTH, D, PEER_HEADS * PEER_QDIM), D ** -0.5),
        "peer_sub_keys": nrm((DEPTH, 2, N_KEYS, PEER_HALF), PEER_HALF ** -0.5),
        "peer_u": nrm((DEPTH, N_EXPERTS, D), D ** -0.5),
        "peer_v": nrm((DEPTH, N_EXPERTS, D), 1.0),
        "final_g": 1.0 + nrm((D,), 0.02),
    }


def reference(x_prompt, x_sample, c_prompt, c_sample, ln_mix_g, ln_ffn_g, w_mod, b_mod,
              rec_w_in, rec_conv_w, rec_conv_b, rec_ga_w, rec_ga_b, rec_gx_w, rec_gx_b,
              rec_lam, rec_w_out, att_w_qkv, att_q_g, att_k_g, att_w_o,
              peer_w_query, peer_sub_keys, peer_u, peer_v, final_g):
    def trunk(x, c):
        for i in range(DEPTH):
            mod = (jax.nn.silu(c) @ w_mod[i] + b_mod[i])[:, None, :]
            sh1, sc1, g1, sh2, sc2, g2 = jnp.split(mod, 6, axis=-1)
            h = rmsnorm(x, ln_mix_g[i]) * (1.0 + sc1) + sh1
            j = i // 2
            if i % 2 == 0:
                m = rglru_mixer(h, rec_w_in[j], rec_conv_w[j], rec_conv_b[j], rec_ga_w[j], rec_ga_b[j],
                                rec_gx_w[j], rec_gx_b[j], rec_lam[j], rec_w_out[j])
            else:
                m = attention_mixer(h, att_w_qkv[j], att_q_g[j], att_k_g[j], att_w_o[j])
            x = x + g1 * m
            h = rmsnorm(x, ln_ffn_g[i]) * (1.0 + sc2) + sh2
            x = x + g2 * peer_mixer(h, peer_w_query[i], peer_sub_keys[i], peer_u[i], peer_v[i])
        return rmsnorm(x, final_g)

    y_prompt = trunk(x_prompt, c_prompt)
    y_sample = trunk(x_sample, c_sample)
    return (y_prompt, y_sample)
```

```python
import functools

import jax
import jax.numpy as jnp
from jax import lax
from jax.experimental import pallas as pl
from jax.experimental.pallas import tpu as pltpu

F32 = jnp.float32
BF16 = jnp.bfloat16
I32 = jnp.int32

D_MODEL = 1024
DEPTH = 2
GRID_W = 64
EPS = 1e-6
RNN_WIDTH = D_MODEL
RNN_BLOCKS = 16
RNN_BLOCK_W = RNN_WIDTH // RNN_BLOCKS
CONV_W = 4
LRU_C = 8.0
N_HEADS = 16
N_KV_HEADS = 4
HEAD_DIM = D_MODEL // N_HEADS
GROUP = N_HEADS // N_KV_HEADS
AXIS_DIM = HEAD_DIM // 2
ROPE_THETA = 10000.0
N_KEYS = 128
N_EXPERTS = N_KEYS * N_KEYS
PEER_HEADS = 8
PEER_TOPK = 16
PEER_QDIM = 256
PEER_HALF = PEER_QDIM // 2
PEER_PICKS = PEER_HEADS * PEER_TOPK

LANES = 128
SUBLANES = 8
VMEM_LIMIT = 48 * 1024 * 1024

GATHER_PITCH = 12
GATHER_TOKENS = 64
ROWS_PER_EXPERT = D_MODEL // LANES


def _cparams(*sem):
    return pltpu.CompilerParams(dimension_semantics=sem, vmem_limit_bytes=VMEM_LIMIT)


def _gelu(x):
    return jax.nn.gelu(x)


def _norm_mod(x, gamma, scale, shift):
    ms = jnp.mean(x * x, axis=-1, keepdims=True)
    y = x * lax.rsqrt(ms + EPS) * gamma
    return y * (1.0 + scale) + shift


def _mod_body(c_ref, w_ref, b_ref, o_ref):
    c = c_ref[...]
    s = c * jax.nn.sigmoid(c)
    o_ref[...] = jnp.dot(s, w_ref[...], preferred_element_type=F32) + b_ref[...]


def _mod(c, w, b):
    nb, n = c.shape[0], w.shape[1]
    tn = 1536
    out = pl.pallas_call(
        _mod_body,
        grid=(n // tn,),
        in_specs=[
            pl.BlockSpec((nb, D_MODEL), lambda j: (0, 0)),
            pl.BlockSpec((D_MODEL, tn), lambda j: (0, j)),
            pl.BlockSpec((1, tn), lambda j: (0, j)),
        ],
        out_specs=pl.BlockSpec((nb, tn), lambda j: (0, j)),
        out_shape=jax.ShapeDtypeStruct((nb, n), F32),
        compiler_params=_cparams("arbitrary"),
        name="adaln_mod",
    )(c, w, b.reshape(1, n))
    return out.reshape(nb, 6, D_MODEL)


def _nmm_body(x_ref, mod_ref, g_ref, w_ref, o_ref, *, sh, sc):
    h = _norm_mod(x_ref[0], g_ref[...], mod_ref[0, sc:sc + 1, :], mod_ref[0, sh:sh + 1, :])
    o_ref[0] = jnp.dot(h.astype(BF16), w_ref[...], preferred_element_type=F32)


def _norm_mod_matmul(x, mod, gamma, w_bf16, sh, sc, tm=512):
    nb, s, _ = x.shape
    n = w_bf16.shape[1]
    return pl.pallas_call(
        functools.partial(_nmm_body, sh=sh, sc=sc),
        grid=(nb, s // tm),
        in_specs=[
            pl.BlockSpec((1, tm, D_MODEL), lambda b, i: (b, i, 0)),
            pl.BlockSpec((1, 6, D_MODEL), lambda b, i: (b, 0, 0)),
            pl.BlockSpec((1, D_MODEL), lambda b, i: (0, 0)),
            pl.BlockSpec((D_MODEL, n), lambda b, i: (0, 0)),
        ],
        out_specs=pl.BlockSpec((1, tm, n), lambda b, i: (b, i, 0)),
        out_shape=jax.ShapeDtypeStruct((nb, s, n), F32),
        compiler_params=_cparams("arbitrary", "arbitrary"),
        name="norm_mod_matmul",
    )(x, mod, gamma, w_bf16)


def _proj_res_body(m_ref, w_ref, x_ref, mod_ref, o_ref, *, gi):
    y = jnp.dot(m_ref[0].astype(BF16), w_ref[...], preferred_element_type=F32)
    o_ref[0] = x_ref[0] + mod_ref[0, gi:gi + 1, :] * y


def _proj_residual(m, w_bf16, x, mod, gi, tm=512):
    nb, s, k = m.shape
    return pl.pallas_call(
        functools.partial(_proj_res_body, gi=gi),
        grid=(nb, s // tm),
        in_specs=[
            pl.BlockSpec((1, tm, k), lambda b, i: (b, i, 0)),
            pl.BlockSpec((k, D_MODEL), lambda b, i: (0, 0)),
            pl.BlockSpec((1, tm, D_MODEL), lambda b, i: (b, i, 0)),
            pl.BlockSpec((1, 6, D_MODEL), lambda b, i: (b, 0, 0)),
        ],
        out_specs=pl.BlockSpec((1, tm, D_MODEL), lambda b, i: (b, i, 0)),
        out_shape=jax.ShapeDtypeStruct((nb, s, D_MODEL), F32),
        compiler_params=_cparams("arbitrary", "arbitrary"),
        name="proj_residual",
    )(m, w_bf16, x, mod)


def _scan_tile(a, b, carry, reverse):
    tt = a.shape[0]
    row = lax.broadcasted_iota(I32, (tt, LANES), 0) % SUBLANES
    for d in (1, 2, 4):
        if reverse:
            shift, keep = tt - d, row < SUBLANES - d
        else:
            shift, keep = d, row >= d
        ap = pltpu.roll(a, shift, 0)
        bp = pltpu.roll(b, shift, 0)
        b = jnp.where(keep, a * bp + b, b)
        a = jnp.where(keep, a * ap, a)
    groups = tt // SUBLANES
    hs = [None] * groups
    order = range(groups - 1, -1, -1) if reverse else range(groups)
    for g in order:
        lo = g * SUBLANES
        h = a[lo:lo + SUBLANES] * carry + b[lo:lo + SUBLANES]
        hs[g] = h
        last = h[0:1] if reverse else h[SUBLANES - 1:SUBLANES]
        carry = jnp.broadcast_to(last, (SUBLANES, LANES))
    return jnp.concatenate(hs, axis=0), carry


def _rglru_body(y_ref, x_ref, cw_ref, cb_ref, gaw_ref, gab_ref, gxw_ref, gxb_ref, nc_ref,
                o_ref, xp, hf, *, s, tt):
    nt = s // tt
    pad = SUBLANES
    zeros = jnp.zeros((pad, LANES), F32)
    xp[0:pad, :] = zeros
    xp[s + pad:s + 2 * pad, :] = zeros

    def copy_tile(i, c):
        r0 = pl.multiple_of(i * tt, tt)
        xp[pl.ds(pl.multiple_of(r0 + pad, SUBLANES), tt), :] = x_ref[0, pl.ds(r0, tt), :]
        return c

    lax.fori_loop(0, nt, copy_tile, 0)

    cw = cw_ref[...]
    cb = cb_ref[...]

    def conv_tile(r0):
        win = xp[pl.ds(r0, tt + 2 * pad), :]
        acc = cb + cw[2:3] * win[pad:pad + tt]
        for k in (0, 1, 3):
            shifted = pltpu.roll(win, (2 - k) % (tt + 2 * pad), 0)
            acc = acc + cw[k:k + 1] * shifted[pad:pad + tt]
        return acc

    def gates(d, xc):
        xb = xc.astype(BF16)
        r = jax.nn.sigmoid(jnp.dot(xb, gaw_ref[d, 0], preferred_element_type=F32) + gab_ref[d])
        ig = jax.nn.sigmoid(jnp.dot(xb, gxw_ref[d, 0], preferred_element_type=F32) + gxb_ref[d])
        a = jnp.exp(nc_ref[d] * r)
        b = jnp.sqrt(1.0 - a * a) * (ig * xc)
        return a, b

    carry0 = jnp.zeros((SUBLANES, LANES), F32)

    def fwd(i, carry):
        r0 = pl.multiple_of(i * tt, tt)
        a, b = gates(0, conv_tile(r0))
        h, carry = _scan_tile(a, b, carry, False)
        hf[pl.ds(r0, tt), :] = h
        return carry

    lax.fori_loop(0, nt, fwd, carry0)

    def bwd(ii, carry):
        r0 = pl.multiple_of((nt - 1 - ii) * tt, tt)
        a, b = gates(1, conv_tile(r0))
        h, carry = _scan_tile(a, b, carry, True)
        o_ref[0, pl.ds(r0, tt), :] = (hf[pl.ds(r0, tt), :] + h) * _gelu(y_ref[0, pl.ds(r0, tt), :])
        return carry

    lax.fori_loop(0, nt, bwd, carry0)


def _rglru_core(u, conv_w, conv_b, gaw, gab, gxw, gxb, negc, tt=256):
    nb, s, _ = u.shape
    ng = RNN_WIDTH // LANES
    return pl.pallas_call(
        functools.partial(_rglru_body, s=s, tt=tt),
        grid=(nb, ng),
        in_specs=[
            pl.BlockSpec((1, s, LANES), lambda b, j: (b, 0, j)),
            pl.BlockSpec((1, s, LANES), lambda b, j: (b, 0, ng + j)),
            pl.BlockSpec((CONV_W, LANES), lambda b, j: (0, j)),
            pl.BlockSpec((1, LANES), lambda b, j: (0, j)),
            pl.BlockSpec((2, 1, LANES, LANES), lambda b, j: (0, j, 0, 0)),
            pl.BlockSpec((2, 1, LANES), lambda b, j: (0, 0, j)),
            pl.BlockSpec((2, 1, LANES, LANES), lambda b, j: (0, j, 0, 0)),
            pl.BlockSpec((2, 1, LANES), lambda b, j: (0, 0, j)),
            pl.BlockSpec((2, 1, LANES), lambda b, j: (0, 0, j)),
        ],
        out_specs=pl.BlockSpec((1, s, LANES), lambda b, j: (b, 0, j)),
        out_shape=jax.ShapeDtypeStruct((nb, s, RNN_WIDTH), F32),
        scratch_shapes=[pltpu.VMEM((s + 2 * SUBLANES, LANES), F32), pltpu.VMEM((s, LANES), F32)],
        compiler_params=_cparams("arbitrary", "arbitrary"),
        name="rglru_core",
    )(u, u, conv_w, conv_b, gaw, gab, gxw, gxb, negc)


def _block_diag_groups(w):
    ng = RNN_WIDTH // LANES
    w = w.reshape(ng, 2, RNN_BLOCK_W, RNN_BLOCK_W)
    out = jnp.zeros((ng, LANES, LANES), w.dtype)
    out = out.at[:, :RNN_BLOCK_W, :RNN_BLOCK_W].set(w[:, 0])
    out = out.at[:, RNN_BLOCK_W:, RNN_BLOCK_W:].set(w[:, 1])
    return out


def _rglru_layer(x, mod, gamma, w_in, conv_w, conv_b, ga_w, ga_b, gx_w, gx_b, lam, w_out):
    u = _norm_mod_matmul(x, mod, gamma, w_in.astype(BF16), sh=0, sc=1)
    gaw = jnp.stack([_block_diag_groups(ga_w[d]) for d in range(2)]).astype(BF16)
    gxw = jnp.stack([_block_diag_groups(gx_w[d]) for d in range(2)]).astype(BF16)
    gab = ga_b.reshape(2, 1, RNN_WIDTH)
    gxb = gx_b.reshape(2, 1, RNN_WIDTH)
    negc = (-LRU_C * jax.nn.softplus(-lam)).reshape(2, 1, RNN_WIDTH)
    m = _rglru_core(u, conv_w, conv_b.reshape(1, RNN_WIDTH), gaw, gab, gxw, gxb, negc)
    return _proj_residual(m, w_out.astype(BF16), x, mod, gi=2)


def _seg_mean(x2, seg_ref):
    hi = x2.astype(BF16)
    lo = (x2 - hi.astype(F32)).astype(BF16)
    return (jnp.dot(hi, seg_ref[...], preferred_element_type=F32)
            + jnp.dot(lo, seg_ref[...], preferred_element_type=F32))


def _rope(x, cos, sin, lane_lo):
    outs = []
    for j in range(x.shape[1] // LANES):
        xt = x[:, j * LANES:(j + 1) * LANES]
        rot = jnp.where(lane_lo, pltpu.roll(xt, LANES - AXIS_DIM // 2, 1), pltpu.roll(xt, AXIS_DIM // 2, 1))
        outs.append(xt * cos + rot * sin)
    return jnp.concatenate(outs, axis=1) if len(outs) > 1 else outs[0]


def _qkv_body(x_ref, mod_ref, g_ref, w_ref, segq_ref, segk_ref, qg_ref, kg_ref, cosq_ref, sinq_ref,
              cosk_ref, sink_ref, q_ref, kt_ref, v_ref, *, tm):
    h = _norm_mod(x_ref[0], g_ref[...], mod_ref[0, 1:2, :], mod_ref[0, 0:1, :])
    qkv = jnp.dot(h.astype(BF16), w_ref[...], preferred_element_type=F32)
    nq = N_HEADS * HEAD_DIM
    nk = N_KV_HEADS * HEAD_DIM
    q = qkv[:, :nq]
    k = qkv[:, nq:nq + nk]
    v = qkv[:, nq + nk:]
    lane = lax.broadcasted_iota(I32, (tm, LANES), 1)
    lane_lo = (lane % AXIS_DIM) < (AXIS_DIM // 2)
    q = q * lax.rsqrt(_seg_mean(q * q, segq_ref) + EPS) * qg_ref[...]
    k = k * lax.rsqrt(_seg_mean(k * k, segk_ref) + EPS) * kg_ref[...]
    q = _rope(q, cosq_ref[...], sinq_ref[...], lane_lo)
    k = _rope(k, cosk_ref[...], sink_ref[...], lane_lo)
    q_ref[0] = q.astype(BF16)
    kt = k.T.astype(BF16)
    for g in range(N_KV_HEADS):
        kt_ref[0, g] = kt[g * HEAD_DIM:(g + 1) * HEAD_DIM, :]
        v_ref[0, g] = v[:, g * HEAD_DIM:(g + 1) * HEAD_DIM].astype(BF16)


def _rope_tables(s):
    rows = s // GRID_W
    row = jnp.repeat(jnp.arange(rows, dtype=F32), GRID_W)
    col = jnp.tile(jnp.arange(GRID_W, dtype=F32), rows)
    inv = ROPE_THETA ** (-jnp.arange(0, AXIS_DIM, 2, dtype=F32) / AXIS_DIM)
    ar = row[:, None] * inv
    ac = col[:, None] * inv
    cos = jnp.concatenate([jnp.cos(ar), jnp.cos(ar), jnp.cos(ac), jnp.cos(ac)], axis=1)
    sin = jnp.concatenate([-jnp.sin(ar), jnp.sin(ar), -jnp.sin(ac), jnp.sin(ac)], axis=1)
    return jnp.tile(cos, (1, LANES // HEAD_DIM)), jnp.tile(sin, (1, LANES // HEAD_DIM))


def _attn_body(q_ref, kt_ref, v_ref, o_ref):
    kt = kt_ref[0, 0]
    v = v_ref[0, 0]
    outs = []
    for hh in range(GROUP):
        qh = q_ref[0, :, hh * HEAD_DIM:(hh + 1) * HEAD_DIM]
        sc = jnp.dot(qh, kt, preferred_element_type=F32)
        m = jnp.max(sc, axis=-1, keepdims=True)
        p = jnp.exp(sc - m)
        l = jnp.sum(p, axis=-1, keepdims=True)
        o = jnp.dot(p.astype(BF16), v, preferred_element_type=F32)
        outs.append(o / l)
    o_ref[0] = jnp.concatenate(outs, axis=1).astype(BF16)


def _attention_layer(x, mod, gamma, w_qkv, q_g, k_g, w_o, tm=256, tq=128):
    nb, s, _ = x.shape
    nq = N_HEADS * HEAD_DIM
    nk = N_KV_HEADS * HEAD_DIM
    seg = jnp.kron(jnp.eye(N_HEADS, dtype=F32), jnp.full((HEAD_DIM, HEAD_DIM), 1.0 / HEAD_DIM, F32)).astype(BF16)
    segk = seg[:nk, :nk]
    cos, sin = _rope_tables(s)
    scale = HEAD_DIM ** -0.5
    q, kt, v = pl.pallas_call(
        functools.partial(_qkv_body, tm=tm),
        grid=(nb, s // tm),
        in_specs=[
            pl.BlockSpec((1, tm, D_MODEL), lambda b, i: (b, i, 0)),
            pl.BlockSpec((1, 6, D_MODEL), lambda b, i: (b, 0, 0)),
            pl.BlockSpec((1, D_MODEL), lambda b, i: (0, 0)),
            pl.BlockSpec((D_MODEL, nq + 2 * nk), lambda b, i: (0, 0)),
            pl.BlockSpec((nq, nq), lambda b, i: (0, 0)),
            pl.BlockSpec((nk, nk), lambda b, i: (0, 0)),
            pl.BlockSpec((1, nq), lambda b, i: (0, 0)),
            pl.BlockSpec((1, nk), lambda b, i: (0, 0)),
            pl.BlockSpec((tm, LANES), lambda b, i: (i, 0)),
            pl.BlockSpec((tm, LANES), lambda b, i: (i, 0)),
            pl.BlockSpec((tm, LANES), lambda b, i: (i, 0)),
            pl.BlockSpec((tm, LANES), lambda b, i: (i, 0)),
        ],
        out_specs=[
            pl.BlockSpec((1, tm, nq), lambda b, i: (b, i, 0)),
            pl.BlockSpec((1, N_KV_HEADS, HEAD_DIM, tm), lambda b, i: (b, 0, 0, i)),
            pl.BlockSpec((1, N_KV_HEADS, tm, HEAD_DIM), lambda b, i: (b, 0, i, 0)),
        ],
        out_shape=[
            jax.ShapeDtypeStruct((nb, s, nq), BF16),
            jax.ShapeDtypeStruct((nb, N_KV_HEADS, HEAD_DIM, s), BF16),
            jax.ShapeDtypeStruct((nb, N_KV_HEADS, s, HEAD_DIM), BF16),
        ],
        compiler_params=_cparams("arbitrary", "arbitrary"),
        name="qkv_rope",
    )(x, mod, gamma, w_qkv.astype(BF16), seg, segk,
      jnp.tile(q_g, N_HEADS).reshape(1, nq), jnp.tile(k_g, N_KV_HEADS).reshape(1, nk),
      cos * scale, sin * scale, cos, sin)

    gw = GROUP * HEAD_DIM
    o = pl.pallas_call(
        _attn_body,
        grid=(nb, N_KV_HEADS, s // tq),
        in_specs=[
            pl.BlockSpec((1, tq, gw), lambda b, g, i: (b, i, g)),
            pl.BlockSpec((1, 1, HEAD_DIM, s), lambda b, g, i: (b, g, 0, 0)),
            pl.BlockSpec((1, 1, s, HEAD_DIM), lambda b, g, i: (b, g, 0, 0)),
        ],
        out_specs=pl.BlockSpec((1, tq, gw), lambda b, g, i: (b, i, g)),
        out_shape=jax.ShapeDtypeStruct((nb, s, nq), BF16),
        compiler_params=_cparams("arbitrary", "arbitrary", "arbitrary"),
        name="attention",
    )(q, kt, v)
    return _proj_residual(o, w_o.astype(BF16), x, mod, gi=2)


def _topk_rows(s, ids, k, id_bound):
    vals, picks = [], []
    for _ in range(k):
        m = jnp.max(s, axis=0, keepdims=True)
        i = jnp.min(jnp.where(s == m, ids, id_bound), axis=0, keepdims=True)
        vals.append(m)
        picks.append(i)
        s = jnp.where(ids == i, -jnp.inf, s)
    return jnp.concatenate(vals, axis=0), jnp.concatenate(picks, axis=0)


def _route_body(x_ref, mod_ref, g_ref, wq_ref, keys_ref, h_ref, e_ref, gate_ref, hb, *, tm):
    hd = pl.program_id(2)

    @pl.when(hd == 0)
    def _():
        h = _norm_mod(x_ref[0], g_ref[...], mod_ref[0, 4:5, :], mod_ref[0, 3:4, :])
        h_ref[0] = h
        hb[...] = h.astype(BF16)

    q = jnp.dot(hb[...], wq_ref[...], preferred_element_type=F32)
    key_ids = lax.broadcasted_iota(I32, (N_KEYS, tm), 0)
    tops = []
    for p in range(2):
        qp = q[:, p * PEER_HALF:(p + 1) * PEER_HALF].astype(BF16)
        st = lax.dot_general(keys_ref[p], qp, (((1,), (1,)), ((), ())), preferred_element_type=F32)
        tops.append(_topk_rows(st, key_ids, PEER_TOPK, N_KEYS))
    (v0, i0), (v1, i1) = tops

    r16 = lax.broadcasted_iota(I32, (PEER_TOPK, tm), 0)
    cand, fid, eid = [], [], []
    for a in range(4):
        cand.append(v0[a:a + 1] + v1)
        fid.append(a * PEER_TOPK + r16)
        eid.append(i0[a:a + 1] * N_KEYS + i1)
    for b in range(3):
        cand.append(jnp.where(r16 >= 4, v0 + v1[b:b + 1], -jnp.inf))
        fid.append(r16 * PEER_TOPK + b)
        eid.append(i0 * N_KEYS + i1[b:b + 1])
    cand = jnp.concatenate(cand, axis=0)
    fid = jnp.concatenate(fid, axis=0)
    eid = jnp.concatenate(eid, axis=0)

    best, chosen = [], []
    for _ in range(PEER_TOPK):
        m = jnp.max(cand, axis=0, keepdims=True)
        f = jnp.min(jnp.where(cand == m, fid, PEER_TOPK * PEER_TOPK), axis=0, keepdims=True)
        hit = fid == f
        best.append(m)
        chosen.append(jnp.max(jnp.where(hit, eid, -1), axis=0, keepdims=True))
        cand = jnp.where(hit, -jnp.inf, cand)
    best = jnp.concatenate(best, axis=0)
    ex = jnp.exp(best - best[0:1])
    gate_ref[0, 0] = ex / jnp.sum(ex, axis=0, keepdims=True)
    e_ref[0, 0] = jnp.concatenate(chosen, axis=0)


def _peer_route(x, mod, gamma, w_query, sub_keys, tm=128):
    nb, s, _ = x.shape
    return pl.pallas_call(
        functools.partial(_route_body, tm=tm),
        grid=(nb, s // tm, PEER_HEADS),
        in_specs=[
            pl.BlockSpec((1, tm, D_MODEL), lambda b, i, h: (b, i, 0)),
            pl.BlockSpec((1, 6, D_MODEL), lambda b, i, h: (b, 0, 0)),
            pl.BlockSpec((1, D_MODEL), lambda b, i, h: (0, 0)),
            pl.BlockSpec((D_MODEL, PEER_QDIM), lambda b, i, h: (0, h)),
            pl.BlockSpec((2, N_KEYS, PEER_HALF), lambda b, i, h: (0, 0, 0)),
        ],
        out_specs=[
            pl.BlockSpec((1, tm, D_MODEL), lambda b, i, h: (b, i, 0)),
            pl.BlockSpec((1, 1, PEER_TOPK, tm), lambda b, i, h: (b, h, 0, i)),
            pl.BlockSpec((1, 1, PEER_TOPK, tm), lambda b, i, h: (b, h, 0, i)),
        ],
        out_shape=[
            jax.ShapeDtypeStruct((nb, s, D_MODEL), F32),
            jax.ShapeDtypeStruct((nb, PEER_HEADS, PEER_TOPK, s), I32),
            jax.ShapeDtypeStruct((nb, PEER_HEADS, PEER_TOPK, s), F32),
        ],
        scratch_shapes=[pltpu.VMEM((tm, D_MODEL), BF16)],
        compiler_params=_cparams("arbitrary", "arbitrary", "arbitrary"),
        name="peer_route",
    )(x, mod, gamma, w_query.astype(BF16), sub_keys.astype(BF16))


def _pack_table(u, v):
    ne = u.shape[0]
    pairs = jnp.stack([u.astype(BF16).reshape(ne * ROWS_PER_EXPERT, LANES),
                       v.astype(BF16).reshape(ne * ROWS_PER_EXPERT, LANES)], axis=-1)
    return lax.bitcast_convert_type(pairs, I32)


def _gather_body(eid_ref, h_ref, g_ref, tab_ref, x_ref, mod_ref, fg_ref, o_ref, buf, sem, po,
                 *, tb, final_norm):
    ne = PEER_PICKS
    rows = ROWS_PER_EXPERT

    def issue(t, slot, lo, hi):
        for k in range(lo, hi):
            src = tab_ref.at[pl.ds(pl.multiple_of(eid_ref[t, k] * rows, rows), rows), :]
            dst = buf.at[slot, pl.ds(GATHER_PITCH * k, rows), :]
            pltpu.make_async_copy(src, dst, sem.at[slot]).start()

    def wait(slot):
        pltpu.make_async_copy(tab_ref.at[pl.ds(0, ne * rows), :], buf.at[slot, pl.ds(0, ne * rows), :],
                              sem.at[slot]).wait()

    po[...] = jnp.zeros_like(po)
    issue(0, 0, 0, ne)
    per = ne // (2 * rows)
    even = lax.broadcasted_iota(I32, (SUBLANES, 2 * LANES), 1) % 2 == 0

    def body(t, carry):
        slot = lax.rem(t, 2)
        nxt = jnp.minimum(t + 1, tb - 1)
        wait(slot)
        j = lax.rem(t, SUBLANES)
        base = pl.multiple_of(t - j, SUBLANES)
        rowm = lax.broadcasted_iota(I32, (SUBLANES, LANES), 0) == j

        def bcast_row(tile):
            z = jnp.where(rowm, tile, 0.0)
            z = z + pltpu.roll(z, 4, 0)
            z = z + pltpu.roll(z, 2, 0)
            return z + pltpu.roll(z, 1, 0)

        def chunk(c):
            words = buf[slot, pl.ds(c, ne, stride=GATHER_PITCH), :]
            return pltpu.bitcast(words, BF16)

        s8 = jnp.zeros((SUBLANES, 2 * LANES), F32)
        for c in range(rows):
            issue(nxt, 1 - slot, c * per, (c + 1) * per)
            xb = bcast_row(h_ref[pl.ds(base, SUBLANES), c * LANES:(c + 1) * LANES]).astype(BF16)
            s8 = s8 + lax.dot_general(xb, chunk(c), (((1,), (1,)), ((), ())), preferred_element_type=F32)
        g8 = jnp.concatenate([bcast_row(g_ref[pl.ds(base, SUBLANES), 0:LANES]),
                              bcast_row(g_ref[pl.ds(base, SUBLANES), LANES:2 * LANES])], axis=1)
        coef = pltpu.roll(jnp.where(even, _gelu(s8) * g8, 0.0), 1, 1).astype(BF16)
        for c in range(rows):
            issue(nxt, 1 - slot, (rows + c) * per, (rows + c + 1) * per)
            o = jnp.dot(coef, chunk(c), preferred_element_type=F32)
            cols = slice(c * LANES, (c + 1) * LANES)
            po[pl.ds(base, SUBLANES), cols] = po[pl.ds(base, SUBLANES), cols] + jnp.where(rowm, o, 0.0)
        return carry

    lax.fori_loop(0, tb, body, 0)
    wait(tb % 2)
    y = x_ref[...] + mod_ref[0, 5:6, :] * po[...]
    if final_norm:
        y = y * lax.rsqrt(jnp.mean(y * y, axis=-1, keepdims=True) + EPS) * fg_ref[...]
    o_ref[...] = y


def _peer_gather(eid, h, g, table, x, mod, final_g, seq_len, final_norm, tb=GATHER_TOKENS):
    t = h.shape[0]
    per_seq = seq_len // tb
    return pl.pallas_call(
        functools.partial(_gather_body, tb=tb, final_norm=final_norm),
        grid=(t // tb,),
        in_specs=[
            pl.BlockSpec((tb, PEER_PICKS), lambda i: (i, 0), memory_space=pltpu.SMEM),
            pl.BlockSpec((tb, D_MODEL), lambda i: (i, 0)),
            pl.BlockSpec((tb, 2 * PEER_PICKS), lambda i: (i, 0)),
            pl.BlockSpec(memory_space=pl.ANY),
            pl.BlockSpec((tb, D_MODEL), lambda i: (i, 0)),
            pl.BlockSpec((1, 6, D_MODEL), lambda i: (i // per_seq, 0, 0)),
            pl.BlockSpec((1, D_MODEL), lambda i: (0, 0)),
        ],
        out_specs=pl.BlockSpec((tb, D_MODEL), lambda i: (i, 0)),
        out_shape=jax.ShapeDtypeStruct((t, D_MODEL), F32),
        scratch_shapes=[
            pltpu.VMEM((2, PEER_PICKS * GATHER_PITCH, LANES), I32),
            pltpu.SemaphoreType.DMA((2,)),
            pltpu.VMEM((tb, D_MODEL), F32),
        ],
        compiler_params=_cparams("arbitrary"),
        name="peer_gather",
    )(eid, h, g, table, x, mod, final_g)


def _peer_layer(x, mod, gamma, w_query, sub_keys, table, final_g, final_norm):
    nb, s, _ = x.shape
    h, e_t, g_t = _peer_route(x, mod, gamma, w_query, sub_keys)
    eid = e_t.transpose(0, 3, 1, 2).reshape(nb * s, PEER_PICKS)
    gate = g_t.transpose(0, 3, 1, 2).reshape(nb * s, PEER_PICKS)
    gate = jnp.stack([gate, jnp.zeros_like(gate)], axis=-1).reshape(nb * s, 2 * PEER_PICKS)
    y = _peer_gather(eid, h.reshape(nb * s, D_MODEL), gate, table, x.reshape(nb * s, D_MODEL), mod,
                     final_g, s, final_norm)
    return y.reshape(nb, s, D_MODEL)


def kernel(x_prompt, x_sample, c_prompt, c_sample, ln_mix_g, ln_ffn_g, w_mod, b_mod, rec_w_in, rec_conv_w,
           rec_conv_b, rec_ga_w, rec_ga_b, rec_gx_w, rec_gx_b, rec_lam, rec_w_out, att_w_qkv, att_q_g,
           att_k_g, att_w_o, peer_w_query, peer_sub_keys, peer_u, peer_v, final_g):
    tables = [_pack_table(peer_u[i], peer_v[i]) for i in range(DEPTH)]
    fg = final_g.reshape(1, D_MODEL)

    def trunk(x, c):
        for i in range(DEPTH):
            mod = _mod(c, w_mod[i], b_mod[i])
            gm = ln_mix_g[i].reshape(1, D_MODEL)
            gf = ln_ffn_g[i].reshape(1, D_MODEL)
            j = i // 2
            if i % 2 == 0:
                x = _rglru_layer(x, mod, gm, rec_w_in[j], rec_conv_w[j], rec_conv_b[j], rec_ga_w[j],
                                 rec_ga_b[j], rec_gx_w[j], rec_gx_b[j], rec_lam[j], rec_w_out[j])
            else:
                x = _attention_layer(x, mod, gm, att_w_qkv[j], att_q_g[j], att_k_g[j], att_w_o[j])
            x = _peer_layer(x, mod, gf, peer_w_query[i], peer_sub_keys[i], tables[i], fg,
                            final_norm=(i == DEPTH - 1))
        return x

    return (trunk(x_prompt, c_prompt), trunk(x_sample, c_sample))
```

```python
import functools

import jax
import jax.numpy as jnp
from jax import lax
from jax.experimental import pallas as pl
from jax.experimental.pallas import tpu as pltpu

F32 = jnp.float32
BF16 = jnp.bfloat16
I32 = jnp.int32

D_MODEL = 1024
DEPTH = 2
GRID_W = 64
EPS = 1e-6
RNN_WIDTH = D_MODEL
RNN_BLOCKS = 16
RNN_BLOCK_W = RNN_WIDTH // RNN_BLOCKS
CONV_W = 4
LRU_C = 8.0
N_HEADS = 16
N_KV_HEADS = 4
HEAD_DIM = D_MODEL // N_HEADS
GROUP = N_HEADS // N_KV_HEADS
AXIS_DIM = HEAD_DIM // 2
ROPE_THETA = 10000.0
N_KEYS = 128
N_EXPERTS = N_KEYS * N_KEYS
PEER_HEADS = 8
PEER_TOPK = 16
PEER_QDIM = 256
PEER_HALF = PEER_QDIM // 2
PEER_PICKS = PEER_HEADS * PEER_TOPK

LANES = 128
SUBLANES = 8
VMEM_LIMIT = 48 * 1024 * 1024

GATHER_PITCH = 12
GATHER_TOKENS = 64
GATHER_SLOTS = 4
ROWS_PER_EXPERT = D_MODEL // LANES


def _cparams(*sem):
    return pltpu.CompilerParams(dimension_semantics=sem, vmem_limit_bytes=VMEM_LIMIT)


def _gelu(x):
    return jax.nn.gelu(x)


def _norm_mod(x, gamma, scale, shift):
    ms = jnp.mean(x * x, axis=-1, keepdims=True)
    y = x * lax.rsqrt(ms + EPS) * gamma
    return y * (1.0 + scale) + shift


def _mod_body(c_ref, w_ref, b_ref, o_ref):
    c = c_ref[...]
    s = c * jax.nn.sigmoid(c)
    o_ref[...] = jnp.dot(s, w_ref[...], preferred_element_type=F32) + b_ref[...]


def _mod(c, w, b):
    nb, n = c.shape[0], w.shape[1]
    tn = 1536
    out = pl.pallas_call(
        _mod_body,
        grid=(n // tn,),
        in_specs=[
            pl.BlockSpec((nb, D_MODEL), lambda j: (0, 0)),
            pl.BlockSpec((D_MODEL, tn), lambda j: (0, j)),
            pl.BlockSpec((1, tn), lambda j: (0, j)),
        ],
        out_specs=pl.BlockSpec((nb, tn), lambda j: (0, j)),
        out_shape=jax.ShapeDtypeStruct((nb, n), F32),
        compiler_params=_cparams("arbitrary"),
        name="adaln_mod",
    )(c, w, b.reshape(1, n))
    return out.reshape(nb, 6, D_MODEL)


def _nmm_body(x_ref, mod_ref, g_ref, w_ref, o_ref, *, sh, sc):
    h = _norm_mod(x_ref[0], g_ref[...], mod_ref[0, sc:sc + 1, :], mod_ref[0, sh:sh + 1, :])
    o_ref[0] = jnp.dot(h.astype(BF16), w_ref[...], preferred_element_type=F32)


def _norm_mod_matmul(x, mod, gamma, w_bf16, sh, sc, tm=512):
    nb, s, _ = x.shape
    n = w_bf16.shape[1]
    return pl.pallas_call(
        functools.partial(_nmm_body, sh=sh, sc=sc),
        grid=(nb, s // tm),
        in_specs=[
            pl.BlockSpec((1, tm, D_MODEL), lambda b, i: (b, i, 0)),
            pl.BlockSpec((1, 6, D_MODEL), lambda b, i: (b, 0, 0)),
            pl.BlockSpec((1, D_MODEL), lambda b, i: (0, 0)),
            pl.BlockSpec((D_MODEL, n), lambda b, i: (0, 0)),
        ],
        out_specs=pl.BlockSpec((1, tm, n), lambda b, i: (b, i, 0)),
        out_shape=jax.ShapeDtypeStruct((nb, s, n), F32),
        compiler_params=_cparams("arbitrary", "arbitrary"),
        name="norm_mod_matmul",
    )(x, mod, gamma, w_bf16)


def _proj_res_body(m_ref, w_ref, x_ref, mod_ref, o_ref, *, gi):
    y = jnp.dot(m_ref[0].astype(BF16), w_ref[...], preferred_element_type=F32)
    o_ref[0] = x_ref[0] + mod_ref[0, gi:gi + 1, :] * y


def _proj_residual(m, w_bf16, x, mod, gi, tm=512):
    nb, s, k = m.shape
    return pl.pallas_call(
        functools.partial(_proj_res_body, gi=gi),
        grid=(nb, s // tm),
        in_specs=[
            pl.BlockSpec((1, tm, k), lambda b, i: (b, i, 0)),
            pl.BlockSpec((k, D_MODEL), lambda b, i: (0, 0)),
            pl.BlockSpec((1, tm, D_MODEL), lambda b, i: (b, i, 0)),
            pl.BlockSpec((1, 6, D_MODEL), lambda b, i: (b, 0, 0)),
        ],
        out_specs=pl.BlockSpec((1, tm, D_MODEL), lambda b, i: (b, i, 0)),
        out_shape=jax.ShapeDtypeStruct((nb, s, D_MODEL), F32),
        compiler_params=_cparams("arbitrary", "arbitrary"),
        name="proj_residual",
    )(m, w_bf16, x, mod)


def _scan_tile(a, b, carry, reverse):
    tt = a.shape[0]
    row = lax.broadcasted_iota(I32, (tt, LANES), 0) % SUBLANES
    for d in (1, 2, 4):
        if reverse:
            shift, keep = tt - d, row < SUBLANES - d
        else:
            shift, keep = d, row >= d
        ap = pltpu.roll(a, shift, 0)
        bp = pltpu.roll(b, shift, 0)
        b = jnp.where(keep, a * bp + b, b)
        a = jnp.where(keep, a * ap, a)
    groups = tt // SUBLANES
    hs = [None] * groups
    order = range(groups - 1, -1, -1) if reverse else range(groups)
    for g in order:
        lo = g * SUBLANES
        h = a[lo:lo + SUBLANES] * carry + b[lo:lo + SUBLANES]
        hs[g] = h
        last = h[0:1] if reverse else h[SUBLANES - 1:SUBLANES]
        carry = jnp.broadcast_to(last, (SUBLANES, LANES))
    return jnp.concatenate(hs, axis=0), carry


def _rglru_body(y_ref, x_ref, cw_ref, cb_ref, gaw_ref, gab_ref, gxw_ref, gxb_ref, nc_ref,
                o_ref, xp, hf, *, s, tt):
    nt = s // tt
    pad = SUBLANES
    zeros = jnp.zeros((pad, LANES), F32)
    xp[0:pad, :] = zeros
    xp[s + pad:s + 2 * pad, :] = zeros

    def copy_tile(i, c):
        r0 = pl.multiple_of(i * tt, tt)
        xp[pl.ds(pl.multiple_of(r0 + pad, SUBLANES), tt), :] = x_ref[0, pl.ds(r0, tt), :]
        return c

    lax.fori_loop(0, nt, copy_tile, 0)

    cw = cw_ref[...]
    cb = cb_ref[...]

    def conv_tile(r0):
        win = xp[pl.ds(r0, tt + 2 * pad), :]
        acc = cb + cw[2:3] * win[pad:pad + tt]
        for k in (0, 1, 3):
            shifted = pltpu.roll(win, (2 - k) % (tt + 2 * pad), 0)
            acc = acc + cw[k:k + 1] * shifted[pad:pad + tt]
        return acc

    def gates(d, xc):
        xb = xc.astype(BF16)
        r = jax.nn.sigmoid(jnp.dot(xb, gaw_ref[d, 0], preferred_element_type=F32) + gab_ref[d])
        ig = jax.nn.sigmoid(jnp.dot(xb, gxw_ref[d, 0], preferred_element_type=F32) + gxb_ref[d])
        a = jnp.exp(nc_ref[d] * r)
        b = jnp.sqrt(1.0 - a * a) * (ig * xc)
        return a, b

    carry0 = jnp.zeros((SUBLANES, LANES), F32)

    def fwd(i, carry):
        r0 = pl.multiple_of(i * tt, tt)
        a, b = gates(0, conv_tile(r0))
        h, carry = _scan_tile(a, b, carry, False)
        hf[pl.ds(r0, tt), :] = h
        return carry

    lax.fori_loop(0, nt, fwd, carry0)

    def bwd(ii, carry):
        r0 = pl.multiple_of((nt - 1 - ii) * tt, tt)
        a, b = gates(1, conv_tile(r0))
        h, carry = _scan_tile(a, b, carry, True)
        o_ref[0, pl.ds(r0, tt), :] = (hf[pl.ds(r0, tt), :] + h) * _gelu(y_ref[0, pl.ds(r0, tt), :])
        return carry

    lax.fori_loop(0, nt, bwd, carry0)


def _rglru_core(u, conv_w, conv_b, gaw, gab, gxw, gxb, negc, tt=256):
    nb, s, _ = u.shape
    ng = RNN_WIDTH // LANES
    return pl.pallas_call(
        functools.partial(_rglru_body, s=s, tt=tt),
        grid=(nb, ng),
        in_specs=[
            pl.BlockSpec((1, s, LANES), lambda b, j: (b, 0, j)),
            pl.BlockSpec((1, s, LANES), lambda b, j: (b, 0, ng + j)),
            pl.BlockSpec((CONV_W, LANES), lambda b, j: (0, j)),
            pl.BlockSpec((1, LANES), lambda b, j: (0, j)),
            pl.BlockSpec((2, 1, LANES, LANES), lambda b, j: (0, j, 0, 0)),
            pl.BlockSpec((2, 1, LANES), lambda b, j: (0, 0, j)),
            pl.BlockSpec((2, 1, LANES, LANES), lambda b, j: (0, j, 0, 0)),
            pl.BlockSpec((2, 1, LANES), lambda b, j: (0, 0, j)),
            pl.BlockSpec((2, 1, LANES), lambda b, j: (0, 0, j)),
        ],
        out_specs=pl.BlockSpec((1, s, LANES), lambda b, j: (b, 0, j)),
        out_shape=jax.ShapeDtypeStruct((nb, s, RNN_WIDTH), F32),
        scratch_shapes=[pltpu.VMEM((s + 2 * SUBLANES, LANES), F32), pltpu.VMEM((s, LANES), F32)],
        compiler_params=_cparams("arbitrary", "arbitrary"),
        name="rglru_core",
    )(u, u, conv_w, conv_b, gaw, gab, gxw, gxb, negc)


def _block_diag_groups(w):
    ng = RNN_WIDTH // LANES
    w = w.reshape(ng, 2, RNN_BLOCK_W, RNN_BLOCK_W)
    out = jnp.zeros((ng, LANES, LANES), w.dtype)
    out = out.at[:, :RNN_BLOCK_W, :RNN_BLOCK_W].set(w[:, 0])
    out = out.at[:, RNN_BLOCK_W:, RNN_BLOCK_W:].set(w[:, 1])
    return out


def _rglru_layer(x, mod, gamma, w_in, conv_w, conv_b, ga_w, ga_b, gx_w, gx_b, lam, w_out):
    u = _norm_mod_matmul(x, mod, gamma, w_in.astype(BF16), sh=0, sc=1)
    gaw = jnp.stack([_block_diag_groups(ga_w[d]) for d in range(2)]).astype(BF16)
    gxw = jnp.stack([_block_diag_groups(gx_w[d]) for d in range(2)]).astype(BF16)
    gab = ga_b.reshape(2, 1, RNN_WIDTH)
    gxb = gx_b.reshape(2, 1, RNN_WIDTH)
    negc = (-LRU_C * jax.nn.softplus(-lam)).reshape(2, 1, RNN_WIDTH)
    m = _rglru_core(u, conv_w, conv_b.reshape(1, RNN_WIDTH), gaw, gab, gxw, gxb, negc)
    return _proj_residual(m, w_out.astype(BF16), x, mod, gi=2)


def _seg_mean(x2, seg_ref):
    hi = x2.astype(BF16)
    lo = (x2 - hi.astype(F32)).astype(BF16)
    return (jnp.dot(hi, seg_ref[...], preferred_element_type=F32)
            + jnp.dot(lo, seg_ref[...], preferred_element_type=F32))


def _rope(x, cos, sin, lane_lo):
    outs = []
    for j in range(x.shape[1] // LANES):
        xt = x[:, j * LANES:(j + 1) * LANES]
        rot = jnp.where(lane_lo, pltpu.roll(xt, LANES - AXIS_DIM // 2, 1), pltpu.roll(xt, AXIS_DIM // 2, 1))
        outs.append(xt * cos + rot * sin)
    return jnp.concatenate(outs, axis=1) if len(outs) > 1 else outs[0]


def _qkv_body(x_ref, mod_ref, g_ref, w_ref, segq_ref, segk_ref, qg_ref, kg_ref, cosq_ref, sinq_ref,
              cosk_ref, sink_ref, q_ref, kt_ref, v_ref, *, tm):
    h = _norm_mod(x_ref[0], g_ref[...], mod_ref[0, 1:2, :], mod_ref[0, 0:1, :])
    qkv = jnp.dot(h.astype(BF16), w_ref[...], preferred_element_type=F32)
    nq = N_HEADS * HEAD_DIM
    nk = N_KV_HEADS * HEAD_DIM
    q = qkv[:, :nq]
    k = qkv[:, nq:nq + nk]
    v = qkv[:, nq + nk:]
    lane = lax.broadcasted_iota(I32, (tm, LANES), 1)
    lane_lo = (lane % AXIS_DIM) < (AXIS_DIM // 2)
    q = q * lax.rsqrt(_seg_mean(q * q, segq_ref) + EPS) * qg_ref[...]
    k = k * lax.rsqrt(_seg_mean(k * k, segk_ref) + EPS) * kg_ref[...]
    q = _rope(q, cosq_ref[...], sinq_ref[...], lane_lo)
    k = _rope(k, cosk_ref[...], sink_ref[...], lane_lo)
    q_ref[0] = q.astype(BF16)
    kt = k.T.astype(BF16)
    for g in range(N_KV_HEADS):
        kt_ref[0, g] = kt[g * HEAD_DIM:(g + 1) * HEAD_DIM, :]
        v_ref[0, g] = v[:, g * HEAD_DIM:(g + 1) * HEAD_DIM].astype(BF16)


def _rope_tables(s):
    rows = s // GRID_W
    row = jnp.repeat(jnp.arange(rows, dtype=F32), GRID_W)
    col = jnp.tile(jnp.arange(GRID_W, dtype=F32), rows)
    inv = ROPE_THETA ** (-jnp.arange(0, AXIS_DIM, 2, dtype=F32) / AXIS_DIM)
    ar = row[:, None] * inv
    ac = col[:, None] * inv
    cos = jnp.concatenate([jnp.cos(ar), jnp.cos(ar), jnp.cos(ac), jnp.cos(ac)], axis=1)
    sin = jnp.concatenate([-jnp.sin(ar), jnp.sin(ar), -jnp.sin(ac), jnp.sin(ac)], axis=1)
    return jnp.tile(cos, (1, LANES // HEAD_DIM)), jnp.tile(sin, (1, LANES // HEAD_DIM))


def _attn_body(q_ref, kt_ref, v_ref, o_ref):
    kt = kt_ref[0, 0]
    v = v_ref[0, 0]
    outs = []
    for hh in range(GROUP):
        qh = q_ref[0, :, hh * HEAD_DIM:(hh + 1) * HEAD_DIM]
        sc = jnp.dot(qh, kt, preferred_element_type=F32)
        m = jnp.max(sc, axis=-1, keepdims=True)
        p = jnp.exp(sc - m)
        l = jnp.sum(p, axis=-1, keepdims=True)
        o = jnp.dot(p.astype(BF16), v, preferred_element_type=F32)
        outs.append(o / l)
    o_ref[0] = jnp.concatenate(outs, axis=1).astype(BF16)


def _attention_layer(x, mod, gamma, w_qkv, q_g, k_g, w_o, tm=256, tq=128):
    nb, s, _ = x.shape
    nq = N_HEADS * HEAD_DIM
    nk = N_KV_HEADS * HEAD_DIM
    seg = jnp.kron(jnp.eye(N_HEADS, dtype=F32), jnp.full((HEAD_DIM, HEAD_DIM), 1.0 / HEAD_DIM, F32)).astype(BF16)
    segk = seg[:nk, :nk]
    cos, sin = _rope_tables(s)
    scale = HEAD_DIM ** -0.5
    q, kt, v = pl.pallas_call(
        functools.partial(_qkv_body, tm=tm),
        grid=(nb, s // tm),
        in_specs=[
            pl.BlockSpec((1, tm, D_MODEL), lambda b, i: (b, i, 0)),
            pl.BlockSpec((1, 6, D_MODEL), lambda b, i: (b, 0, 0)),
            pl.BlockSpec((1, D_MODEL), lambda b, i: (0, 0)),
            pl.BlockSpec((D_MODEL, nq + 2 * nk), lambda b, i: (0, 0)),
            pl.BlockSpec((nq, nq), lambda b, i: (0, 0)),
            pl.BlockSpec((nk, nk), lambda b, i: (0, 0)),
            pl.BlockSpec((1, nq), lambda b, i: (0, 0)),
            pl.BlockSpec((1, nk), lambda b, i: (0, 0)),
            pl.BlockSpec((tm, LANES), lambda b, i: (i, 0)),
            pl.BlockSpec((tm, LANES), lambda b, i: (i, 0)),
            pl.BlockSpec((tm, LANES), lambda b, i: (i, 0)),
            pl.BlockSpec((tm, LANES), lambda b, i: (i, 0)),
        ],
        out_specs=[
            pl.BlockSpec((1, tm, nq), lambda b, i: (b, i, 0)),
            pl.BlockSpec((1, N_KV_HEADS, HEAD_DIM, tm), lambda b, i: (b, 0, 0, i)),
            pl.BlockSpec((1, N_KV_HEADS, tm, HEAD_DIM), lambda b, i: (b, 0, i, 0)),
        ],
        out_shape=[
            jax.ShapeDtypeStruct((nb, s, nq), BF16),
            jax.ShapeDtypeStruct((nb, N_KV_HEADS, HEAD_DIM, s), BF16),
            jax.ShapeDtypeStruct((nb, N_KV_HEADS, s, HEAD_DIM), BF16),
        ],
        compiler_params=_cparams("arbitrary", "arbitrary"),
        name="qkv_rope",
    )(x, mod, gamma, w_qkv.astype(BF16), seg, segk,
      jnp.tile(q_g, N_HEADS).reshape(1, nq), jnp.tile(k_g, N_KV_HEADS).reshape(1, nk),
      cos * scale, sin * scale, cos, sin)

    gw = GROUP * HEAD_DIM
    o = pl.pallas_call(
        _attn_body,
        grid=(nb, N_KV_HEADS, s // tq),
        in_specs=[
            pl.BlockSpec((1, tq, gw), lambda b, g, i: (b, i, g)),
            pl.BlockSpec((1, 1, HEAD_DIM, s), lambda b, g, i: (b, g, 0, 0)),
            pl.BlockSpec((1, 1, s, HEAD_DIM), lambda b, g, i: (b, g, 0, 0)),
        ],
        out_specs=pl.BlockSpec((1, tq, gw), lambda b, g, i: (b, i, g)),
        out_shape=jax.ShapeDtypeStruct((nb, s, nq), BF16),
        compiler_params=_cparams("arbitrary", "arbitrary", "arbitrary"),
        name="attention",
    )(q, kt, v)
    return _proj_residual(o, w_o.astype(BF16), x, mod, gi=2)


def _topk_rows(s, ids, k, id_bound):
    vals, picks = [], []
    for _ in range(k):
        m = jnp.max(s, axis=0, keepdims=True)
        i = jnp.min(jnp.where(s == m, ids, id_bound), axis=0, keepdims=True)
        vals.append(m)
        picks.append(i)
        s = jnp.where(ids == i, -jnp.inf, s)
    return jnp.concatenate(vals, axis=0), jnp.concatenate(picks, axis=0)


def _route_body(x_ref, mod_ref, g_ref, wq_ref, keys_ref, h_ref, e_ref, gate_ref, hb, *, tm):
    hd = pl.program_id(2)

    @pl.when(hd == 0)
    def _():
        h = _norm_mod(x_ref[0], g_ref[...], mod_ref[0, 4:5, :], mod_ref[0, 3:4, :])
        h_ref[0] = h
        hb[...] = h.astype(BF16)

    q = jnp.dot(hb[...], wq_ref[...], preferred_element_type=F32)
    key_ids = lax.broadcasted_iota(I32, (N_KEYS, tm), 0)
    tops = []
    for p in range(2):
        qp = q[:, p * PEER_HALF:(p + 1) * PEER_HALF].astype(BF16)
        st = lax.dot_general(keys_ref[p], qp, (((1,), (1,)), ((), ())), preferred_element_type=F32)
        tops.append(_topk_rows(st, key_ids, PEER_TOPK, N_KEYS))
    (v0, i0), (v1, i1) = tops

    r16 = lax.broadcasted_iota(I32, (PEER_TOPK, tm), 0)
    cand, fid, eid = [], [], []
    for a in range(4):
        cand.append(v0[a:a + 1] + v1)
        fid.append(a * PEER_TOPK + r16)
        eid.append(i0[a:a + 1] * N_KEYS + i1)
    for b in range(3):
        cand.append(jnp.where(r16 >= 4, v0 + v1[b:b + 1], -jnp.inf))
        fid.append(r16 * PEER_TOPK + b)
        eid.append(i0 * N_KEYS + i1[b:b + 1])
    cand = jnp.concatenate(cand, axis=0)
    fid = jnp.concatenate(fid, axis=0)
    eid = jnp.concatenate(eid, axis=0)

    best, chosen = [], []
    for _ in range(PEER_TOPK):
        m = jnp.max(cand, axis=0, keepdims=True)
        f = jnp.min(jnp.where(cand == m, fid, PEER_TOPK * PEER_TOPK), axis=0, keepdims=True)
        hit = fid == f
        best.append(m)
        chosen.append(jnp.max(jnp.where(hit, eid, -1), axis=0, keepdims=True))
        cand = jnp.where(hit, -jnp.inf, cand)
    best = jnp.concatenate(best, axis=0)
    ex = jnp.exp(best - best[0:1])
    gate_ref[0, 0] = ex / jnp.sum(ex, axis=0, keepdims=True)
    e_ref[0, 0] = jnp.concatenate(chosen, axis=0)


def _peer_route(x, mod, gamma, w_query, sub_keys, tm=128):
    nb, s, _ = x.shape
    return pl.pallas_call(
        functools.partial(_route_body, tm=tm),
        grid=(nb, s // tm, PEER_HEADS),
        in_specs=[
            pl.BlockSpec((1, tm, D_MODEL), lambda b, i, h: (b, i, 0)),
            pl.BlockSpec((1, 6, D_MODEL), lambda b, i, h: (b, 0, 0)),
            pl.BlockSpec((1, D_MODEL), lambda b, i, h: (0, 0)),
            pl.BlockSpec((D_MODEL, PEER_QDIM), lambda b, i, h: (0, h)),
            pl.BlockSpec((2, N_KEYS, PEER_HALF), lambda b, i, h: (0, 0, 0)),
        ],
        out_specs=[
            pl.BlockSpec((1, tm, D_MODEL), lambda b, i, h: (b, i, 0)),
            pl.BlockSpec((1, 1, PEER_TOPK, tm), lambda b, i, h: (b, h, 0, i)),
            pl.BlockSpec((1, 1, PEER_TOPK, tm), lambda b, i, h: (b, h, 0, i)),
        ],
        out_shape=[
            jax.ShapeDtypeStruct((nb, s, D_MODEL), F32),
            jax.ShapeDtypeStruct((nb, PEER_HEADS, PEER_TOPK, s), I32),
            jax.ShapeDtypeStruct((nb, PEER_HEADS, PEER_TOPK, s), F32),
        ],
        scratch_shapes=[pltpu.VMEM((tm, D_MODEL), BF16)],
        compiler_params=_cparams("arbitrary", "arbitrary", "arbitrary"),
        name="peer_route",
    )(x, mod, gamma, w_query.astype(BF16), sub_keys.astype(BF16))


def _pack_table(u, v):
    ne = u.shape[0]
    pairs = jnp.stack([u.astype(BF16).reshape(ne * ROWS_PER_EXPERT, LANES),
                       v.astype(BF16).reshape(ne * ROWS_PER_EXPERT, LANES)], axis=-1)
    return lax.bitcast_convert_type(pairs, I32)


def _gather_body(eid_ref, h_ref, g_ref, tab_ref, x_ref, mod_ref, fg_ref, o_ref, buf, sem, po,
                 *, tb, final_norm):
    ne = PEER_PICKS
    rows = ROWS_PER_EXPERT

    ns = GATHER_SLOTS
    ahead = ns - 1

    def issue(t, slot, lo, hi):
        for k in range(lo, hi):
            src = tab_ref.at[pl.ds(pl.multiple_of(eid_ref[t, k] * rows, rows), rows), :]
            dst = buf.at[slot, pl.ds(GATHER_PITCH * k, rows), :]
            pltpu.make_async_copy(src, dst, sem.at[slot]).start(priority=k % 2)

    def wait(slot):
        pltpu.make_async_copy(tab_ref.at[pl.ds(0, ne * rows), :], buf.at[slot, pl.ds(0, ne * rows), :],
                              sem.at[slot]).wait()

    for q in range(ahead):
        issue(q, q, 0, ne)
    early = 4
    halves = 2
    per_half = ne // halves
    even = lax.broadcasted_iota(I32, (1, 2 * LANES), 1) % 2 == 0
    row_id = lax.broadcasted_iota(I32, (SUBLANES, LANES), 0)

    def chunk(slot, c, first=0, count=ne):
        words = buf[slot, pl.ds(first * GATHER_PITCH + c, count, stride=GATHER_PITCH), :]
        return pltpu.bitcast(words, BF16)

    def score_half(t8, j, hf, c, acc):
        xrow = h_ref[pl.ds(t8, SUBLANES), c * LANES:(c + 1) * LANES][j:j + 1]
        return acc + chunk(j % ns, c, hf * per_half, per_half).astype(F32) * xrow

    def lane_sums(acc):
        return jnp.sum(acc.T, axis=0, keepdims=True)

    def coefficients(parts, t8, j):
        s = jnp.concatenate(parts, axis=1)
        g = g_ref[pl.ds(t8, SUBLANES), :][j:j + 1]
        coef = pltpu.roll(jnp.where(even, _gelu(s) * g, 0.0), 1, 1)
        return jnp.broadcast_to(coef, (SUBLANES, 2 * LANES))

    wait(0)
    parts = []
    for hf in range(halves):
        a = jnp.zeros((2 * per_half, LANES), F32)
        for c in range(rows):
            a = score_half(0, 0, hf, c, a)
        parts.append(lane_sums(a))
    coef0 = coefficients(parts, 0, 0)

    def body(it, coef):
        t8 = pl.multiple_of(it * SUBLANES, SUBLANES)
        t8_next = pl.multiple_of(jnp.minimum(t8 + SUBLANES, tb - SUBLANES), SUBLANES)
        acc = [None] * rows
        for j in range(SUBLANES):
            jn = (j + 1) % SUBLANES
            t8n = t8 if j + 1 < SUBLANES else t8_next
            nxt = jnp.minimum(t8 + j + ahead, tb - 1)
            nslot = (j + ahead) % ns
            wait((j + 1) % ns)
            coef_b = coef.astype(BF16)
            step = 0
            parts = []
            for hf in range(halves):
                a = jnp.zeros((2 * per_half, LANES), F32)
                for c in range(rows):
                    issue(nxt, nslot, step * early, (step + 1) * early)
                    step += 1
                    a = score_half(t8n, jn, hf, c, a)
                parts.append(lane_sums(a))
            for c in range(rows):
                issue(nxt, nslot, step * early, (step + 1) * early)
                step += 1
                o = jnp.dot(coef_b, chunk(j % ns, c), preferred_element_type=F32)
                acc[c] = o if j == 0 else jnp.where(row_id == j, o, acc[c])
            issue(nxt, nslot, step * early, ne)
            coef = coefficients(parts, t8n, jn)
        for c in range(rows):
            po[pl.ds(t8, SUBLANES), c * LANES:(c + 1) * LANES] = acc[c]
        return coef

    lax.fori_loop(0, tb // SUBLANES, body, coef0)
    for q in range(1, ahead):
        wait((tb + q) % ns)
    y = x_ref[...] + mod_ref[0, 5:6, :] * po[...]
    if final_norm:
        y = y * lax.rsqrt(jnp.mean(y * y, axis=-1, keepdims=True) + EPS) * fg_ref[...]
    o_ref[...] = y


def _peer_gather(eid, h, g, table, x, mod, final_g, seq_len, final_norm, tb=GATHER_TOKENS):
    t = h.shape[0]
    per_seq = seq_len // tb
    return pl.pallas_call(
        functools.partial(_gather_body, tb=tb, final_norm=final_norm),
        grid=(t // tb,),
        in_specs=[
            pl.BlockSpec((tb, PEER_PICKS), lambda i: (i, 0), memory_space=pltpu.SMEM),
            pl.BlockSpec((tb, D_MODEL), lambda i: (i, 0)),
            pl.BlockSpec((tb, 2 * PEER_PICKS), lambda i: (i, 0)),
            pl.BlockSpec(memory_space=pl.ANY),
            pl.BlockSpec((tb, D_MODEL), lambda i: (i, 0)),
            pl.BlockSpec((1, 6, D_MODEL), lambda i: (i // per_seq, 0, 0)),
            pl.BlockSpec((1, D_MODEL), lambda i: (0, 0)),
        ],
        out_specs=pl.BlockSpec((tb, D_MODEL), lambda i: (i, 0)),
        out_shape=jax.ShapeDtypeStruct((t, D_MODEL), F32),
        scratch_shapes=[
            pltpu.VMEM((GATHER_SLOTS, PEER_PICKS * GATHER_PITCH, LANES), I32),
            pltpu.SemaphoreType.DMA((GATHER_SLOTS,)),
            pltpu.VMEM((tb, D_MODEL), F32),
        ],
        compiler_params=_cparams("arbitrary"),
        name="peer_gather",
    )(eid, h, g, table, x, mod, final_g)


def _peer_layer(x, mod, gamma, w_query, sub_keys, table, final_g, final_norm):
    nb, s, _ = x.shape
    h, e_t, g_t = _peer_route(x, mod, gamma, w_query, sub_keys)
    eid = e_t.transpose(0, 3, 1, 2).reshape(nb * s, PEER_PICKS)
    gate = g_t.transpose(0, 3, 1, 2).reshape(nb * s, PEER_PICKS)
    gate = jnp.stack([gate, jnp.zeros_like(gate)], axis=-1).reshape(nb * s, 2 * PEER_PICKS)
    y = _peer_gather(eid, h.reshape(nb * s, D_MODEL), gate, table, x.reshape(nb * s, D_MODEL), mod,
                     final_g, s, final_norm)
    return y.reshape(nb, s, D_MODEL)


def kernel(x_prompt, x_sample, c_prompt, c_sample, ln_mix_g, ln_ffn_g, w_mod, b_mod, rec_w_in, rec_conv_w,
           rec_conv_b, rec_ga_w, rec_ga_b, rec_gx_w, rec_gx_b, rec_lam, rec_w_out, att_w_qkv, att_q_g,
           att_k_g, att_w_o, peer_w_query, peer_sub_keys, peer_u, peer_v, final_g):
    tables = [_pack_table(peer_u[i], peer_v[i]) for i in range(DEPTH)]
    fg = final_g.reshape(1, D_MODEL)

    def trunk(x, c):
        for i in range(DEPTH):
            mod = _mod(c, w_mod[i], b_mod[i])
            gm = ln_mix_g[i].reshape(1, D_MODEL)
            gf = ln_ffn_g[i].reshape(1, D_MODEL)
            j = i // 2
            if i % 2 == 0:
                x = _rglru_layer(x, mod, gm, rec_w_in[j], rec_conv_w[j], rec_conv_b[j], rec_ga_w[j],
                                 rec_ga_b[j], rec_gx_w[j], rec_gx_b[j], rec_lam[j], rec_w_out[j])
            else:
                x = _attention_layer(x, mod, gm, att_w_qkv[j], att_q_g[j], att_k_g[j], att_w_o[j])
            x = _peer_layer(x, mod, gf, peer_w_query[i], peer_sub_keys[i], tables[i], fg,
                            final_norm=(i == DEPTH - 1))
        return x

    return (trunk(x_prompt, c_prompt), trunk(x_sample, c_sample))
```

```python
import dataclasses
import functools

import jax
import jax.numpy as jnp
from jax import lax
from jax.experimental import pallas as pl
from jax.experimental.pallas import tpu as pltpu
from jax.experimental.pallas import tpu_sc as plsc

F32 = jnp.float32
BF16 = jnp.bfloat16
I32 = jnp.int32

D_MODEL = 1024
DEPTH = 2
GRID_W = 64
EPS = 1e-6
RNN_WIDTH = D_MODEL
RNN_BLOCKS = 16
RNN_BLOCK_W = RNN_WIDTH // RNN_BLOCKS
CONV_W = 4
LRU_C = 8.0
N_HEADS = 16
N_KV_HEADS = 4
HEAD_DIM = D_MODEL // N_HEADS
GROUP = N_HEADS // N_KV_HEADS
AXIS_DIM = HEAD_DIM // 2
ROPE_THETA = 10000.0
N_KEYS = 128
N_EXPERTS = N_KEYS * N_KEYS
PEER_HEADS = 8
PEER_TOPK = 16
PEER_QDIM = 256
PEER_HALF = PEER_QDIM // 2
PEER_PICKS = PEER_HEADS * PEER_TOPK

LANES = 128
SUBLANES = 8
VMEM_LIMIT = 48 * 1024 * 1024

GATHER_PITCH = 12
GATHER_TOKENS = 64
GATHER_SLOTS = 4
ROWS_PER_EXPERT = D_MODEL // LANES


def _cparams(*sem):
    return pltpu.CompilerParams(dimension_semantics=sem, vmem_limit_bytes=VMEM_LIMIT)


def _gelu(x):
    return jax.nn.gelu(x)


def _norm_mod(x, gamma, scale, shift):
    ms = jnp.mean(x * x, axis=-1, keepdims=True)
    y = x * lax.rsqrt(ms + EPS) * gamma
    return y * (1.0 + scale) + shift


def _mod_body(c_ref, w_ref, b_ref, o_ref):
    c = c_ref[...]
    s = c * jax.nn.sigmoid(c)
    o_ref[...] = jnp.dot(s, w_ref[...], preferred_element_type=F32) + b_ref[...]


def _mod(c, w, b):
    nb, n = c.shape[0], w.shape[1]
    tn = 1536
    out = pl.pallas_call(
        _mod_body,
        grid=(n // tn,),
        in_specs=[
            pl.BlockSpec((nb, D_MODEL), lambda j: (0, 0)),
            pl.BlockSpec((D_MODEL, tn), lambda j: (0, j)),
            pl.BlockSpec((1, tn), lambda j: (0, j)),
        ],
        out_specs=pl.BlockSpec((nb, tn), lambda j: (0, j)),
        out_shape=jax.ShapeDtypeStruct((nb, n), F32),
        compiler_params=_cparams("arbitrary"),
        name="adaln_mod",
    )(c, w, b.reshape(1, n))
    return out.reshape(nb, 6, D_MODEL)


def _nmm_body(x_ref, mod_ref, g_ref, w_ref, o_ref, *, sh, sc):
    h = _norm_mod(x_ref[0], g_ref[...], mod_ref[0, sc:sc + 1, :], mod_ref[0, sh:sh + 1, :])
    o_ref[0] = jnp.dot(h.astype(BF16), w_ref[...], preferred_element_type=F32)


def _norm_mod_matmul(x, mod, gamma, w_bf16, sh, sc, tm=512):
    nb, s, _ = x.shape
    n = w_bf16.shape[1]
    return pl.pallas_call(
        functools.partial(_nmm_body, sh=sh, sc=sc),
        grid=(nb, s // tm),
        in_specs=[
            pl.BlockSpec((1, tm, D_MODEL), lambda b, i: (b, i, 0)),
            pl.BlockSpec((1, 6, D_MODEL), lambda b, i: (b, 0, 0)),
            pl.BlockSpec((1, D_MODEL), lambda b, i: (0, 0)),
            pl.BlockSpec((D_MODEL, n), lambda b, i: (0, 0)),
        ],
        out_specs=pl.BlockSpec((1, tm, n), lambda b, i: (b, i, 0)),
        out_shape=jax.ShapeDtypeStruct((nb, s, n), F32),
        compiler_params=_cparams("arbitrary", "arbitrary"),
        name="norm_mod_matmul",
    )(x, mod, gamma, w_bf16)


def _proj_res_body(m_ref, w_ref, x_ref, mod_ref, o_ref, *, gi):
    y = jnp.dot(m_ref[0].astype(BF16), w_ref[...], preferred_element_type=F32)
    o_ref[0] = x_ref[0] + mod_ref[0, gi:gi + 1, :] * y


def _proj_residual(m, w_bf16, x, mod, gi, tm=512):
    nb, s, k = m.shape
    return pl.pallas_call(
        functools.partial(_proj_res_body, gi=gi),
        grid=(nb, s // tm),
        in_specs=[
            pl.BlockSpec((1, tm, k), lambda b, i: (b, i, 0)),
            pl.BlockSpec((k, D_MODEL), lambda b, i: (0, 0)),
            pl.BlockSpec((1, tm, D_MODEL), lambda b, i: (b, i, 0)),
            pl.BlockSpec((1, 6, D_MODEL), lambda b, i: (b, 0, 0)),
        ],
        out_specs=pl.BlockSpec((1, tm, D_MODEL), lambda b, i: (b, i, 0)),
        out_shape=jax.ShapeDtypeStruct((nb, s, D_MODEL), F32),
        compiler_params=_cparams("arbitrary", "arbitrary"),
        name="proj_residual",
    )(m, w_bf16, x, mod)


def _scan_tile(a, b, carry, reverse):
    tt = a.shape[0]
    row = lax.broadcasted_iota(I32, (tt, LANES), 0) % SUBLANES
    for d in (1, 2, 4):
        if reverse:
            shift, keep = tt - d, row < SUBLANES - d
        else:
            shift, keep = d, row >= d
        ap = pltpu.roll(a, shift, 0)
        bp = pltpu.roll(b, shift, 0)
        b = jnp.where(keep, a * bp + b, b)
        a = jnp.where(keep, a * ap, a)
    groups = tt // SUBLANES
    hs = [None] * groups
    order = range(groups - 1, -1, -1) if reverse else range(groups)
    for g in order:
        lo = g * SUBLANES
        h = a[lo:lo + SUBLANES] * carry + b[lo:lo + SUBLANES]
        hs[g] = h
        last = h[0:1] if reverse else h[SUBLANES - 1:SUBLANES]
        carry = jnp.broadcast_to(last, (SUBLANES, LANES))
    return jnp.concatenate(hs, axis=0), carry


def _rglru_body(y_ref, x_ref, cw_ref, cb_ref, gaw_ref, gab_ref, gxw_ref, gxb_ref, nc_ref,
                o_ref, xp, hf, *, s, tt):
    nt = s // tt
    pad = SUBLANES
    zeros = jnp.zeros((pad, LANES), F32)
    xp[0:pad, :] = zeros
    xp[s + pad:s + 2 * pad, :] = zeros

    def copy_tile(i, c):
        r0 = pl.multiple_of(i * tt, tt)
        xp[pl.ds(pl.multiple_of(r0 + pad, SUBLANES), tt), :] = x_ref[0, pl.ds(r0, tt), :]
        return c

    lax.fori_loop(0, nt, copy_tile, 0)

    cw = cw_ref[...]
    cb = cb_ref[...]

    def conv_tile(r0):
        win = xp[pl.ds(r0, tt + 2 * pad), :]
        acc = cb + cw[2:3] * win[pad:pad + tt]
        for k in (0, 1, 3):
            shifted = pltpu.roll(win, (2 - k) % (tt + 2 * pad), 0)
            acc = acc + cw[k:k + 1] * shifted[pad:pad + tt]
        return acc

    def gates(d, xc):
        xb = xc.astype(BF16)
        r = jax.nn.sigmoid(jnp.dot(xb, gaw_ref[d, 0], preferred_element_type=F32) + gab_ref[d])
        ig = jax.nn.sigmoid(jnp.dot(xb, gxw_ref[d, 0], preferred_element_type=F32) + gxb_ref[d])
        a = jnp.exp(nc_ref[d] * r)
        b = jnp.sqrt(1.0 - a * a) * (ig * xc)
        return a, b

    carry0 = jnp.zeros((SUBLANES, LANES), F32)

    def fwd(i, carry):
        r0 = pl.multiple_of(i * tt, tt)
        a, b = gates(0, conv_tile(r0))
        h, carry = _scan_tile(a, b, carry, False)
        hf[pl.ds(r0, tt), :] = h
        return carry

    lax.fori_loop(0, nt, fwd, carry0)

    def bwd(ii, carry):
        r0 = pl.multiple_of((nt - 1 - ii) * tt, tt)
        a, b = gates(1, conv_tile(r0))
        h, carry = _scan_tile(a, b, carry, True)
        o_ref[0, pl.ds(r0, tt), :] = (hf[pl.ds(r0, tt), :] + h) * _gelu(y_ref[0, pl.ds(r0, tt), :])
        return carry

    lax.fori_loop(0, nt, bwd, carry0)


def _rglru_core(u, conv_w, conv_b, gaw, gab, gxw, gxb, negc, tt=256):
    nb, s, _ = u.shape
    ng = RNN_WIDTH // LANES
    return pl.pallas_call(
        functools.partial(_rglru_body, s=s, tt=tt),
        grid=(nb, ng),
        in_specs=[
            pl.BlockSpec((1, s, LANES), lambda b, j: (b, 0, j)),
            pl.BlockSpec((1, s, LANES), lambda b, j: (b, 0, ng + j)),
            pl.BlockSpec((CONV_W, LANES), lambda b, j: (0, j)),
            pl.BlockSpec((1, LANES), lambda b, j: (0, j)),
            pl.BlockSpec((2, 1, LANES, LANES), lambda b, j: (0, j, 0, 0)),
            pl.BlockSpec((2, 1, LANES), lambda b, j: (0, 0, j)),
            pl.BlockSpec((2, 1, LANES, LANES), lambda b, j: (0, j, 0, 0)),
            pl.BlockSpec((2, 1, LANES), lambda b, j: (0, 0, j)),
            pl.BlockSpec((2, 1, LANES), lambda b, j: (0, 0, j)),
        ],
        out_specs=pl.BlockSpec((1, s, LANES), lambda b, j: (b, 0, j)),
        out_shape=jax.ShapeDtypeStruct((nb, s, RNN_WIDTH), F32),
        scratch_shapes=[pltpu.VMEM((s + 2 * SUBLANES, LANES), F32), pltpu.VMEM((s, LANES), F32)],
        compiler_params=_cparams("arbitrary", "arbitrary"),
        name="rglru_core",
    )(u, u, conv_w, conv_b, gaw, gab, gxw, gxb, negc)


def _block_diag_groups(w):
    ng = RNN_WIDTH // LANES
    w = w.reshape(ng, 2, RNN_BLOCK_W, RNN_BLOCK_W)
    out = jnp.zeros((ng, LANES, LANES), w.dtype)
    out = out.at[:, :RNN_BLOCK_W, :RNN_BLOCK_W].set(w[:, 0])
    out = out.at[:, RNN_BLOCK_W:, RNN_BLOCK_W:].set(w[:, 1])
    return out


def _rglru_layer(x, mod, gamma, w_in, conv_w, conv_b, ga_w, ga_b, gx_w, gx_b, lam, w_out):
    u = _norm_mod_matmul(x, mod, gamma, w_in.astype(BF16), sh=0, sc=1)
    gaw = jnp.stack([_block_diag_groups(ga_w[d]) for d in range(2)]).astype(BF16)
    gxw = jnp.stack([_block_diag_groups(gx_w[d]) for d in range(2)]).astype(BF16)
    gab = ga_b.reshape(2, 1, RNN_WIDTH)
    gxb = gx_b.reshape(2, 1, RNN_WIDTH)
    negc = (-LRU_C * jax.nn.softplus(-lam)).reshape(2, 1, RNN_WIDTH)
    m = _rglru_core(u, conv_w, conv_b.reshape(1, RNN_WIDTH), gaw, gab, gxw, gxb, negc)
    return _proj_residual(m, w_out.astype(BF16), x, mod, gi=2)


def _seg_mean(x2, seg_ref):
    hi = x2.astype(BF16)
    lo = (x2 - hi.astype(F32)).astype(BF16)
    return (jnp.dot(hi, seg_ref[...], preferred_element_type=F32)
            + jnp.dot(lo, seg_ref[...], preferred_element_type=F32))


def _rope(x, cos, sin, lane_lo):
    outs = []
    for j in range(x.shape[1] // LANES):
        xt = x[:, j * LANES:(j + 1) * LANES]
        rot = jnp.where(lane_lo, pltpu.roll(xt, LANES - AXIS_DIM // 2, 1), pltpu.roll(xt, AXIS_DIM // 2, 1))
        outs.append(xt * cos + rot * sin)
    return jnp.concatenate(outs, axis=1) if len(outs) > 1 else outs[0]


def _qkv_body(x_ref, mod_ref, g_ref, w_ref, segq_ref, segk_ref, qg_ref, kg_ref, cosq_ref, sinq_ref,
              cosk_ref, sink_ref, q_ref, kt_ref, v_ref, *, tm):
    h = _norm_mod(x_ref[0], g_ref[...], mod_ref[0, 1:2, :], mod_ref[0, 0:1, :])
    qkv = jnp.dot(h.astype(BF16), w_ref[...], preferred_element_type=F32)
    nq = N_HEADS * HEAD_DIM
    nk = N_KV_HEADS * HEAD_DIM
    q = qkv[:, :nq]
    k = qkv[:, nq:nq + nk]
    v = qkv[:, nq + nk:]
    lane = lax.broadcasted_iota(I32, (tm, LANES), 1)
    lane_lo = (lane % AXIS_DIM) < (AXIS_DIM // 2)
    q = q * lax.rsqrt(_seg_mean(q * q, segq_ref) + EPS) * qg_ref[...]
    k = k * lax.rsqrt(_seg_mean(k * k, segk_ref) + EPS) * kg_ref[...]
    q = _rope(q, cosq_ref[...], sinq_ref[...], lane_lo)
    k = _rope(k, cosk_ref[...], sink_ref[...], lane_lo)
    q_ref[0] = q.astype(BF16)
    kt = k.T.astype(BF16)
    for g in range(N_KV_HEADS):
        kt_ref[0, g] = kt[g * HEAD_DIM:(g + 1) * HEAD_DIM, :]
        v_ref[0, g] = v[:, g * HEAD_DIM:(g + 1) * HEAD_DIM].astype(BF16)


def _rope_tables(s):
    rows = s // GRID_W
    row = jnp.repeat(jnp.arange(rows, dtype=F32), GRID_W)
    col = jnp.tile(jnp.arange(GRID_W, dtype=F32), rows)
    inv = ROPE_THETA ** (-jnp.arange(0, AXIS_DIM, 2, dtype=F32) / AXIS_DIM)
    ar = row[:, None] * inv
    ac = col[:, None] * inv
    cos = jnp.concatenate([jnp.cos(ar), jnp.cos(ar), jnp.cos(ac), jnp.cos(ac)], axis=1)
    sin = jnp.concatenate([-jnp.sin(ar), jnp.sin(ar), -jnp.sin(ac), jnp.sin(ac)], axis=1)
    return jnp.tile(cos, (1, LANES // HEAD_DIM)), jnp.tile(sin, (1, LANES // HEAD_DIM))


def _attn_body(q_ref, kt_ref, v_ref, o_ref):
    kt = kt_ref[0, 0]
    v = v_ref[0, 0]
    outs = []
    for hh in range(GROUP):
        qh = q_ref[0, :, hh * HEAD_DIM:(hh + 1) * HEAD_DIM]
        sc = jnp.dot(qh, kt, preferred_element_type=F32)
        m = jnp.max(sc, axis=-1, keepdims=True)
        p = jnp.exp(sc - m)
        l = jnp.sum(p, axis=-1, keepdims=True)
        o = jnp.dot(p.astype(BF16), v, preferred_element_type=F32)
        outs.append(o / l)
    o_ref[0] = jnp.concatenate(outs, axis=1).astype(BF16)


def _attention_layer(x, mod, gamma, w_qkv, q_g, k_g, w_o, tm=256, tq=128):
    nb, s, _ = x.shape
    nq = N_HEADS * HEAD_DIM
    nk = N_KV_HEADS * HEAD_DIM
    seg = jnp.kron(jnp.eye(N_HEADS, dtype=F32), jnp.full((HEAD_DIM, HEAD_DIM), 1.0 / HEAD_DIM, F32)).astype(BF16)
    segk = seg[:nk, :nk]
    cos, sin = _rope_tables(s)
    scale = HEAD_DIM ** -0.5
    q, kt, v = pl.pallas_call(
        functools.partial(_qkv_body, tm=tm),
        grid=(nb, s // tm),
        in_specs=[
            pl.BlockSpec((1, tm, D_MODEL), lambda b, i: (b, i, 0)),
            pl.BlockSpec((1, 6, D_MODEL), lambda b, i: (b, 0, 0)),
            pl.BlockSpec((1, D_MODEL), lambda b, i: (0, 0)),
            pl.BlockSpec((D_MODEL, nq + 2 * nk), lambda b, i: (0, 0)),
            pl.BlockSpec((nq, nq), lambda b, i: (0, 0)),
            pl.BlockSpec((nk, nk), lambda b, i: (0, 0)),
            pl.BlockSpec((1, nq), lambda b, i: (0, 0)),
            pl.BlockSpec((1, nk), lambda b, i: (0, 0)),
            pl.BlockSpec((tm, LANES), lambda b, i: (i, 0)),
            pl.BlockSpec((tm, LANES), lambda b, i: (i, 0)),
            pl.BlockSpec((tm, LANES), lambda b, i: (i, 0)),
            pl.BlockSpec((tm, LANES), lambda b, i: (i, 0)),
        ],
        out_specs=[
            pl.BlockSpec((1, tm, nq), lambda b, i: (b, i, 0)),
            pl.BlockSpec((1, N_KV_HEADS, HEAD_DIM, tm), lambda b, i: (b, 0, 0, i)),
            pl.BlockSpec((1, N_KV_HEADS, tm, HEAD_DIM), lambda b, i: (b, 0, i, 0)),
        ],
        out_shape=[
            jax.ShapeDtypeStruct((nb, s, nq), BF16),
            jax.ShapeDtypeStruct((nb, N_KV_HEADS, HEAD_DIM, s), BF16),
            jax.ShapeDtypeStruct((nb, N_KV_HEADS, s, HEAD_DIM), BF16),
        ],
        compiler_params=_cparams("arbitrary", "arbitrary"),
        name="qkv_rope",
    )(x, mod, gamma, w_qkv.astype(BF16), seg, segk,
      jnp.tile(q_g, N_HEADS).reshape(1, nq), jnp.tile(k_g, N_KV_HEADS).reshape(1, nk),
      cos * scale, sin * scale, cos, sin)

    gw = GROUP * HEAD_DIM
    o = pl.pallas_call(
        _attn_body,
        grid=(nb, N_KV_HEADS, s // tq),
        in_specs=[
            pl.BlockSpec((1, tq, gw), lambda b, g, i: (b, i, g)),
            pl.BlockSpec((1, 1, HEAD_DIM, s), lambda b, g, i: (b, g, 0, 0)),
            pl.BlockSpec((1, 1, s, HEAD_DIM), lambda b, g, i: (b, g, 0, 0)),
        ],
        out_specs=pl.BlockSpec((1, tq, gw), lambda b, g, i: (b, i, g)),
        out_shape=jax.ShapeDtypeStruct((nb, s, nq), BF16),
        compiler_params=_cparams("arbitrary", "arbitrary", "arbitrary"),
        name="attention",
    )(q, kt, v)
    return _proj_residual(o, w_o.astype(BF16), x, mod, gi=2)


def _topk_rows(s, ids, k, id_bound):
    vals, picks = [], []
    for _ in range(k):
        m = jnp.max(s, axis=0, keepdims=True)
        i = jnp.min(jnp.where(s == m, ids, id_bound), axis=0, keepdims=True)
        vals.append(m)
        picks.append(i)
        s = jnp.where(ids == i, -jnp.inf, s)
    return jnp.concatenate(vals, axis=0), jnp.concatenate(picks, axis=0)


def _route_body(x_ref, mod_ref, g_ref, wq_ref, keys_ref, h_ref, e_ref, gate_ref, hb, *, tm):
    hd = pl.program_id(2)

    @pl.when(hd == 0)
    def _():
        h = _norm_mod(x_ref[0], g_ref[...], mod_ref[0, 4:5, :], mod_ref[0, 3:4, :])
        h_ref[0] = h
        hb[...] = h.astype(BF16)

    q = jnp.dot(hb[...], wq_ref[...], preferred_element_type=F32)
    key_ids = lax.broadcasted_iota(I32, (N_KEYS, tm), 0)
    tops = []
    for p in range(2):
        qp = q[:, p * PEER_HALF:(p + 1) * PEER_HALF].astype(BF16)
        st = lax.dot_general(keys_ref[p], qp, (((1,), (1,)), ((), ())), preferred_element_type=F32)
        tops.append(_topk_rows(st, key_ids, PEER_TOPK, N_KEYS))
    (v0, i0), (v1, i1) = tops

    r16 = lax.broadcasted_iota(I32, (PEER_TOPK, tm), 0)
    cand, fid, eid = [], [], []
    for a in range(4):
        cand.append(v0[a:a + 1] + v1)
        fid.append(a * PEER_TOPK + r16)
        eid.append(i0[a:a + 1] * N_KEYS + i1)
    for b in range(3):
        cand.append(jnp.where(r16 >= 4, v0 + v1[b:b + 1], -jnp.inf))
        fid.append(r16 * PEER_TOPK + b)
        eid.append(i0 * N_KEYS + i1[b:b + 1])
    cand = jnp.concatenate(cand, axis=0)
    fid = jnp.concatenate(fid, axis=0)
    eid = jnp.concatenate(eid, axis=0)

    best, chosen = [], []
    for _ in range(PEER_TOPK):
        m = jnp.max(cand, axis=0, keepdims=True)
        f = jnp.min(jnp.where(cand == m, fid, PEER_TOPK * PEER_TOPK), axis=0, keepdims=True)
        hit = fid == f
        best.append(m)
        chosen.append(jnp.max(jnp.where(hit, eid, -1), axis=0, keepdims=True))
        cand = jnp.where(hit, -jnp.inf, cand)
    best = jnp.concatenate(best, axis=0)
    ex = jnp.exp(best - best[0:1])
    gate_ref[0, 0] = ex / jnp.sum(ex, axis=0, keepdims=True)
    e_ref[0, 0] = jnp.concatenate(chosen, axis=0)


def _peer_route(x, mod, gamma, w_query, sub_keys, tm=128):
    nb, s, _ = x.shape
    return pl.pallas_call(
        functools.partial(_route_body, tm=tm),
        grid=(nb, s // tm, PEER_HEADS),
        in_specs=[
            pl.BlockSpec((1, tm, D_MODEL), lambda b, i, h: (b, i, 0)),
            pl.BlockSpec((1, 6, D_MODEL), lambda b, i, h: (b, 0, 0)),
            pl.BlockSpec((1, D_MODEL), lambda b, i, h: (0, 0)),
            pl.BlockSpec((D_MODEL, PEER_QDIM), lambda b, i, h: (0, h)),
            pl.BlockSpec((2, N_KEYS, PEER_HALF), lambda b, i, h: (0, 0, 0)),
        ],
        out_specs=[
            pl.BlockSpec((1, tm, D_MODEL), lambda b, i, h: (b, i, 0)),
            pl.BlockSpec((1, 1, PEER_TOPK, tm), lambda b, i, h: (b, h, 0, i)),
            pl.BlockSpec((1, 1, PEER_TOPK, tm), lambda b, i, h: (b, h, 0, i)),
        ],
        out_shape=[
            jax.ShapeDtypeStruct((nb, s, D_MODEL), F32),
            jax.ShapeDtypeStruct((nb, PEER_HEADS, PEER_TOPK, s), I32),
            jax.ShapeDtypeStruct((nb, PEER_HEADS, PEER_TOPK, s), F32),
        ],
        scratch_shapes=[pltpu.VMEM((tm, D_MODEL), BF16)],
        compiler_params=_cparams("arbitrary", "arbitrary", "arbitrary"),
        name="peer_route",
    )(x, mod, gamma, w_query.astype(BF16), sub_keys.astype(BF16))


def _pack_table(u, v):
    ne = u.shape[0]
    pairs = jnp.stack([u.astype(BF16).reshape(ne * ROWS_PER_EXPERT, LANES),
                       v.astype(BF16).reshape(ne * ROWS_PER_EXPERT, LANES)], axis=-1)
    return lax.bitcast_convert_type(pairs, I32)


def _gather_body(eid_ref, h_ref, g_ref, tab_ref, x_ref, mod_ref, fg_ref, o_ref, buf, sem, po,
                 *, tb, final_norm):
    ne = PEER_PICKS
    rows = ROWS_PER_EXPERT

    ns = GATHER_SLOTS
    ahead = ns - 1

    def issue(t, slot, lo, hi):
        for k in range(lo, hi):
            src = tab_ref.at[pl.ds(pl.multiple_of(eid_ref[t, k] * rows, rows), rows), :]
            dst = buf.at[slot, pl.ds(GATHER_PITCH * k, rows), :]
            pltpu.make_async_copy(src, dst, sem.at[slot]).start(priority=k % 2)

    def wait(slot):
        pltpu.make_async_copy(tab_ref.at[pl.ds(0, ne * rows), :], buf.at[slot, pl.ds(0, ne * rows), :],
                              sem.at[slot]).wait()

    for q in range(ahead):
        issue(q, q, 0, ne)
    early = 4
    halves = 2
    per_half = ne // halves
    even = lax.broadcasted_iota(I32, (1, 2 * LANES), 1) % 2 == 0
    row_id = lax.broadcasted_iota(I32, (SUBLANES, LANES), 0)

    def chunk(slot, c, first=0, count=ne):
        words = buf[slot, pl.ds(first * GATHER_PITCH + c, count, stride=GATHER_PITCH), :]
        return pltpu.bitcast(words, BF16)

    def score_half(t8, j, hf, c, acc):
        xrow = h_ref[pl.ds(t8, SUBLANES), c * LANES:(c + 1) * LANES][j:j + 1]
        return acc + chunk(j % ns, c, hf * per_half, per_half).astype(F32) * xrow

    def lane_sums(acc):
        return jnp.sum(acc.T, axis=0, keepdims=True)

    def coefficients(parts, t8, j):
        s = jnp.concatenate(parts, axis=1)
        g = g_ref[pl.ds(t8, SUBLANES), :][j:j + 1]
        coef = pltpu.roll(jnp.where(even, _gelu(s) * g, 0.0), 1, 1)
        return jnp.broadcast_to(coef, (SUBLANES, 2 * LANES))

    wait(0)
    parts = []
    for hf in range(halves):
        a = jnp.zeros((2 * per_half, LANES), F32)
        for c in range(rows):
            a = score_half(0, 0, hf, c, a)
        parts.append(lane_sums(a))
    coef0 = coefficients(parts, 0, 0)

    def body(it, coef):
        t8 = pl.multiple_of(it * SUBLANES, SUBLANES)
        t8_next = pl.multiple_of(jnp.minimum(t8 + SUBLANES, tb - SUBLANES), SUBLANES)
        acc = [None] * rows
        for j in range(SUBLANES):
            jn = (j + 1) % SUBLANES
            t8n = t8 if j + 1 < SUBLANES else t8_next
            nxt = jnp.minimum(t8 + j + ahead, tb - 1)
            nslot = (j + ahead) % ns
            wait((j + 1) % ns)
            coef_b = coef.astype(BF16)
            step = 0
            parts = []
            for hf in range(halves):
                a = jnp.zeros((2 * per_half, LANES), F32)
                for c in range(rows):
                    issue(nxt, nslot, step * early, (step + 1) * early)
                    step += 1
                    a = score_half(t8n, jn, hf, c, a)
                parts.append(lane_sums(a))
            for c in range(rows):
                issue(nxt, nslot, step * early, (step + 1) * early)
                step += 1
                o = jnp.dot(coef_b, chunk(j % ns, c), preferred_element_type=F32)
                acc[c] = o if j == 0 else jnp.where(row_id == j, o, acc[c])
            issue(nxt, nslot, step * early, ne)
            coef = coefficients(parts, t8n, jn)
        for c in range(rows):
            po[pl.ds(t8, SUBLANES), c * LANES:(c + 1) * LANES] = acc[c]
        return coef

    lax.fori_loop(0, tb // SUBLANES, body, coef0)
    for q in range(1, ahead):
        wait((tb + q) % ns)
    y = x_ref[...] + mod_ref[0, 5:6, :] * po[...]
    if final_norm:
        y = y * lax.rsqrt(jnp.mean(y * y, axis=-1, keepdims=True) + EPS) * fg_ref[...]
    o_ref[...] = y


def _peer_gather(eid, h, g, table, x, mod, final_g, seq_len, final_norm, tok_offset, tb=GATHER_TOKENS):
    t = h.shape[0]
    return pl.pallas_call(
        functools.partial(_gather_body, tb=tb, final_norm=final_norm),
        grid=(t // tb,),
        in_specs=[
            pl.BlockSpec((tb, PEER_PICKS), lambda i: (i, 0), memory_space=pltpu.SMEM),
            pl.BlockSpec((tb, D_MODEL), lambda i: (i, 0)),
            pl.BlockSpec((tb, 2 * PEER_PICKS), lambda i: (i, 0)),
            pl.BlockSpec(memory_space=pl.ANY),
            pl.BlockSpec((tb, D_MODEL), lambda i: (i, 0)),
            pl.BlockSpec((1, 6, D_MODEL), lambda i: ((tok_offset + i * tb) // seq_len, 0, 0)),
            pl.BlockSpec((1, D_MODEL), lambda i: (0, 0)),
        ],
        out_specs=pl.BlockSpec((tb, D_MODEL), lambda i: (i, 0)),
        out_shape=jax.ShapeDtypeStruct((t, D_MODEL), F32),
        scratch_shapes=[
            pltpu.VMEM((GATHER_SLOTS, PEER_PICKS * GATHER_PITCH, LANES), I32),
            pltpu.SemaphoreType.DMA((GATHER_SLOTS,)),
            pltpu.VMEM((tb, D_MODEL), F32),
        ],
        compiler_params=_cparams("arbitrary"),
        name="peer_gather",
    )(eid, h, g, table, x, mod, final_g)


SC_LANES = 16
SC_WORKERS = 32
SC_GROUP = 8
SC_TOKENS = 8192
SC_CHUNKS = D_MODEL // SC_LANES


def _sc_body(tab_hbm, eid_hbm, g_hbm, h_hbm, out_hbm, idx_v, g_v, x_v, o_v, rows0, rows1,
             sem0, sem1, *, tpw):
    nl = SC_LANES
    wid = lax.axis_index("s") * 2 + lax.axis_index("c")
    lane = lax.iota(I32, nl)

    def permute(x, idx):
        return jnp.take_along_axis(x, idx, axis=0, mode="promise_in_bounds")
    rows = (rows0, rows1)
    sems = (sem0, sem1)
    pairs = SC_GROUP * PEER_HEADS // 2

    def gather(tok, hd, slot):
        return pltpu.make_async_copy(tab_hbm.at[idx_v.at[tok, hd]], rows[slot], sems[slot])

    def compute(tok, hd, slot, t):
        rv = rows[slot]

        @pl.when(hd == 0)
        def _():
            def zero(c, cc):
                o_v[pl.ds(c * nl, nl)] = jnp.zeros((nl,), F32)
                return cc
            lax.fori_loop(0, SC_CHUNKS, zero, 0)

        def score(c, accs):
            x = x_v[tok, pl.ds(c * nl, nl)]
            out = []
            for kk in range(PEER_TOPK):
                u = plsc.bitcast(lax.shift_left(rv[kk, pl.ds(c * nl, nl)], 16), F32)
                out.append(accs[kk] + u * x)
            return tuple(out)

        accs = lax.fori_loop(0, SC_CHUNKS, score, tuple(jnp.zeros((nl,), F32) for _ in range(PEER_TOPK)))
        vecs = list(accs)
        d = 1
        while len(vecs) > 1:
            partner = lane ^ d
            take_lo = (lane & d) == 0
            nxt_vecs = []
            for i in range(0, len(vecs), 2):
                a = vecs[i] + permute(vecs[i], partner)
                b = vecs[i + 1] + permute(vecs[i + 1], partner)
                nxt_vecs.append(jnp.where(take_lo, a, b))
            vecs = nxt_vecs
            d *= 2
        s = vecs[0]
        z = 0.7978845608028654 * (s + 0.044715 * s * s * s)
        act = s * (1.0 - 1.0 / (jnp.exp(2.0 * z) + 1.0))
        coef = act * g_v[tok, hd, :]
        coefs = [permute(coef, jnp.full((nl,), kk, I32)) for kk in range(PEER_TOPK)]

        def combine(c, cc):
            o = o_v[pl.ds(c * nl, nl)]
            for kk in range(PEER_TOPK):
                v = plsc.bitcast(lax.bitwise_and(rv[kk, pl.ds(c * nl, nl)], jnp.int32(-65536)), F32)
                o = o + coefs[kk] * v
            o_v[pl.ds(c * nl, nl)] = o
            return cc

        lax.fori_loop(0, SC_CHUNKS, combine, 0)

        @pl.when(hd == PEER_HEADS - 1)
        def _():
            pltpu.sync_copy(o_v, out_hbm.at[t])

    def group(gi, carry):
        t0 = wid * tpw + gi * SC_GROUP
        pltpu.sync_copy(eid_hbm.at[pl.ds(t0, SC_GROUP)], idx_v)
        pltpu.sync_copy(g_hbm.at[pl.ds(t0, SC_GROUP)], g_v)
        pltpu.sync_copy(h_hbm.at[pl.ds(t0, SC_GROUP)], x_v)
        gather(0, 0, 0).start()

        def pair(pi, cc):
            j0 = pi * 2
            tok, hd = j0 // PEER_HEADS, j0 % PEER_HEADS
            gather(tok, hd + 1, 1).start()
            gather(tok, hd, 0).wait()
            compute(tok, hd, 0, t0 + tok)
            j2 = j0 + 2

            @pl.when(j2 < 2 * pairs)
            def _():
                gather(j2 // PEER_HEADS, j2 % PEER_HEADS, 0).start()

            gather(tok, hd + 1, 1).wait()
            compute(tok, hd + 1, 1, t0 + tok)
            return cc

        lax.fori_loop(0, pairs, pair, 0)
        return carry

    lax.fori_loop(0, tpw // SC_GROUP, group, 0)


def _peer_sc(table, eid, gate, h):
    t = h.shape[0]
    tpw = t // SC_WORKERS
    cp = pltpu.CompilerParams()
    if "needs_layout_passes" in pltpu.CompilerParams.__dataclass_fields__:
        cp = dataclasses.replace(cp, needs_layout_passes=False)
    run = pl.kernel(
        functools.partial(_sc_body, tpw=tpw),
        out_type=jax.ShapeDtypeStruct((t, D_MODEL), F32),
        mesh=plsc.VectorSubcoreMesh(core_axis_name="c", subcore_axis_name="s"),
        scratch_types=[
            pltpu.VMEM((SC_GROUP, PEER_HEADS, PEER_TOPK), I32),
            pltpu.VMEM((SC_GROUP, PEER_HEADS, PEER_TOPK), F32),
            pltpu.VMEM((SC_GROUP, D_MODEL), F32),
            pltpu.VMEM((D_MODEL,), F32),
            pltpu.VMEM((PEER_TOPK, D_MODEL), I32),
            pltpu.VMEM((PEER_TOPK, D_MODEL), I32),
            pltpu.SemaphoreType.DMA,
            pltpu.SemaphoreType.DMA,
        ],
        compiler_params=cp,
        name="peer_sc",
    )
    return run(table, eid.reshape(t, PEER_HEADS, PEER_TOPK), gate.reshape(t, PEER_HEADS, PEER_TOPK), h)


def _finish_body(x_ref, po_ref, mod_ref, fg_ref, o_ref, *, final_norm):
    y = x_ref[...] + mod_ref[0, 5:6, :] * po_ref[...]
    if final_norm:
        y = y * lax.rsqrt(jnp.mean(y * y, axis=-1, keepdims=True) + EPS) * fg_ref[...]
    o_ref[...] = y


def _peer_finish(x, po, mod, final_g, seq_len, final_norm, tm=512):
    t = x.shape[0]
    return pl.pallas_call(
        functools.partial(_finish_body, final_norm=final_norm),
        grid=(t // tm,),
        in_specs=[
            pl.BlockSpec((tm, D_MODEL), lambda i: (i, 0)),
            pl.BlockSpec((tm, D_MODEL), lambda i: (i, 0)),
            pl.BlockSpec((1, 6, D_MODEL), lambda i: ((i * tm) // seq_len, 0, 0)),
            pl.BlockSpec((1, D_MODEL), lambda i: (0, 0)),
        ],
        out_specs=pl.BlockSpec((tm, D_MODEL), lambda i: (i, 0)),
        out_shape=jax.ShapeDtypeStruct((t, D_MODEL), F32),
        compiler_params=_cparams("arbitrary"),
        name="peer_finish",
    )(x, po, mod, final_g)


def _peer_layer(x, mod, gamma, w_query, sub_keys, table, final_g, final_norm):
    nb, s, _ = x.shape
    t = nb * s
    h, e_t, g_t = _peer_route(x, mod, gamma, w_query, sub_keys)
    eid = e_t.transpose(0, 3, 1, 2).reshape(t, PEER_PICKS)
    gate = g_t.transpose(0, 3, 1, 2).reshape(t, PEER_PICKS)
    hf = h.reshape(t, D_MODEL)
    xf = x.reshape(t, D_MODEL)
    t_sc = SC_TOKENS
    gate_tc = gate[t_sc:]
    gate_tc = jnp.stack([gate_tc, jnp.zeros_like(gate_tc)], axis=-1).reshape(t - t_sc, 2 * PEER_PICKS)
    y_tc = _peer_gather(eid[t_sc:], hf[t_sc:], gate_tc, table, xf[t_sc:], mod, final_g, s, final_norm, t_sc)
    po = _peer_sc(table.reshape(N_EXPERTS, D_MODEL), eid[:t_sc], gate[:t_sc], hf[:t_sc])
    y_sc = _peer_finish(xf[:t_sc], po, mod, final_g, s, final_norm)
    return jnp.concatenate([y_sc, y_tc], axis=0).reshape(nb, s, D_MODEL)


def kernel(x_prompt, x_sample, c_prompt, c_sample, ln_mix_g, ln_ffn_g, w_mod, b_mod, rec_w_in, rec_conv_w,
           rec_conv_b, rec_ga_w, rec_ga_b, rec_gx_w, rec_gx_b, rec_lam, rec_w_out, att_w_qkv, att_q_g,
           att_k_g, att_w_o, peer_w_query, peer_sub_keys, peer_u, peer_v, final_g):
    tables = [_pack_table(peer_u[i], peer_v[i]) for i in range(DEPTH)]
    fg = final_g.reshape(1, D_MODEL)

    def trunk(x, c):
        for i in range(DEPTH):
            mod = _mod(c, w_mod[i], b_mod[i])
            gm = ln_mix_g[i].reshape(1, D_MODEL)
            gf = ln_ffn_g[i].reshape(1, D_MODEL)
            j = i // 2
            if i % 2 == 0:
                x = _rglru_layer(x, mod, gm, rec_w_in[j], rec_conv_w[j], rec_conv_b[j], rec_ga_w[j],
                                 rec_ga_b[j], rec_gx_w[j], rec_gx_b[j], rec_lam[j], rec_w_out[j])
            else:
                x = _attention_layer(x, mod, gm, att_w_qkv[j], att_q_g[j], att_k_g[j], att_w_o[j])
            x = _peer_layer(x, mod, gf, peer_w_query[i], peer_sub_keys[i], tables[i], fg,
                            final_norm=(i == DEPTH - 1))
        return x

    return (trunk(x_prompt, c_prompt), trunk(x_sample, c_sample))
```

```python
import dataclasses
import functools

import jax
import jax.numpy as jnp
from jax import lax
from jax.experimental import pallas as pl
from jax.experimental.pallas import tpu as pltpu
from jax.experimental.pallas import tpu_sc as plsc

F32 = jnp.float32
BF16 = jnp.bfloat16
I32 = jnp.int32

D_MODEL = 1024
DEPTH = 2
GRID_W = 64
EPS = 1e-6
RNN_WIDTH = D_MODEL
RNN_BLOCKS = 16
RNN_BLOCK_W = RNN_WIDTH // RNN_BLOCKS
CONV_W = 4
LRU_C = 8.0
N_HEADS = 16
N_KV_HEADS = 4
HEAD_DIM = D_MODEL // N_HEADS
GROUP = N_HEADS // N_KV_HEADS
AXIS_DIM = HEAD_DIM // 2
ROPE_THETA = 10000.0
N_KEYS = 128
N_EXPERTS = N_KEYS * N_KEYS
PEER_HEADS = 8
PEER_TOPK = 16
PEER_QDIM = 256
PEER_HALF = PEER_QDIM // 2
PEER_PICKS = PEER_HEADS * PEER_TOPK

LANES = 128
SUBLANES = 8
VMEM_LIMIT = 48 * 1024 * 1024

GATHER_PITCH = 12
GATHER_TOKENS = 64
GATHER_SLOTS = 4
ROWS_PER_EXPERT = D_MODEL // LANES


def _cparams(*sem):
    return pltpu.CompilerParams(dimension_semantics=sem, vmem_limit_bytes=VMEM_LIMIT)


def _gelu(x):
    return jax.nn.gelu(x)


def _norm_mod(x, gamma, scale, shift):
    ms = jnp.mean(x * x, axis=-1, keepdims=True)
    y = x * lax.rsqrt(ms + EPS) * gamma
    return y * (1.0 + scale) + shift


def _mod_body(c_ref, w_ref, b_ref, o_ref):
    c = c_ref[...]
    s = c * jax.nn.sigmoid(c)
    o_ref[...] = jnp.dot(s, w_ref[...], preferred_element_type=F32) + b_ref[...]


def _mod(c, w, b):
    nb, n = c.shape[0], w.shape[1]
    tn = 1536
    out = pl.pallas_call(
        _mod_body,
        grid=(n // tn,),
        in_specs=[
            pl.BlockSpec((nb, D_MODEL), lambda j: (0, 0)),
            pl.BlockSpec((D_MODEL, tn), lambda j: (0, j)),
            pl.BlockSpec((1, tn), lambda j: (0, j)),
        ],
        out_specs=pl.BlockSpec((nb, tn), lambda j: (0, j)),
        out_shape=jax.ShapeDtypeStruct((nb, n), F32),
        compiler_params=_cparams("arbitrary"),
        name="adaln_mod",
    )(c, w, b.reshape(1, n))
    return out.reshape(nb, 6, D_MODEL)


def _nmm_body(x_ref, mod_ref, g_ref, w_ref, o_ref, *, sh, sc):
    h = _norm_mod(x_ref[0], g_ref[...], mod_ref[0, sc:sc + 1, :], mod_ref[0, sh:sh + 1, :])
    o_ref[0] = jnp.dot(h.astype(BF16), w_ref[...], preferred_element_type=F32)


def _norm_mod_matmul(x, mod, gamma, w_bf16, sh, sc, tm=512):
    nb, s, _ = x.shape
    n = w_bf16.shape[1]
    return pl.pallas_call(
        functools.partial(_nmm_body, sh=sh, sc=sc),
        grid=(nb, s // tm),
        in_specs=[
            pl.BlockSpec((1, tm, D_MODEL), lambda b, i: (b, i, 0)),
            pl.BlockSpec((1, 6, D_MODEL), lambda b, i: (b, 0, 0)),
            pl.BlockSpec((1, D_MODEL), lambda b, i: (0, 0)),
            pl.BlockSpec((D_MODEL, n), lambda b, i: (0, 0)),
        ],
        out_specs=pl.BlockSpec((1, tm, n), lambda b, i: (b, i, 0)),
        out_shape=jax.ShapeDtypeStruct((nb, s, n), F32),
        compiler_params=_cparams("arbitrary", "arbitrary"),
        name="norm_mod_matmul",
    )(x, mod, gamma, w_bf16)


def _proj_res_body(m_ref, w_ref, x_ref, mod_ref, o_ref, *, gi):
    y = jnp.dot(m_ref[0].astype(BF16), w_ref[...], preferred_element_type=F32)
    o_ref[0] = x_ref[0] + mod_ref[0, gi:gi + 1, :] * y


def _proj_residual(m, w_bf16, x, mod, gi, tm=512):
    nb, s, k = m.shape
    return pl.pallas_call(
        functools.partial(_proj_res_body, gi=gi),
        grid=(nb, s // tm),
        in_specs=[
            pl.BlockSpec((1, tm, k), lambda b, i: (b, i, 0)),
            pl.BlockSpec((k, D_MODEL), lambda b, i: (0, 0)),
            pl.BlockSpec((1, tm, D_MODEL), lambda b, i: (b, i, 0)),
            pl.BlockSpec((1, 6, D_MODEL), lambda b, i: (b, 0, 0)),
        ],
        out_specs=pl.BlockSpec((1, tm, D_MODEL), lambda b, i: (b, i, 0)),
        out_shape=jax.ShapeDtypeStruct((nb, s, D_MODEL), F32),
        compiler_params=_cparams("arbitrary", "arbitrary"),
        name="proj_residual",
    )(m, w_bf16, x, mod)


def _scan_tile(a, b, carry, reverse):
    tt = a.shape[0]
    row = lax.broadcasted_iota(I32, (tt, LANES), 0) % SUBLANES
    for d in (1, 2, 4):
        if reverse:
            shift, keep = tt - d, row < SUBLANES - d
        else:
            shift, keep = d, row >= d
        ap = pltpu.roll(a, shift, 0)
        bp = pltpu.roll(b, shift, 0)
        b = jnp.where(keep, a * bp + b, b)
        a = jnp.where(keep, a * ap, a)
    groups = tt // SUBLANES
    hs = [None] * groups
    order = range(groups - 1, -1, -1) if reverse else range(groups)
    for g in order:
        lo = g * SUBLANES
        h = a[lo:lo + SUBLANES] * carry + b[lo:lo + SUBLANES]
        hs[g] = h
        last = h[0:1] if reverse else h[SUBLANES - 1:SUBLANES]
        carry = jnp.broadcast_to(last, (SUBLANES, LANES))
    return jnp.concatenate(hs, axis=0), carry


def _rglru_body(y_ref, x_ref, cw_ref, cb_ref, gaw_ref, gab_ref, gxw_ref, gxb_ref, nc_ref,
                o_ref, xp, hf, *, s, tt):
    nt = s // tt
    pad = SUBLANES
    zeros = jnp.zeros((pad, LANES), F32)
    xp[0:pad, :] = zeros
    xp[s + pad:s + 2 * pad, :] = zeros

    def copy_tile(i, c):
        r0 = pl.multiple_of(i * tt, tt)
        xp[pl.ds(pl.multiple_of(r0 + pad, SUBLANES), tt), :] = x_ref[0, pl.ds(r0, tt), :]
        return c

    lax.fori_loop(0, nt, copy_tile, 0)

    cw = cw_ref[...]
    cb = cb_ref[...]

    def conv_tile(r0):
        win = xp[pl.ds(r0, tt + 2 * pad), :]
        acc = cb + cw[2:3] * win[pad:pad + tt]
        for k in (0, 1, 3):
            shifted = pltpu.roll(win, (2 - k) % (tt + 2 * pad), 0)
            acc = acc + cw[k:k + 1] * shifted[pad:pad + tt]
        return acc

    def gates(d, xc):
        xb = xc.astype(BF16)
        r = jax.nn.sigmoid(jnp.dot(xb, gaw_ref[d, 0], preferred_element_type=F32) + gab_ref[d])
        ig = jax.nn.sigmoid(jnp.dot(xb, gxw_ref[d, 0], preferred_element_type=F32) + gxb_ref[d])
        a = jnp.exp(nc_ref[d] * r)
        b = jnp.sqrt(1.0 - a * a) * (ig * xc)
        return a, b

    carry0 = jnp.zeros((SUBLANES, LANES), F32)

    def fwd(i, carry):
        r0 = pl.multiple_of(i * tt, tt)
        a, b = gates(0, conv_tile(r0))
        h, carry = _scan_tile(a, b, carry, False)
        hf[pl.ds(r0, tt), :] = h
        return carry

    lax.fori_loop(0, nt, fwd, carry0)

    def bwd(ii, carry):
        r0 = pl.multiple_of((nt - 1 - ii) * tt, tt)
        a, b = gates(1, conv_tile(r0))
        h, carry = _scan_tile(a, b, carry, True)
        o_ref[0, pl.ds(r0, tt), :] = (hf[pl.ds(r0, tt), :] + h) * _gelu(y_ref[0, pl.ds(r0, tt), :])
        return carry

    lax.fori_loop(0, nt, bwd, carry0)


def _rglru_core(u, conv_w, conv_b, gaw, gab, gxw, gxb, negc, tt=256):
    nb, s, _ = u.shape
    ng = RNN_WIDTH // LANES
    return pl.pallas_call(
        functools.partial(_rglru_body, s=s, tt=tt),
        grid=(nb, ng),
        in_specs=[
            pl.BlockSpec((1, s, LANES), lambda b, j: (b, 0, j)),
            pl.BlockSpec((1, s, LANES), lambda b, j: (b, 0, ng + j)),
            pl.BlockSpec((CONV_W, LANES), lambda b, j: (0, j)),
            pl.BlockSpec((1, LANES), lambda b, j: (0, j)),
            pl.BlockSpec((2, 1, LANES, LANES), lambda b, j: (0, j, 0, 0)),
            pl.BlockSpec((2, 1, LANES), lambda b, j: (0, 0, j)),
            pl.BlockSpec((2, 1, LANES, LANES), lambda b, j: (0, j, 0, 0)),
            pl.BlockSpec((2, 1, LANES), lambda b, j: (0, 0, j)),
            pl.BlockSpec((2, 1, LANES), lambda b, j: (0, 0, j)),
        ],
        out_specs=pl.BlockSpec((1, s, LANES), lambda b, j: (b, 0, j)),
        out_shape=jax.ShapeDtypeStruct((nb, s, RNN_WIDTH), F32),
        scratch_shapes=[pltpu.VMEM((s + 2 * SUBLANES, LANES), F32), pltpu.VMEM((s, LANES), F32)],
        compiler_params=_cparams("arbitrary", "arbitrary"),
        name="rglru_core",
    )(u, u, conv_w, conv_b, gaw, gab, gxw, gxb, negc)


def _block_diag_groups(w):
    ng = RNN_WIDTH // LANES
    w = w.reshape(ng, 2, RNN_BLOCK_W, RNN_BLOCK_W)
    out = jnp.zeros((ng, LANES, LANES), w.dtype)
    out = out.at[:, :RNN_BLOCK_W, :RNN_BLOCK_W].set(w[:, 0])
    out = out.at[:, RNN_BLOCK_W:, RNN_BLOCK_W:].set(w[:, 1])
    return out


def _rglru_layer(x, mod, gamma, w_in, conv_w, conv_b, ga_w, ga_b, gx_w, gx_b, lam, w_out):
    u = _norm_mod_matmul(x, mod, gamma, w_in.astype(BF16), sh=0, sc=1)
    gaw = jnp.stack([_block_diag_groups(ga_w[d]) for d in range(2)]).astype(BF16)
    gxw = jnp.stack([_block_diag_groups(gx_w[d]) for d in range(2)]).astype(BF16)
    gab = ga_b.reshape(2, 1, RNN_WIDTH)
    gxb = gx_b.reshape(2, 1, RNN_WIDTH)
    negc = (-LRU_C * jax.nn.softplus(-lam)).reshape(2, 1, RNN_WIDTH)
    m = _rglru_core(u, conv_w, conv_b.reshape(1, RNN_WIDTH), gaw, gab, gxw, gxb, negc)
    return _proj_residual(m, w_out.astype(BF16), x, mod, gi=2)


def _seg_mean(x2, seg_ref):
    hi = x2.astype(BF16)
    lo = (x2 - hi.astype(F32)).astype(BF16)
    return (jnp.dot(hi, seg_ref[...], preferred_element_type=F32)
            + jnp.dot(lo, seg_ref[...], preferred_element_type=F32))


def _rope(x, cos, sin, lane_lo):
    outs = []
    for j in range(x.shape[1] // LANES):
        xt = x[:, j * LANES:(j + 1) * LANES]
        rot = jnp.where(lane_lo, pltpu.roll(xt, LANES - AXIS_DIM // 2, 1), pltpu.roll(xt, AXIS_DIM // 2, 1))
        outs.append(xt * cos + rot * sin)
    return jnp.concatenate(outs, axis=1) if len(outs) > 1 else outs[0]


def _qkv_body(x_ref, mod_ref, g_ref, w_ref, segq_ref, segk_ref, qg_ref, kg_ref, cosq_ref, sinq_ref,
              cosk_ref, sink_ref, q_ref, kt_ref, v_ref, *, tm):
    h = _norm_mod(x_ref[0], g_ref[...], mod_ref[0, 1:2, :], mod_ref[0, 0:1, :])
    qkv = jnp.dot(h.astype(BF16), w_ref[...], preferred_element_type=F32)
    nq = N_HEADS * HEAD_DIM
    nk = N_KV_HEADS * HEAD_DIM
    q = qkv[:, :nq]
    k = qkv[:, nq:nq + nk]
    v = qkv[:, nq + nk:]
    lane = lax.broadcasted_iota(I32, (tm, LANES), 1)
    lane_lo = (lane % AXIS_DIM) < (AXIS_DIM // 2)
    q = q * lax.rsqrt(_seg_mean(q * q, segq_ref) + EPS) * qg_ref[...]
    k = k * lax.rsqrt(_seg_mean(k * k, segk_ref) + EPS) * kg_ref[...]
    q = _rope(q, cosq_ref[...], sinq_ref[...], lane_lo)
    k = _rope(k, cosk_ref[...], sink_ref[...], lane_lo)
    q_ref[0] = q.astype(BF16)
    kt = k.T.astype(BF16)
    for g in range(N_KV_HEADS):
        kt_ref[0, g] = kt[g * HEAD_DIM:(g + 1) * HEAD_DIM, :]
        v_ref[0, g] = v[:, g * HEAD_DIM:(g + 1) * HEAD_DIM].astype(BF16)


def _rope_tables(s):
    rows = s // GRID_W
    row = jnp.repeat(jnp.arange(rows, dtype=F32), GRID_W)
    col = jnp.tile(jnp.arange(GRID_W, dtype=F32), rows)
    inv = ROPE_THETA ** (-jnp.arange(0, AXIS_DIM, 2, dtype=F32) / AXIS_DIM)
    ar = row[:, None] * inv
    ac = col[:, None] * inv
    cos = jnp.concatenate([jnp.cos(ar), jnp.cos(ar), jnp.cos(ac), jnp.cos(ac)], axis=1)
    sin = jnp.concatenate([-jnp.sin(ar), jnp.sin(ar), -jnp.sin(ac), jnp.sin(ac)], axis=1)
    return jnp.tile(cos, (1, LANES // HEAD_DIM)), jnp.tile(sin, (1, LANES // HEAD_DIM))


def _attn_body(q_ref, kt_ref, v_ref, o_ref):
    kt = kt_ref[0, 0]
    v = v_ref[0, 0]
    outs = []
    for hh in range(GROUP):
        qh = q_ref[0, :, hh * HEAD_DIM:(hh + 1) * HEAD_DIM]
        sc = jnp.dot(qh, kt, preferred_element_type=F32)
        m = jnp.max(sc, axis=-1, keepdims=True)
        p = jnp.exp(sc - m)
        l = jnp.sum(p, axis=-1, keepdims=True)
        o = jnp.dot(p.astype(BF16), v, preferred_element_type=F32)
        outs.append(o / l)
    o_ref[0] = jnp.concatenate(outs, axis=1).astype(BF16)


def _attention_layer(x, mod, gamma, w_qkv, q_g, k_g, w_o, tm=256, tq=128):
    nb, s, _ = x.shape
    nq = N_HEADS * HEAD_DIM
    nk = N_KV_HEADS * HEAD_DIM
    seg = jnp.kron(jnp.eye(N_HEADS, dtype=F32), jnp.full((HEAD_DIM, HEAD_DIM), 1.0 / HEAD_DIM, F32)).astype(BF16)
    segk = seg[:nk, :nk]
    cos, sin = _rope_tables(s)
    scale = HEAD_DIM ** -0.5
    q, kt, v = pl.pallas_call(
        functools.partial(_qkv_body, tm=tm),
        grid=(nb, s // tm),
        in_specs=[
            pl.BlockSpec((1, tm, D_MODEL), lambda b, i: (b, i, 0)),
            pl.BlockSpec((1, 6, D_MODEL), lambda b, i: (b, 0, 0)),
            pl.BlockSpec((1, D_MODEL), lambda b, i: (0, 0)),
            pl.BlockSpec((D_MODEL, nq + 2 * nk), lambda b, i: (0, 0)),
            pl.BlockSpec((nq, nq), lambda b, i: (0, 0)),
            pl.BlockSpec((nk, nk), lambda b, i: (0, 0)),
            pl.BlockSpec((1, nq), lambda b, i: (0, 0)),
            pl.BlockSpec((1, nk), lambda b, i: (0, 0)),
            pl.BlockSpec((tm, LANES), lambda b, i: (i, 0)),
            pl.BlockSpec((tm, LANES), lambda b, i: (i, 0)),
            pl.BlockSpec((tm, LANES), lambda b, i: (i, 0)),
            pl.BlockSpec((tm, LANES), lambda b, i: (i, 0)),
        ],
        out_specs=[
            pl.BlockSpec((1, tm, nq), lambda b, i: (b, i, 0)),
            pl.BlockSpec((1, N_KV_HEADS, HEAD_DIM, tm), lambda b, i: (b, 0, 0, i)),
            pl.BlockSpec((1, N_KV_HEADS, tm, HEAD_DIM), lambda b, i: (b, 0, i, 0)),
        ],
        out_shape=[
            jax.ShapeDtypeStruct((nb, s, nq), BF16),
            jax.ShapeDtypeStruct((nb, N_KV_HEADS, HEAD_DIM, s), BF16),
            jax.ShapeDtypeStruct((nb, N_KV_HEADS, s, HEAD_DIM), BF16),
        ],
        compiler_params=_cparams("arbitrary", "arbitrary"),
        name="qkv_rope",
    )(x, mod, gamma, w_qkv.astype(BF16), seg, segk,
      jnp.tile(q_g, N_HEADS).reshape(1, nq), jnp.tile(k_g, N_KV_HEADS).reshape(1, nk),
      cos * scale, sin * scale, cos, sin)

    gw = GROUP * HEAD_DIM
    o = pl.pallas_call(
        _attn_body,
        grid=(nb, N_KV_HEADS, s // tq),
        in_specs=[
            pl.BlockSpec((1, tq, gw), lambda b, g, i: (b, i, g)),
            pl.BlockSpec((1, 1, HEAD_DIM, s), lambda b, g, i: (b, g, 0, 0)),
            pl.BlockSpec((1, 1, s, HEAD_DIM), lambda b, g, i: (b, g, 0, 0)),
        ],
        out_specs=pl.BlockSpec((1, tq, gw), lambda b, g, i: (b, i, g)),
        out_shape=jax.ShapeDtypeStruct((nb, s, nq), BF16),
        compiler_params=_cparams("arbitrary", "arbitrary", "arbitrary"),
        name="attention",
    )(q, kt, v)
    return _proj_residual(o, w_o.astype(BF16), x, mod, gi=2)


def _topk_rows(s, ids, k, id_bound):
    vals, picks = [], []
    for _ in range(k):
        m = jnp.max(s, axis=0, keepdims=True)
        i = jnp.min(jnp.where(s == m, ids, id_bound), axis=0, keepdims=True)
        vals.append(m)
        picks.append(i)
        s = jnp.where(ids == i, -jnp.inf, s)
    return jnp.concatenate(vals, axis=0), jnp.concatenate(picks, axis=0)


def _route_body(x_ref, mod_ref, g_ref, wq_ref, keys_ref, h_ref, e_ref, gate_ref, hb, *, tm):
    hd = pl.program_id(2)

    @pl.when(hd == 0)
    def _():
        h = _norm_mod(x_ref[0], g_ref[...], mod_ref[0, 4:5, :], mod_ref[0, 3:4, :])
        h_ref[0] = h
        hb[...] = h.astype(BF16)

    q = jnp.dot(hb[...], wq_ref[...], preferred_element_type=F32)
    key_ids = lax.broadcasted_iota(I32, (N_KEYS, tm), 0)
    tops = []
    for p in range(2):
        qp = q[:, p * PEER_HALF:(p + 1) * PEER_HALF].astype(BF16)
        st = lax.dot_general(keys_ref[p], qp, (((1,), (1,)), ((), ())), preferred_element_type=F32)
        tops.append(_topk_rows(st, key_ids, PEER_TOPK, N_KEYS))
    (v0, i0), (v1, i1) = tops

    r16 = lax.broadcasted_iota(I32, (PEER_TOPK, tm), 0)
    cand, fid, eid = [], [], []
    for a in range(4):
        cand.append(v0[a:a + 1] + v1)
        fid.append(a * PEER_TOPK + r16)
        eid.append(i0[a:a + 1] * N_KEYS + i1)
    for b in range(3):
        cand.append(jnp.where(r16 >= 4, v0 + v1[b:b + 1], -jnp.inf))
        fid.append(r16 * PEER_TOPK + b)
        eid.append(i0 * N_KEYS + i1[b:b + 1])
    cand = jnp.concatenate(cand, axis=0)
    fid = jnp.concatenate(fid, axis=0)
    eid = jnp.concatenate(eid, axis=0)

    best, chosen = [], []
    for _ in range(PEER_TOPK):
        m = jnp.max(cand, axis=0, keepdims=True)
        f = jnp.min(jnp.where(cand == m, fid, PEER_TOPK * PEER_TOPK), axis=0, keepdims=True)
        hit = fid == f
        best.append(m)
        chosen.append(jnp.max(jnp.where(hit, eid, -1), axis=0, keepdims=True))
        cand = jnp.where(hit, -jnp.inf, cand)
    best = jnp.concatenate(best, axis=0)
    ex = jnp.exp(best - best[0:1])
    gate_ref[0, 0] = ex / jnp.sum(ex, axis=0, keepdims=True)
    e_ref[0, 0] = jnp.concatenate(chosen, axis=0)


def _peer_route(x, mod, gamma, w_query, sub_keys, tm=128):
    nb, s, _ = x.shape
    return pl.pallas_call(
        functools.partial(_route_body, tm=tm),
        grid=(nb, s // tm, PEER_HEADS),
        in_specs=[
            pl.BlockSpec((1, tm, D_MODEL), lambda b, i, h: (b, i, 0)),
            pl.BlockSpec((1, 6, D_MODEL), lambda b, i, h: (b, 0, 0)),
            pl.BlockSpec((1, D_MODEL), lambda b, i, h: (0, 0)),
            pl.BlockSpec((D_MODEL, PEER_QDIM), lambda b, i, h: (0, h)),
            pl.BlockSpec((2, N_KEYS, PEER_HALF), lambda b, i, h: (0, 0, 0)),
        ],
        out_specs=[
            pl.BlockSpec((1, tm, D_MODEL), lambda b, i, h: (b, i, 0)),
            pl.BlockSpec((1, 1, PEER_TOPK, tm), lambda b, i, h: (b, h, 0, i)),
            pl.BlockSpec((1, 1, PEER_TOPK, tm), lambda b, i, h: (b, h, 0, i)),
        ],
        out_shape=[
            jax.ShapeDtypeStruct((nb, s, D_MODEL), F32),
            jax.ShapeDtypeStruct((nb, PEER_HEADS, PEER_TOPK, s), I32),
            jax.ShapeDtypeStruct((nb, PEER_HEADS, PEER_TOPK, s), F32),
        ],
        scratch_shapes=[pltpu.VMEM((tm, D_MODEL), BF16)],
        compiler_params=_cparams("arbitrary", "arbitrary", "arbitrary"),
        name="peer_route",
    )(x, mod, gamma, w_query.astype(BF16), sub_keys.astype(BF16))


def _pack_table(u, v):
    ne = u.shape[0]
    pairs = jnp.stack([u.astype(BF16).reshape(ne * ROWS_PER_EXPERT, LANES),
                       v.astype(BF16).reshape(ne * ROWS_PER_EXPERT, LANES)], axis=-1)
    return lax.bitcast_convert_type(pairs, I32)


def _gather_body(eid_ref, h_ref, g_ref, tab_ref, x_ref, mod_ref, fg_ref, o_ref, buf, sem, po,
                 *, tb, final_norm):
    ne = PEER_PICKS
    rows = ROWS_PER_EXPERT

    ns = GATHER_SLOTS
    ahead = ns - 1

    def issue(t, slot, lo, hi):
        for k in range(lo, hi):
            src = tab_ref.at[pl.ds(pl.multiple_of(eid_ref[t, k] * rows, rows), rows), :]
            dst = buf.at[slot, pl.ds(GATHER_PITCH * k, rows), :]
            pltpu.make_async_copy(src, dst, sem.at[slot]).start(priority=k % 2)

    def wait(slot):
        pltpu.make_async_copy(tab_ref.at[pl.ds(0, ne * rows), :], buf.at[slot, pl.ds(0, ne * rows), :],
                              sem.at[slot]).wait()

    for q in range(ahead):
        issue(q, q, 0, ne)
    early = 4
    halves = 2
    per_half = ne // halves
    even = lax.broadcasted_iota(I32, (1, 2 * LANES), 1) % 2 == 0
    row_id = lax.broadcasted_iota(I32, (SUBLANES, LANES), 0)

    def chunk(slot, c, first=0, count=ne):
        words = buf[slot, pl.ds(first * GATHER_PITCH + c, count, stride=GATHER_PITCH), :]
        return pltpu.bitcast(words, BF16)

    def score_half(t8, j, hf, c, acc):
        xrow = h_ref[pl.ds(t8, SUBLANES), c * LANES:(c + 1) * LANES][j:j + 1]
        return acc + chunk(j % ns, c, hf * per_half, per_half).astype(F32) * xrow

    def lane_sums(acc):
        return jnp.sum(acc.T, axis=0, keepdims=True)

    def coefficients(parts, t8, j):
        s = jnp.concatenate(parts, axis=1)
        g = g_ref[pl.ds(t8, SUBLANES), :][j:j + 1]
        coef = pltpu.roll(jnp.where(even, _gelu(s) * g, 0.0), 1, 1)
        return jnp.broadcast_to(coef, (SUBLANES, 2 * LANES))

    wait(0)
    parts = []
    for hf in range(halves):
        a = jnp.zeros((2 * per_half, LANES), F32)
        for c in range(rows):
            a = score_half(0, 0, hf, c, a)
        parts.append(lane_sums(a))
    coef0 = coefficients(parts, 0, 0)

    def body(it, coef):
        t8 = pl.multiple_of(it * SUBLANES, SUBLANES)
        t8_next = pl.multiple_of(jnp.minimum(t8 + SUBLANES, tb - SUBLANES), SUBLANES)
        acc = [None] * rows
        for j in range(SUBLANES):
            jn = (j + 1) % SUBLANES
            t8n = t8 if j + 1 < SUBLANES else t8_next
            nxt = jnp.minimum(t8 + j + ahead, tb - 1)
            nslot = (j + ahead) % ns
            wait((j + 1) % ns)
            coef_b = coef.astype(BF16)
            step = 0
            parts = []
            for hf in range(halves):
                a = jnp.zeros((2 * per_half, LANES), F32)
                for c in range(rows):
                    issue(nxt, nslot, step * early, (step + 1) * early)
                    step += 1
                    a = score_half(t8n, jn, hf, c, a)
                parts.append(lane_sums(a))
            for c in range(rows):
                issue(nxt, nslot, step * early, (step + 1) * early)
                step += 1
                o = jnp.dot(coef_b, chunk(j % ns, c), preferred_element_type=F32)
                acc[c] = o if j == 0 else jnp.where(row_id == j, o, acc[c])
            issue(nxt, nslot, step * early, ne)
            coef = coefficients(parts, t8n, jn)
        for c in range(rows):
            po[pl.ds(t8, SUBLANES), c * LANES:(c + 1) * LANES] = acc[c]
        return coef

    lax.fori_loop(0, tb // SUBLANES, body, coef0)
    for q in range(1, ahead):
        wait((tb + q) % ns)
    y = x_ref[...] + mod_ref[0, 5:6, :] * po[...]
    if final_norm:
        y = y * lax.rsqrt(jnp.mean(y * y, axis=-1, keepdims=True) + EPS) * fg_ref[...]
    o_ref[...] = y


def _peer_gather(eid, h, g, table, x, mod, final_g, seq_len, final_norm, tok_offset, tb=GATHER_TOKENS):
    t = h.shape[0]
    return pl.pallas_call(
        functools.partial(_gather_body, tb=tb, final_norm=final_norm),
        grid=(t // tb,),
        in_specs=[
            pl.BlockSpec((tb, PEER_PICKS), lambda i: (i, 0), memory_space=pltpu.SMEM),
            pl.BlockSpec((tb, D_MODEL), lambda i: (i, 0)),
            pl.BlockSpec((tb, 2 * PEER_PICKS), lambda i: (i, 0)),
            pl.BlockSpec(memory_space=pl.ANY),
            pl.BlockSpec((tb, D_MODEL), lambda i: (i, 0)),
            pl.BlockSpec((1, 6, D_MODEL), lambda i: ((tok_offset + i * tb) // seq_len, 0, 0)),
            pl.BlockSpec((1, D_MODEL), lambda i: (0, 0)),
        ],
        out_specs=pl.BlockSpec((tb, D_MODEL), lambda i: (i, 0)),
        out_shape=jax.ShapeDtypeStruct((t, D_MODEL), F32),
        scratch_shapes=[
            pltpu.VMEM((GATHER_SLOTS, PEER_PICKS * GATHER_PITCH, LANES), I32),
            pltpu.SemaphoreType.DMA((GATHER_SLOTS,)),
            pltpu.VMEM((tb, D_MODEL), F32),
        ],
        compiler_params=_cparams("arbitrary"),
        name="peer_gather",
    )(eid, h, g, table, x, mod, final_g)


SC_LANES = 16
SC_WORKERS = 32
SC_GROUP = 8
SC_TOKENS = 12288
SC_CHUNKS = D_MODEL // SC_LANES


def _sc_body(tab_hbm, eid_hbm, g_hbm, h_hbm, out_hbm, idx_v, g_v, x_v, o_v, rows0, rows1, rows2, rows3,
             sem0, sem1, sem2, sem3, *, tpw):
    nl = SC_LANES
    wid = lax.axis_index("s") * 2 + lax.axis_index("c")
    lane = lax.iota(I32, nl)

    def permute(x, idx):
        return jnp.take_along_axis(x, idx, axis=0, mode="promise_in_bounds")
    rows = (rows0, rows1, rows2, rows3)
    sems = (sem0, sem1, sem2, sem3)
    ns = len(rows)
    units = SC_GROUP * PEER_HEADS

    def gather(tok, hd, slot):
        return pltpu.make_async_copy(tab_hbm.at[idx_v.at[tok, hd]], rows[slot], sems[slot])

    def compute(tok, hd, slot, t):
        rv = rows[slot]

        @pl.when(hd == 0)
        def _():
            def zero(c, cc):
                o_v[pl.ds(c * nl, nl)] = jnp.zeros((nl,), F32)
                return cc
            lax.fori_loop(0, SC_CHUNKS, zero, 0)

        def score(c, accs):
            x = x_v[tok, pl.ds(c * nl, nl)]
            out = []
            for kk in range(PEER_TOPK):
                u = plsc.bitcast(lax.shift_left(rv[kk, pl.ds(c * nl, nl)], 16), F32)
                out.append(accs[kk] + u * x)
            return tuple(out)

        accs = lax.fori_loop(0, SC_CHUNKS, score, tuple(jnp.zeros((nl,), F32) for _ in range(PEER_TOPK)))
        vecs = list(accs)
        d = 1
        while len(vecs) > 1:
            partner = lane ^ d
            take_lo = (lane & d) == 0
            nxt_vecs = []
            for i in range(0, len(vecs), 2):
                a = vecs[i] + permute(vecs[i], partner)
                b = vecs[i + 1] + permute(vecs[i + 1], partner)
                nxt_vecs.append(jnp.where(take_lo, a, b))
            vecs = nxt_vecs
            d *= 2
        s = vecs[0]
        z = 0.7978845608028654 * (s + 0.044715 * s * s * s)
        act = s * (1.0 - 1.0 / (jnp.exp(2.0 * z) + 1.0))
        coef = act * g_v[tok, hd, :]
        coefs = [permute(coef, jnp.full((nl,), kk, I32)) for kk in range(PEER_TOPK)]

        def combine(c, cc):
            parts = []
            for q in range(4):
                acc = None
                for kk in range(q * PEER_TOPK // 4, (q + 1) * PEER_TOPK // 4):
                    v = plsc.bitcast(lax.bitwise_and(rv[kk, pl.ds(c * nl, nl)], jnp.int32(-65536)), F32)
                    term = coefs[kk] * v
                    acc = term if acc is None else acc + term
                parts.append(acc)
            o_v[pl.ds(c * nl, nl)] = o_v[pl.ds(c * nl, nl)] + ((parts[0] + parts[1]) + (parts[2] + parts[3]))
            return cc

        lax.fori_loop(0, SC_CHUNKS, combine, 0)

        @pl.when(hd == PEER_HEADS - 1)
        def _():
            pltpu.sync_copy(o_v, out_hbm.at[t])

    def group(gi, carry):
        t0 = wid * tpw + gi * SC_GROUP
        pltpu.sync_copy(eid_hbm.at[pl.ds(t0, SC_GROUP)], idx_v)
        pltpu.sync_copy(g_hbm.at[pl.ds(t0, SC_GROUP)], g_v)
        pltpu.sync_copy(h_hbm.at[pl.ds(t0, SC_GROUP)], x_v)
        for q in range(ns - 1):
            gather(0, q, q).start()

        def ring(ri, cc):
            j0 = ri * ns
            tok, hd0 = j0 // PEER_HEADS, j0 % PEER_HEADS
            for q in range(ns):
                jn = j0 + q + ns - 1

                @pl.when(jn < units)
                def _():
                    gather(jn // PEER_HEADS, jn % PEER_HEADS, (q + ns - 1) % ns).start()

                gather(tok, hd0 + q, q).wait()
                compute(tok, hd0 + q, q, t0 + tok)
            return cc

        lax.fori_loop(0, units // ns, ring, 0)
        return carry

    lax.fori_loop(0, tpw // SC_GROUP, group, 0)


def _peer_sc(table, eid, gate, h):
    t = h.shape[0]
    tpw = t // SC_WORKERS
    cp = pltpu.CompilerParams()
    if "needs_layout_passes" in pltpu.CompilerParams.__dataclass_fields__:
        cp = dataclasses.replace(cp, needs_layout_passes=False)
    run = pl.kernel(
        functools.partial(_sc_body, tpw=tpw),
        out_type=jax.ShapeDtypeStruct((t, D_MODEL), F32),
        mesh=plsc.VectorSubcoreMesh(core_axis_name="c", subcore_axis_name="s"),
        scratch_types=[
            pltpu.VMEM((SC_GROUP, PEER_HEADS, PEER_TOPK), I32),
            pltpu.VMEM((SC_GROUP, PEER_HEADS, PEER_TOPK), F32),
            pltpu.VMEM((SC_GROUP, D_MODEL), F32),
            pltpu.VMEM((D_MODEL,), F32),
            pltpu.VMEM((PEER_TOPK, D_MODEL), I32),
            pltpu.VMEM((PEER_TOPK, D_MODEL), I32),
            pltpu.VMEM((PEER_TOPK, D_MODEL), I32),
            pltpu.VMEM((PEER_TOPK, D_MODEL), I32),
            pltpu.SemaphoreType.DMA,
            pltpu.SemaphoreType.DMA,
            pltpu.SemaphoreType.DMA,
            pltpu.SemaphoreType.DMA,
        ],
        compiler_params=cp,
        name="peer_sc",
    )
    return run(table, eid.reshape(t, PEER_HEADS, PEER_TOPK), gate.reshape(t, PEER_HEADS, PEER_TOPK), h)


def _finish_body(x_ref, po_ref, mod_ref, fg_ref, o_ref, *, final_norm):
    y = x_ref[...] + mod_ref[0, 5:6, :] * po_ref[...]
    if final_norm:
        y = y * lax.rsqrt(jnp.mean(y * y, axis=-1, keepdims=True) + EPS) * fg_ref[...]
    o_ref[...] = y


def _peer_finish(x, po, mod, final_g, seq_len, final_norm, tm=512):
    t = x.shape[0]
    return pl.pallas_call(
        functools.partial(_finish_body, final_norm=final_norm),
        grid=(t // tm,),
        in_specs=[
            pl.BlockSpec((tm, D_MODEL), lambda i: (i, 0)),
            pl.BlockSpec((tm, D_MODEL), lambda i: (i, 0)),
            pl.BlockSpec((1, 6, D_MODEL), lambda i: ((i * tm) // seq_len, 0, 0)),
            pl.BlockSpec((1, D_MODEL), lambda i: (0, 0)),
        ],
        out_specs=pl.BlockSpec((tm, D_MODEL), lambda i: (i, 0)),
        out_shape=jax.ShapeDtypeStruct((t, D_MODEL), F32),
        compiler_params=_cparams("arbitrary"),
        name="peer_finish",
    )(x, po, mod, final_g)


def _peer_layer(x, mod, gamma, w_query, sub_keys, table, final_g, final_norm):
    nb, s, _ = x.shape
    t = nb * s
    h, e_t, g_t = _peer_route(x, mod, gamma, w_query, sub_keys)
    eid = e_t.transpose(0, 3, 1, 2).reshape(t, PEER_PICKS)
    gate = g_t.transpose(0, 3, 1, 2).reshape(t, PEER_PICKS)
    hf = h.reshape(t, D_MODEL)
    xf = x.reshape(t, D_MODEL)
    t_sc = SC_TOKENS
    gate_tc = gate[t_sc:]
    gate_tc = jnp.stack([gate_tc, jnp.zeros_like(gate_tc)], axis=-1).reshape(t - t_sc, 2 * PEER_PICKS)
    y_tc = _peer_gather(eid[t_sc:], hf[t_sc:], gate_tc, table, xf[t_sc:], mod, final_g, s, final_norm, t_sc)
    po = _peer_sc(table.reshape(N_EXPERTS, D_MODEL), eid[:t_sc], gate[:t_sc], hf[:t_sc])
    y_sc = _peer_finish(xf[:t_sc], po, mod, final_g, s, final_norm)
    return jnp.concatenate([y_sc, y_tc], axis=0).reshape(nb, s, D_MODEL)


def kernel(x_prompt, x_sample, c_prompt, c_sample, ln_mix_g, ln_ffn_g, w_mod, b_mod, rec_w_in, rec_conv_w,
           rec_conv_b, rec_ga_w, rec_ga_b, rec_gx_w, rec_gx_b, rec_lam, rec_w_out, att_w_qkv, att_q_g,
           att_k_g, att_w_o, peer_w_query, peer_sub_keys, peer_u, peer_v, final_g):
    tables = [_pack_table(peer_u[i], peer_v[i]) for i in range(DEPTH)]
    fg = final_g.reshape(1, D_MODEL)

    xs = [x_prompt, x_sample]
    cs = [c_prompt, c_sample]
    for i in range(DEPTH):
        gm = ln_mix_g[i].reshape(1, D_MODEL)
        gf = ln_ffn_g[i].reshape(1, D_MODEL)
        j = i // 2
        for tr in range(2):
            x = xs[tr]
            mod = _mod(cs[tr], w_mod[i], b_mod[i])
            if i % 2 == 0:
                x = _rglru_layer(x, mod, gm, rec_w_in[j], rec_conv_w[j], rec_conv_b[j], rec_ga_w[j],
                                 rec_ga_b[j], rec_gx_w[j], rec_gx_b[j], rec_lam[j], rec_w_out[j])
            else:
                x = _attention_layer(x, mod, gm, att_w_qkv[j], att_q_g[j], att_k_g[j], att_w_o[j])
            xs[tr] = _peer_layer(x, mod, gf, peer_w_query[i], peer_sub_keys[i], tables[i], fg,
                                 final_norm=(i == DEPTH - 1))
    return (xs[0], xs[1])
```

```python
import dataclasses
import functools

import jax
import jax.numpy as jnp
from jax import lax
from jax.experimental import pallas as pl
from jax.experimental.pallas import tpu as pltpu
from jax.experimental.pallas import tpu_sc as plsc

F32 = jnp.float32
BF16 = jnp.bfloat16
I32 = jnp.int32

D_MODEL = 1024
DEPTH = 2
GRID_W = 64
EPS = 1e-6
RNN_WIDTH = D_MODEL
RNN_BLOCKS = 16
RNN_BLOCK_W = RNN_WIDTH // RNN_BLOCKS
CONV_W = 4
LRU_C = 8.0
N_HEADS = 16
N_KV_HEADS = 4
HEAD_DIM = D_MODEL // N_HEADS
GROUP = N_HEADS // N_KV_HEADS
AXIS_DIM = HEAD_DIM // 2
ROPE_THETA = 10000.0
N_KEYS = 128
N_EXPERTS = N_KEYS * N_KEYS
PEER_HEADS = 8
PEER_TOPK = 16
PEER_QDIM = 256
PEER_HALF = PEER_QDIM // 2
PEER_PICKS = PEER_HEADS * PEER_TOPK

LANES = 128
SUBLANES = 8
VMEM_LIMIT = 48 * 1024 * 1024

GATHER_PITCH = 12
GATHER_TOKENS = 64
GATHER_SLOTS = 4
ROWS_PER_EXPERT = D_MODEL // LANES


def _cparams(*sem):
    return pltpu.CompilerParams(dimension_semantics=sem, vmem_limit_bytes=VMEM_LIMIT)


def _gelu(x):
    return jax.nn.gelu(x)


def _norm_mod(x, gamma, scale, shift):
    ms = jnp.mean(x * x, axis=-1, keepdims=True)
    y = x * lax.rsqrt(ms + EPS) * gamma
    return y * (1.0 + scale) + shift


def _mod_body(c_ref, w_ref, b_ref, o_ref):
    c = c_ref[...]
    s = c * jax.nn.sigmoid(c)
    o_ref[...] = jnp.dot(s, w_ref[...], preferred_element_type=F32) + b_ref[...]


def _mod(c, w, b):
    nb, n = c.shape[0], w.shape[1]
    tn = 1536
    out = pl.pallas_call(
        _mod_body,
        grid=(n // tn,),
        in_specs=[
            pl.BlockSpec((nb, D_MODEL), lambda j: (0, 0)),
            pl.BlockSpec((D_MODEL, tn), lambda j: (0, j)),
            pl.BlockSpec((1, tn), lambda j: (0, j)),
        ],
        out_specs=pl.BlockSpec((nb, tn), lambda j: (0, j)),
        out_shape=jax.ShapeDtypeStruct((nb, n), F32),
        compiler_params=_cparams("arbitrary"),
        name="adaln_mod",
    )(c, w, b.reshape(1, n))
    return out.reshape(nb, 6, D_MODEL)


def _nmm_body(x_ref, mod_ref, g_ref, w_ref, o_ref, *, sh, sc):
    h = _norm_mod(x_ref[0], g_ref[...], mod_ref[0, sc:sc + 1, :], mod_ref[0, sh:sh + 1, :])
    o_ref[0] = jnp.dot(h.astype(BF16), w_ref[...], preferred_element_type=F32)


def _norm_mod_matmul(x, mod, gamma, w_bf16, sh, sc, tm=512):
    nb, s, _ = x.shape
    n = w_bf16.shape[1]
    return pl.pallas_call(
        functools.partial(_nmm_body, sh=sh, sc=sc),
        grid=(nb, s // tm),
        in_specs=[
            pl.BlockSpec((1, tm, D_MODEL), lambda b, i: (b, i, 0)),
            pl.BlockSpec((1, 6, D_MODEL), lambda b, i: (b, 0, 0)),
            pl.BlockSpec((1, D_MODEL), lambda b, i: (0, 0)),
            pl.BlockSpec((D_MODEL, n), lambda b, i: (0, 0)),
        ],
        out_specs=pl.BlockSpec((1, tm, n), lambda b, i: (b, i, 0)),
        out_shape=jax.ShapeDtypeStruct((nb, s, n), F32),
        compiler_params=_cparams("arbitrary", "arbitrary"),
        name="norm_mod_matmul",
    )(x, mod, gamma, w_bf16)


def _proj_res_body(m_ref, w_ref, x_ref, mod_ref, o_ref, *, gi):
    y = jnp.dot(m_ref[0].astype(BF16), w_ref[...], preferred_element_type=F32)
    o_ref[0] = x_ref[0] + mod_ref[0, gi:gi + 1, :] * y


def _proj_residual(m, w_bf16, x, mod, gi, tm=512):
    nb, s, k = m.shape
    return pl.pallas_call(
        functools.partial(_proj_res_body, gi=gi),
        grid=(nb, s // tm),
        in_specs=[
            pl.BlockSpec((1, tm, k), lambda b, i: (b, i, 0)),
            pl.BlockSpec((k, D_MODEL), lambda b, i: (0, 0)),
            pl.BlockSpec((1, tm, D_MODEL), lambda b, i: (b, i, 0)),
            pl.BlockSpec((1, 6, D_MODEL), lambda b, i: (b, 0, 0)),
        ],
        out_specs=pl.BlockSpec((1, tm, D_MODEL), lambda b, i: (b, i, 0)),
        out_shape=jax.ShapeDtypeStruct((nb, s, D_MODEL), F32),
        compiler_params=_cparams("arbitrary", "arbitrary"),
        name="proj_residual",
    )(m, w_bf16, x, mod)


def _scan_tile(a, b, carry, reverse):
    tt = a.shape[0]
    row = lax.broadcasted_iota(I32, (tt, LANES), 0) % SUBLANES
    for d in (1, 2, 4):
        if reverse:
            shift, keep = tt - d, row < SUBLANES - d
        else:
            shift, keep = d, row >= d
        ap = pltpu.roll(a, shift, 0)
        bp = pltpu.roll(b, shift, 0)
        b = jnp.where(keep, a * bp + b, b)
        a = jnp.where(keep, a * ap, a)
    groups = tt // SUBLANES
    hs = [None] * groups
    order = range(groups - 1, -1, -1) if reverse else range(groups)
    for g in order:
        lo = g * SUBLANES
        h = a[lo:lo + SUBLANES] * carry + b[lo:lo + SUBLANES]
        hs[g] = h
        last = h[0:1] if reverse else h[SUBLANES - 1:SUBLANES]
        carry = jnp.broadcast_to(last, (SUBLANES, LANES))
    return jnp.concatenate(hs, axis=0), carry


def _rglru_body(y_ref, x_ref, cw_ref, cb_ref, gaw_ref, gab_ref, gxw_ref, gxb_ref, nc_ref,
                o_ref, xp, hf, *, s, tt):
    nt = s // tt
    pad = SUBLANES
    zeros = jnp.zeros((pad, LANES), F32)
    xp[0:pad, :] = zeros
    xp[s + pad:s + 2 * pad, :] = zeros

    def copy_tile(i, c):
        r0 = pl.multiple_of(i * tt, tt)
        xp[pl.ds(pl.multiple_of(r0 + pad, SUBLANES), tt), :] = x_ref[0, pl.ds(r0, tt), :]
        return c

    lax.fori_loop(0, nt, copy_tile, 0)

    cw = cw_ref[...]
    cb = cb_ref[...]

    def conv_tile(r0):
        win = xp[pl.ds(r0, tt + 2 * pad), :]
        acc = cb + cw[2:3] * win[pad:pad + tt]
        for k in (0, 1, 3):
            shifted = pltpu.roll(win, (2 - k) % (tt + 2 * pad), 0)
            acc = acc + cw[k:k + 1] * shifted[pad:pad + tt]
        return acc

    def gates(d, xc):
        xb = xc.astype(BF16)
        r = jax.nn.sigmoid(jnp.dot(xb, gaw_ref[d, 0], preferred_element_type=F32) + gab_ref[d])
        ig = jax.nn.sigmoid(jnp.dot(xb, gxw_ref[d, 0], preferred_element_type=F32) + gxb_ref[d])
        a = jnp.exp(nc_ref[d] * r)
        b = jnp.sqrt(1.0 - a * a) * (ig * xc)
        return a, b

    carry0 = jnp.zeros((SUBLANES, LANES), F32)

    def fwd(i, carry):
        r0 = pl.multiple_of(i * tt, tt)
        a, b = gates(0, conv_tile(r0))
        h, carry = _scan_tile(a, b, carry, False)
        hf[pl.ds(r0, tt), :] = h
        return carry

    lax.fori_loop(0, nt, fwd, carry0)

    def bwd(ii, carry):
        r0 = pl.multiple_of((nt - 1 - ii) * tt, tt)
        a, b = gates(1, conv_tile(r0))
        h, carry = _scan_tile(a, b, carry, True)
        o_ref[0, pl.ds(r0, tt), :] = (hf[pl.ds(r0, tt), :] + h) * _gelu(y_ref[0, pl.ds(r0, tt), :])
        return carry

    lax.fori_loop(0, nt, bwd, carry0)


def _rglru_core(u, conv_w, conv_b, gaw, gab, gxw, gxb, negc, tt=256):
    nb, s, _ = u.shape
    ng = RNN_WIDTH // LANES
    return pl.pallas_call(
        functools.partial(_rglru_body, s=s, tt=tt),
        grid=(nb, ng),
        in_specs=[
            pl.BlockSpec((1, s, LANES), lambda b, j: (b, 0, j)),
            pl.BlockSpec((1, s, LANES), lambda b, j: (b, 0, ng + j)),
            pl.BlockSpec((CONV_W, LANES), lambda b, j: (0, j)),
            pl.BlockSpec((1, LANES), lambda b, j: (0, j)),
            pl.BlockSpec((2, 1, LANES, LANES), lambda b, j: (0, j, 0, 0)),
            pl.BlockSpec((2, 1, LANES), lambda b, j: (0, 0, j)),
            pl.BlockSpec((2, 1, LANES, LANES), lambda b, j: (0, j, 0, 0)),
            pl.BlockSpec((2, 1, LANES), lambda b, j: (0, 0, j)),
            pl.BlockSpec((2, 1, LANES), lambda b, j: (0, 0, j)),
        ],
        out_specs=pl.BlockSpec((1, s, LANES), lambda b, j: (b, 0, j)),
        out_shape=jax.ShapeDtypeStruct((nb, s, RNN_WIDTH), F32),
        scratch_shapes=[pltpu.VMEM((s + 2 * SUBLANES, LANES), F32), pltpu.VMEM((s, LANES), F32)],
        compiler_params=_cparams("arbitrary", "arbitrary"),
        name="rglru_core",
    )(u, u, conv_w, conv_b, gaw, gab, gxw, gxb, negc)


def _block_diag_groups(w):
    ng = RNN_WIDTH // LANES
    w = w.reshape(ng, 2, RNN_BLOCK_W, RNN_BLOCK_W)
    out = jnp.zeros((ng, LANES, LANES), w.dtype)
    out = out.at[:, :RNN_BLOCK_W, :RNN_BLOCK_W].set(w[:, 0])
    out = out.at[:, RNN_BLOCK_W:, RNN_BLOCK_W:].set(w[:, 1])
    return out


def _rglru_layer(x, mod, gamma, w_in, conv_w, conv_b, ga_w, ga_b, gx_w, gx_b, lam, w_out):
    u = _norm_mod_matmul(x, mod, gamma, w_in.astype(BF16), sh=0, sc=1)
    gaw = jnp.stack([_block_diag_groups(ga_w[d]) for d in range(2)]).astype(BF16)
    gxw = jnp.stack([_block_diag_groups(gx_w[d]) for d in range(2)]).astype(BF16)
    gab = ga_b.reshape(2, 1, RNN_WIDTH)
    gxb = gx_b.reshape(2, 1, RNN_WIDTH)
    negc = (-LRU_C * jax.nn.softplus(-lam)).reshape(2, 1, RNN_WIDTH)
    m = _rglru_core(u, conv_w, conv_b.reshape(1, RNN_WIDTH), gaw, gab, gxw, gxb, negc)
    return _proj_residual(m, w_out.astype(BF16), x, mod, gi=2)


def _seg_mean(x2, seg_ref):
    hi = x2.astype(BF16)
    lo = (x2 - hi.astype(F32)).astype(BF16)
    return (jnp.dot(hi, seg_ref[...], preferred_element_type=F32)
            + jnp.dot(lo, seg_ref[...], preferred_element_type=F32))


def _rope(x, cos, sin, lane_lo):
    outs = []
    for j in range(x.shape[1] // LANES):
        xt = x[:, j * LANES:(j + 1) * LANES]
        rot = jnp.where(lane_lo, pltpu.roll(xt, LANES - AXIS_DIM // 2, 1), pltpu.roll(xt, AXIS_DIM // 2, 1))
        outs.append(xt * cos + rot * sin)
    return jnp.concatenate(outs, axis=1) if len(outs) > 1 else outs[0]


def _qkv_body(x_ref, mod_ref, g_ref, w_ref, segq_ref, segk_ref, qg_ref, kg_ref, cosq_ref, sinq_ref,
              cosk_ref, sink_ref, q_ref, kt_ref, v_ref, *, tm):
    h = _norm_mod(x_ref[0], g_ref[...], mod_ref[0, 1:2, :], mod_ref[0, 0:1, :])
    qkv = jnp.dot(h.astype(BF16), w_ref[...], preferred_element_type=F32)
    nq = N_HEADS * HEAD_DIM
    nk = N_KV_HEADS * HEAD_DIM
    q = qkv[:, :nq]
    k = qkv[:, nq:nq + nk]
    v = qkv[:, nq + nk:]
    lane = lax.broadcasted_iota(I32, (tm, LANES), 1)
    lane_lo = (lane % AXIS_DIM) < (AXIS_DIM // 2)
    q = q * lax.rsqrt(_seg_mean(q * q, segq_ref) + EPS) * qg_ref[...]
    k = k * lax.rsqrt(_seg_mean(k * k, segk_ref) + EPS) * kg_ref[...]
    q = _rope(q, cosq_ref[...], sinq_ref[...], lane_lo)
    k = _rope(k, cosk_ref[...], sink_ref[...], lane_lo)
    q_ref[0] = q.astype(BF16)
    kt = k.T.astype(BF16)
    for g in range(N_KV_HEADS):
        kt_ref[0, g] = kt[g * HEAD_DIM:(g + 1) * HEAD_DIM, :]
        v_ref[0, g] = v[:, g * HEAD_DIM:(g + 1) * HEAD_DIM].astype(BF16)


def _rope_tables(s):
    rows = s // GRID_W
    row = jnp.repeat(jnp.arange(rows, dtype=F32), GRID_W)
    col = jnp.tile(jnp.arange(GRID_W, dtype=F32), rows)
    inv = ROPE_THETA ** (-jnp.arange(0, AXIS_DIM, 2, dtype=F32) / AXIS_DIM)
    ar = row[:, None] * inv
    ac = col[:, None] * inv
    cos = jnp.concatenate([jnp.cos(ar), jnp.cos(ar), jnp.cos(ac), jnp.cos(ac)], axis=1)
    sin = jnp.concatenate([-jnp.sin(ar), jnp.sin(ar), -jnp.sin(ac), jnp.sin(ac)], axis=1)
    return jnp.tile(cos, (1, LANES // HEAD_DIM)), jnp.tile(sin, (1, LANES // HEAD_DIM))


def _attn_body(q_ref, kt_ref, v_ref, o_ref):
    kt = kt_ref[0, 0]
    v = v_ref[0, 0]
    outs = []
    for hh in range(GROUP):
        qh = q_ref[0, :, hh * HEAD_DIM:(hh + 1) * HEAD_DIM]
        sc = jnp.dot(qh, kt, preferred_element_type=F32)
        m = jnp.max(sc, axis=-1, keepdims=True)
        p = jnp.exp(sc - m)
        l = jnp.sum(p, axis=-1, keepdims=True)
        o = jnp.dot(p.astype(BF16), v, preferred_element_type=F32)
        outs.append(o / l)
    o_ref[0] = jnp.concatenate(outs, axis=1).astype(BF16)


def _attention_layer(x, mod, gamma, w_qkv, q_g, k_g, w_o, tm=256, tq=128):
    nb, s, _ = x.shape
    nq = N_HEADS * HEAD_DIM
    nk = N_KV_HEADS * HEAD_DIM
    seg = jnp.kron(jnp.eye(N_HEADS, dtype=F32), jnp.full((HEAD_DIM, HEAD_DIM), 1.0 / HEAD_DIM, F32)).astype(BF16)
    segk = seg[:nk, :nk]
    cos, sin = _rope_tables(s)
    scale = HEAD_DIM ** -0.5
    q, kt, v = pl.pallas_call(
        functools.partial(_qkv_body, tm=tm),
        grid=(nb, s // tm),
        in_specs=[
            pl.BlockSpec((1, tm, D_MODEL), lambda b, i: (b, i, 0)),
            pl.BlockSpec((1, 6, D_MODEL), lambda b, i: (b, 0, 0)),
            pl.BlockSpec((1, D_MODEL), lambda b, i: (0, 0)),
            pl.BlockSpec((D_MODEL, nq + 2 * nk), lambda b, i: (0, 0)),
            pl.BlockSpec((nq, nq), lambda b, i: (0, 0)),
            pl.BlockSpec((nk, nk), lambda b, i: (0, 0)),
            pl.BlockSpec((1, nq), lambda b, i: (0, 0)),
            pl.BlockSpec((1, nk), lambda b, i: (0, 0)),
            pl.BlockSpec((tm, LANES), lambda b, i: (i, 0)),
            pl.BlockSpec((tm, LANES), lambda b, i: (i, 0)),
            pl.BlockSpec((tm, LANES), lambda b, i: (i, 0)),
            pl.BlockSpec((tm, LANES), lambda b, i: (i, 0)),
        ],
        out_specs=[
            pl.BlockSpec((1, tm, nq), lambda b, i: (b, i, 0)),
            pl.BlockSpec((1, N_KV_HEADS, HEAD_DIM, tm), lambda b, i: (b, 0, 0, i)),
            pl.BlockSpec((1, N_KV_HEADS, tm, HEAD_DIM), lambda b, i: (b, 0, i, 0)),
        ],
        out_shape=[
            jax.ShapeDtypeStruct((nb, s, nq), BF16),
            jax.ShapeDtypeStruct((nb, N_KV_HEADS, HEAD_DIM, s), BF16),
            jax.ShapeDtypeStruct((nb, N_KV_HEADS, s, HEAD_DIM), BF16),
        ],
        compiler_params=_cparams("arbitrary", "arbitrary"),
        name="qkv_rope",
    )(x, mod, gamma, w_qkv.astype(BF16), seg, segk,
      jnp.tile(q_g, N_HEADS).reshape(1, nq), jnp.tile(k_g, N_KV_HEADS).reshape(1, nk),
      cos * scale, sin * scale, cos, sin)

    gw = GROUP * HEAD_DIM
    o = pl.pallas_call(
        _attn_body,
        grid=(nb, N_KV_HEADS, s // tq),
        in_specs=[
            pl.BlockSpec((1, tq, gw), lambda b, g, i: (b, i, g)),
            pl.BlockSpec((1, 1, HEAD_DIM, s), lambda b, g, i: (b, g, 0, 0)),
            pl.BlockSpec((1, 1, s, HEAD_DIM), lambda b, g, i: (b, g, 0, 0)),
        ],
        out_specs=pl.BlockSpec((1, tq, gw), lambda b, g, i: (b, i, g)),
        out_shape=jax.ShapeDtypeStruct((nb, s, nq), BF16),
        compiler_params=_cparams("arbitrary", "arbitrary", "arbitrary"),
        name="attention",
    )(q, kt, v)
    return _proj_residual(o, w_o.astype(BF16), x, mod, gi=2)


def _topk_rows(s, ids, k, id_bound):
    vals, picks = [], []
    for _ in range(k):
        m = jnp.max(s, axis=0, keepdims=True)
        i = jnp.min(jnp.where(s == m, ids, id_bound), axis=0, keepdims=True)
        vals.append(m)
        picks.append(i)
        s = jnp.where(ids == i, -jnp.inf, s)
    return jnp.concatenate(vals, axis=0), jnp.concatenate(picks, axis=0)


def _route_body(x_ref, mod_ref, g_ref, wq_ref, keys_ref, h_ref, e_ref, gate_ref, hb, *, tm):
    hd = pl.program_id(2)

    @pl.when(hd == 0)
    def _():
        h = _norm_mod(x_ref[0], g_ref[...], mod_ref[0, 4:5, :], mod_ref[0, 3:4, :])
        h_ref[0] = h
        hb[...] = h.astype(BF16)

    q = jnp.dot(hb[...], wq_ref[...], preferred_element_type=F32)
    key_ids = lax.broadcasted_iota(I32, (N_KEYS, tm), 0)
    tops = []
    for p in range(2):
        qp = q[:, p * PEER_HALF:(p + 1) * PEER_HALF].astype(BF16)
        st = lax.dot_general(keys_ref[p], qp, (((1,), (1,)), ((), ())), preferred_element_type=F32)
        tops.append(_topk_rows(st, key_ids, PEER_TOPK, N_KEYS))
    (v0, i0), (v1, i1) = tops

    r16 = lax.broadcasted_iota(I32, (PEER_TOPK, tm), 0)
    cand, fid, eid = [], [], []
    for a in range(4):
        cand.append(v0[a:a + 1] + v1)
        fid.append(a * PEER_TOPK + r16)
        eid.append(i0[a:a + 1] * N_KEYS + i1)
    for b in range(3):
        cand.append(jnp.where(r16 >= 4, v0 + v1[b:b + 1], -jnp.inf))
        fid.append(r16 * PEER_TOPK + b)
        eid.append(i0 * N_KEYS + i1[b:b + 1])
    cand = jnp.concatenate(cand, axis=0)
    fid = jnp.concatenate(fid, axis=0)
    eid = jnp.concatenate(eid, axis=0)

    best, chosen = [], []
    for _ in range(PEER_TOPK):
        m = jnp.max(cand, axis=0, keepdims=True)
        f = jnp.min(jnp.where(cand == m, fid, PEER_TOPK * PEER_TOPK), axis=0, keepdims=True)
        hit = fid == f
        best.append(m)
        chosen.append(jnp.max(jnp.where(hit, eid, -1), axis=0, keepdims=True))
        cand = jnp.where(hit, -jnp.inf, cand)
    best = jnp.concatenate(best, axis=0)
    ex = jnp.exp(best - best[0:1])
    gate_ref[0, 0] = ex / jnp.sum(ex, axis=0, keepdims=True)
    e_ref[0, 0] = jnp.concatenate(chosen, axis=0)


def _peer_route(x, mod, gamma, w_query, sub_keys, tm=128):
    nb, s, _ = x.shape
    return pl.pallas_call(
        functools.partial(_route_body, tm=tm),
        grid=(nb, s // tm, PEER_HEADS),
        in_specs=[
            pl.BlockSpec((1, tm, D_MODEL), lambda b, i, h: (b, i, 0)),
            pl.BlockSpec((1, 6, D_MODEL), lambda b, i, h: (b, 0, 0)),
            pl.BlockSpec((1, D_MODEL), lambda b, i, h: (0, 0)),
            pl.BlockSpec((D_MODEL, PEER_QDIM), lambda b, i, h: (0, h)),
            pl.BlockSpec((2, N_KEYS, PEER_HALF), lambda b, i, h: (0, 0, 0)),
        ],
        out_specs=[
            pl.BlockSpec((1, tm, D_MODEL), lambda b, i, h: (b, i, 0)),
            pl.BlockSpec((1, 1, PEER_TOPK, tm), lambda b, i, h: (b, h, 0, i)),
            pl.BlockSpec((1, 1, PEER_TOPK, tm), lambda b, i, h: (b, h, 0, i)),
        ],
        out_shape=[
            jax.ShapeDtypeStruct((nb, s, D_MODEL), F32),
            jax.ShapeDtypeStruct((nb, PEER_HEADS, PEER_TOPK, s), I32),
            jax.ShapeDtypeStruct((nb, PEER_HEADS, PEER_TOPK, s), F32),
        ],
        scratch_shapes=[pltpu.VMEM((tm, D_MODEL), BF16)],
        compiler_params=_cparams("arbitrary", "arbitrary", "arbitrary"),
        name="peer_route",
    )(x, mod, gamma, w_query.astype(BF16), sub_keys.astype(BF16))


def _pack_table(u, v):
    ne = u.shape[0]
    pairs = jnp.stack([u.astype(BF16).reshape(ne * ROWS_PER_EXPERT, LANES),
                       v.astype(BF16).reshape(ne * ROWS_PER_EXPERT, LANES)], axis=-1)
    return lax.bitcast_convert_type(pairs, I32)


def _gather_body(eid_ref, h_ref, g_ref, tab_ref, x_ref, mod_ref, fg_ref, o_ref, buf, sem, po,
                 *, tb, final_norm):
    ne = PEER_PICKS
    rows = ROWS_PER_EXPERT

    ns = GATHER_SLOTS
    ahead = ns - 1

    def issue(t, slot, lo, hi):
        for k in range(lo, hi):
            src = tab_ref.at[pl.ds(pl.multiple_of(eid_ref[t, k] * rows, rows), rows), :]
            dst = buf.at[slot, pl.ds(GATHER_PITCH * k, rows), :]
            pltpu.make_async_copy(src, dst, sem.at[slot]).start(priority=k % 2)

    def wait(slot):
        pltpu.make_async_copy(tab_ref.at[pl.ds(0, ne * rows), :], buf.at[slot, pl.ds(0, ne * rows), :],
                              sem.at[slot]).wait()

    for q in range(ahead):
        issue(q, q, 0, ne)
    early = 4
    halves = 2
    per_half = ne // halves
    even = lax.broadcasted_iota(I32, (1, 2 * LANES), 1) % 2 == 0
    row_id = lax.broadcasted_iota(I32, (SUBLANES, LANES), 0)

    def chunk(slot, c, first=0, count=ne):
        words = buf[slot, pl.ds(first * GATHER_PITCH + c, count, stride=GATHER_PITCH), :]
        return pltpu.bitcast(words, BF16)

    def score_half(t8, j, hf, c, acc):
        xrow = h_ref[pl.ds(t8, SUBLANES), c * LANES:(c + 1) * LANES][j:j + 1]
        return acc + chunk(j % ns, c, hf * per_half, per_half).astype(F32) * xrow

    def lane_sums(acc):
        return jnp.sum(acc.T, axis=0, keepdims=True)

    def coefficients(parts, t8, j):
        s = jnp.concatenate(parts, axis=1)
        g = g_ref[pl.ds(t8, SUBLANES), :][j:j + 1]
        coef = pltpu.roll(jnp.where(even, _gelu(s) * g, 0.0), 1, 1)
        return jnp.broadcast_to(coef, (SUBLANES, 2 * LANES))

    wait(0)
    parts = []
    for hf in range(halves):
        a = jnp.zeros((2 * per_half, LANES), F32)
        for c in range(rows):
            a = score_half(0, 0, hf, c, a)
        parts.append(lane_sums(a))
    coef0 = coefficients(parts, 0, 0)

    def body(it, coef):
        t8 = pl.multiple_of(it * SUBLANES, SUBLANES)
        t8_next = pl.multiple_of(jnp.minimum(t8 + SUBLANES, tb - SUBLANES), SUBLANES)
        acc = [None] * rows
        for j in range(SUBLANES):
            jn = (j + 1) % SUBLANES
            t8n = t8 if j + 1 < SUBLANES else t8_next
            nxt = jnp.minimum(t8 + j + ahead, tb - 1)
            nslot = (j + ahead) % ns
            wait((j + 1) % ns)
            coef_b = coef.astype(BF16)
            step = 0
            parts = []
            for hf in range(halves):
                a = jnp.zeros((2 * per_half, LANES), F32)
                for c in range(rows):
                    issue(nxt, nslot, step * early, (step + 1) * early)
                    step += 1
                    a = score_half(t8n, jn, hf, c, a)
                parts.append(lane_sums(a))
            for c in range(rows):
                issue(nxt, nslot, step * early, (step + 1) * early)
                step += 1
                o = jnp.dot(coef_b, chunk(j % ns, c), preferred_element_type=F32)
                acc[c] = o if j == 0 else jnp.where(row_id == j, o, acc[c])
            issue(nxt, nslot, step * early, ne)
            coef = coefficients(parts, t8n, jn)
        for c in range(rows):
            po[pl.ds(t8, SUBLANES), c * LANES:(c + 1) * LANES] = acc[c]
        return coef

    lax.fori_loop(0, tb // SUBLANES, body, coef0)
    for q in range(1, ahead):
        wait((tb + q) % ns)
    y = x_ref[...] + mod_ref[0, 5:6, :] * po[...]
    if final_norm:
        y = y * lax.rsqrt(jnp.mean(y * y, axis=-1, keepdims=True) + EPS) * fg_ref[...]
    o_ref[...] = y


def _peer_gather(eid, h, g, table, x, mod, final_g, seq_len, final_norm, tok_offset, tb=GATHER_TOKENS):
    t = h.shape[0]
    return pl.pallas_call(
        functools.partial(_gather_body, tb=tb, final_norm=final_norm),
        grid=(t // tb,),
        in_specs=[
            pl.BlockSpec((tb, PEER_PICKS), lambda i: (i, 0), memory_space=pltpu.SMEM),
            pl.BlockSpec((tb, D_MODEL), lambda i: (i, 0)),
            pl.BlockSpec((tb, 2 * PEER_PICKS), lambda i: (i, 0)),
            pl.BlockSpec(memory_space=pl.ANY),
            pl.BlockSpec((tb, D_MODEL), lambda i: (i, 0)),
            pl.BlockSpec((1, 6, D_MODEL), lambda i: ((tok_offset + i * tb) // seq_len, 0, 0)),
            pl.BlockSpec((1, D_MODEL), lambda i: (0, 0)),
        ],
        out_specs=pl.BlockSpec((tb, D_MODEL), lambda i: (i, 0)),
        out_shape=jax.ShapeDtypeStruct((t, D_MODEL), F32),
        scratch_shapes=[
            pltpu.VMEM((GATHER_SLOTS, PEER_PICKS * GATHER_PITCH, LANES), I32),
            pltpu.SemaphoreType.DMA((GATHER_SLOTS,)),
            pltpu.VMEM((tb, D_MODEL), F32),
        ],
        compiler_params=_cparams("arbitrary"),
        name="peer_gather",
    )(eid, h, g, table, x, mod, final_g)


SC_LANES = 16
SC_WORKERS = 32
SC_GROUP = 8
SC_TOKENS = 12288
SC_CHUNKS = D_MODEL // SC_LANES


def _sc_body(tab_hbm, eid_hbm, g_hbm, h_hbm, out_hbm, idx_v, g_v, x_v, o_v, rows0, rows1, rows2, rows3,
             sem0, sem1, sem2, sem3, *, tpw):
    nl = SC_LANES
    wid = lax.axis_index("s") * 2 + lax.axis_index("c")
    lane = lax.iota(I32, nl)

    def permute(x, idx):
        return jnp.take_along_axis(x, idx, axis=0, mode="promise_in_bounds")
    rows = (rows0, rows1, rows2, rows3)
    sems = (sem0, sem1, sem2, sem3)
    ns = len(rows)
    units = SC_GROUP * PEER_HEADS

    def gather(tok, hd, slot):
        return pltpu.make_async_copy(tab_hbm.at[idx_v.at[tok, hd]], rows[slot], sems[slot])

    def compute(tok, hd, slot, t):
        rv = rows[slot]

        @pl.when(hd == 0)
        def _():
            def zero(c, cc):
                o_v[pl.ds(c * nl, nl)] = jnp.zeros((nl,), F32)
                return cc
            lax.fori_loop(0, SC_CHUNKS, zero, 0)

        def score(c, accs):
            x = x_v[tok, pl.ds(c * nl, nl)]
            out = []
            for kk in range(PEER_TOPK):
                u = plsc.bitcast(lax.shift_left(rv[kk, pl.ds(c * nl, nl)], 16), F32)
                out.append(accs[kk] + u * x)
            return tuple(out)

        accs = lax.fori_loop(0, SC_CHUNKS, score, tuple(jnp.zeros((nl,), F32) for _ in range(PEER_TOPK)))
        vecs = list(accs)
        d = 1
        while len(vecs) > 1:
            partner = lane ^ d
            take_lo = (lane & d) == 0
            nxt_vecs = []
            for i in range(0, len(vecs), 2):
                a = vecs[i] + permute(vecs[i], partner)
                b = vecs[i + 1] + permute(vecs[i + 1], partner)
                nxt_vecs.append(jnp.where(take_lo, a, b))
            vecs = nxt_vecs
            d *= 2
        s = vecs[0]
        z = 0.7978845608028654 * (s + 0.044715 * s * s * s)
        act = s * (1.0 - 1.0 / (jnp.exp(2.0 * z) + 1.0))
        coef = act * g_v[tok, hd, :]
        coefs = [permute(coef, jnp.full((nl,), kk, I32)) for kk in range(PEER_TOPK)]

        def combine(c, cc):
            parts = []
            for q in range(4):
                acc = None
                for kk in range(q * PEER_TOPK // 4, (q + 1) * PEER_TOPK // 4):
                    v = plsc.bitcast(lax.bitwise_and(rv[kk, pl.ds(c * nl, nl)], jnp.int32(-65536)), F32)
                    term = coefs[kk] * v
                    acc = term if acc is None else acc + term
                parts.append(acc)
            o_v[pl.ds(c * nl, nl)] = o_v[pl.ds(c * nl, nl)] + ((parts[0] + parts[1]) + (parts[2] + parts[3]))
            return cc

        lax.fori_loop(0, SC_CHUNKS, combine, 0)

        @pl.when(hd == PEER_HEADS - 1)
        def _():
            pltpu.sync_copy(o_v, out_hbm.at[t])

    def group(gi, carry):
        t0 = wid * tpw + gi * SC_GROUP
        pltpu.sync_copy(eid_hbm.at[pl.ds(t0, SC_GROUP)], idx_v)
        pltpu.sync_copy(g_hbm.at[pl.ds(t0, SC_GROUP)], g_v)
        pltpu.sync_copy(h_hbm.at[pl.ds(t0, SC_GROUP)], x_v)
        for q in range(ns - 1):
            gather(0, q, q).start()

        def ring(ri, cc):
            j0 = ri * ns
            tok, hd0 = j0 // PEER_HEADS, j0 % PEER_HEADS
            for q in range(ns):
                jn = j0 + q + ns - 1

                @pl.when(jn < units)
                def _():
                    gather(jn // PEER_HEADS, jn % PEER_HEADS, (q + ns - 1) % ns).start()

                gather(tok, hd0 + q, q).wait()
                compute(tok, hd0 + q, q, t0 + tok)
            return cc

        lax.fori_loop(0, units // ns, ring, 0)
        return carry

    lax.fori_loop(0, tpw // SC_GROUP, group, 0)


def _peer_sc(table, eid, gate, h):
    t = h.shape[0]
    tpw = t // SC_WORKERS
    cp = pltpu.CompilerParams()
    if "needs_layout_passes" in pltpu.CompilerParams.__dataclass_fields__:
        cp = dataclasses.replace(cp, needs_layout_passes=False)
    run = pl.kernel(
        functools.partial(_sc_body, tpw=tpw),
        out_type=jax.ShapeDtypeStruct((t, D_MODEL), F32),
        mesh=plsc.VectorSubcoreMesh(core_axis_name="c", subcore_axis_name="s"),
        scratch_types=[
            pltpu.VMEM((SC_GROUP, PEER_HEADS, PEER_TOPK), I32),
            pltpu.VMEM((SC_GROUP, PEER_HEADS, PEER_TOPK), F32),
            pltpu.VMEM((SC_GROUP, D_MODEL), F32),
            pltpu.VMEM((D_MODEL,), F32),
            pltpu.VMEM((PEER_TOPK, D_MODEL), I32),
            pltpu.VMEM((PEER_TOPK, D_MODEL), I32),
            pltpu.VMEM((PEER_TOPK, D_MODEL), I32),
            pltpu.VMEM((PEER_TOPK, D_MODEL), I32),
            pltpu.SemaphoreType.DMA,
            pltpu.SemaphoreType.DMA,
            pltpu.SemaphoreType.DMA,
            pltpu.SemaphoreType.DMA,
        ],
        compiler_params=cp,
        cost_estimate=pl.CostEstimate(
            flops=4 * t * PEER_PICKS * D_MODEL,
            transcendentals=t * PEER_PICKS,
            bytes_accessed=t * PEER_PICKS * D_MODEL * 4 + 2 * t * D_MODEL * 4,
        ),
        name="peer_sc",
    )
    return run(table, eid.reshape(t, PEER_HEADS, PEER_TOPK), gate.reshape(t, PEER_HEADS, PEER_TOPK), h)


def _finish_body(x_ref, po_ref, mod_ref, fg_ref, o_ref, *, final_norm):
    y = x_ref[...] + mod_ref[0, 5:6, :] * po_ref[...]
    if final_norm:
        y = y * lax.rsqrt(jnp.mean(y * y, axis=-1, keepdims=True) + EPS) * fg_ref[...]
    o_ref[...] = y


def _peer_finish(x, po, mod, final_g, seq_len, final_norm, tm=512):
    t = x.shape[0]
    return pl.pallas_call(
        functools.partial(_finish_body, final_norm=final_norm),
        grid=(t // tm,),
        in_specs=[
            pl.BlockSpec((tm, D_MODEL), lambda i: (i, 0)),
            pl.BlockSpec((tm, D_MODEL), lambda i: (i, 0)),
            pl.BlockSpec((1, 6, D_MODEL), lambda i: ((i * tm) // seq_len, 0, 0)),
            pl.BlockSpec((1, D_MODEL), lambda i: (0, 0)),
        ],
        out_specs=pl.BlockSpec((tm, D_MODEL), lambda i: (i, 0)),
        out_shape=jax.ShapeDtypeStruct((t, D_MODEL), F32),
        compiler_params=_cparams("arbitrary"),
        name="peer_finish",
    )(x, po, mod, final_g)


def _peer_layer(x, mod, gamma, w_query, sub_keys, table, final_g, final_norm):
    nb, s, _ = x.shape
    t = nb * s
    h, e_t, g_t = _peer_route(x, mod, gamma, w_query, sub_keys)
    eid = e_t.transpose(0, 3, 1, 2).reshape(t, PEER_PICKS)
    gate = g_t.transpose(0, 3, 1, 2).reshape(t, PEER_PICKS)
    hf = h.reshape(t, D_MODEL)
    xf = x.reshape(t, D_MODEL)
    t_sc = SC_TOKENS
    gate_tc = gate[t_sc:]
    gate_tc = jnp.stack([gate_tc, jnp.zeros_like(gate_tc)], axis=-1).reshape(t - t_sc, 2 * PEER_PICKS)
    y_tc = _peer_gather(eid[t_sc:], hf[t_sc:], gate_tc, table, xf[t_sc:], mod, final_g, s, final_norm, t_sc)
    po = _peer_sc(table.reshape(N_EXPERTS, D_MODEL), eid[:t_sc], gate[:t_sc], hf[:t_sc])
    y_sc = _peer_finish(xf[:t_sc], po, mod, final_g, s, final_norm)
    return jnp.concatenate([y_sc, y_tc], axis=0).reshape(nb, s, D_MODEL)


def kernel(x_prompt, x_sample, c_prompt, c_sample, ln_mix_g, ln_ffn_g, w_mod, b_mod, rec_w_in, rec_conv_w,
           rec_conv_b, rec_ga_w, rec_ga_b, rec_gx_w, rec_gx_b, rec_lam, rec_w_out, att_w_qkv, att_q_g,
           att_k_g, att_w_o, peer_w_query, peer_sub_keys, peer_u, peer_v, final_g):
    tables = [_pack_table(peer_u[i], peer_v[i]) for i in range(DEPTH)]
    fg = final_g.reshape(1, D_MODEL)

    xs = [x_prompt, x_sample]
    cs = [c_prompt, c_sample]
    for i in range(DEPTH):
        gm = ln_mix_g[i].reshape(1, D_MODEL)
        gf = ln_ffn_g[i].reshape(1, D_MODEL)
        j = i // 2
        for tr in range(2):
            x = xs[tr]
            mod = _mod(cs[tr], w_mod[i], b_mod[i])
            if i % 2 == 0:
                x = _rglru_layer(x, mod, gm, rec_w_in[j], rec_conv_w[j], rec_conv_b[j], rec_ga_w[j],
                                 rec_ga_b[j], rec_gx_w[j], rec_gx_b[j], rec_lam[j], rec_w_out[j])
            else:
                x = _attention_layer(x, mod, gm, att_w_qkv[j], att_q_g[j], att_k_g[j], att_w_o[j])
            xs[tr] = _peer_layer(x, mod, gf, peer_w_query[i], peer_sub_keys[i], tables[i], fg,
                                 final_norm=(i == DEPTH - 1))
    return (xs[0], xs[1])
```

```python
import dataclasses
import functools

import jax
import jax.numpy as jnp
from jax import lax
from jax.experimental import pallas as pl
from jax.experimental.pallas import tpu as pltpu
from jax.experimental.pallas import tpu_sc as plsc

F32 = jnp.float32
BF16 = jnp.bfloat16
I32 = jnp.int32

D_MODEL = 1024
DEPTH = 2
GRID_W = 64
EPS = 1e-6
RNN_WIDTH = D_MODEL
RNN_BLOCKS = 16
RNN_BLOCK_W = RNN_WIDTH // RNN_BLOCKS
CONV_W = 4
LRU_C = 8.0
N_HEADS = 16
N_KV_HEADS = 4
HEAD_DIM = D_MODEL // N_HEADS
GROUP = N_HEADS // N_KV_HEADS
AXIS_DIM = HEAD_DIM // 2
ROPE_THETA = 10000.0
N_KEYS = 128
N_EXPERTS = N_KEYS * N_KEYS
PEER_HEADS = 8
PEER_TOPK = 16
PEER_QDIM = 256
PEER_HALF = PEER_QDIM // 2
PEER_PICKS = PEER_HEADS * PEER_TOPK

LANES = 128
SUBLANES = 8
VMEM_LIMIT = 48 * 1024 * 1024

GATHER_PITCH = 12
GATHER_TOKENS = 64
GATHER_SLOTS = 4
ROWS_PER_EXPERT = D_MODEL // LANES


def _cparams(*sem):
    return pltpu.CompilerParams(dimension_semantics=sem, vmem_limit_bytes=VMEM_LIMIT)


def _gelu(x):
    return jax.nn.gelu(x)


def _norm_mod(x, gamma, scale, shift):
    ms = jnp.mean(x * x, axis=-1, keepdims=True)
    y = x * lax.rsqrt(ms + EPS) * gamma
    return y * (1.0 + scale) + shift


def _mod_body(c_ref, w_ref, b_ref, o_ref):
    c = c_ref[...]
    s = c * jax.nn.sigmoid(c)
    o_ref[...] = jnp.dot(s, w_ref[...], preferred_element_type=F32) + b_ref[...]


def _mod(c, w, b):
    nb, n = c.shape[0], w.shape[1]
    tn = 1536
    out = pl.pallas_call(
        _mod_body,
        grid=(n // tn,),
        in_specs=[
            pl.BlockSpec((nb, D_MODEL), lambda j: (0, 0)),
            pl.BlockSpec((D_MODEL, tn), lambda j: (0, j)),
            pl.BlockSpec((1, tn), lambda j: (0, j)),
        ],
        out_specs=pl.BlockSpec((nb, tn), lambda j: (0, j)),
        out_shape=jax.ShapeDtypeStruct((nb, n), F32),
        compiler_params=_cparams("arbitrary"),
        name="adaln_mod",
    )(c, w, b.reshape(1, n))
    return out.reshape(nb, 6, D_MODEL)


def _nmm_body(x_ref, mod_ref, g_ref, w_ref, o_ref, *, sh, sc):
    h = _norm_mod(x_ref[0], g_ref[...], mod_ref[0, sc:sc + 1, :], mod_ref[0, sh:sh + 1, :])
    o_ref[0] = jnp.dot(h.astype(BF16), w_ref[...], preferred_element_type=F32)


def _norm_mod_matmul(x, mod, gamma, w_bf16, sh, sc, tm=512):
    nb, s, _ = x.shape
    n = w_bf16.shape[1]
    return pl.pallas_call(
        functools.partial(_nmm_body, sh=sh, sc=sc),
        grid=(nb, s // tm),
        in_specs=[
            pl.BlockSpec((1, tm, D_MODEL), lambda b, i: (b, i, 0)),
            pl.BlockSpec((1, 6, D_MODEL), lambda b, i: (b, 0, 0)),
            pl.BlockSpec((1, D_MODEL), lambda b, i: (0, 0)),
            pl.BlockSpec((D_MODEL, n), lambda b, i: (0, 0)),
        ],
        out_specs=pl.BlockSpec((1, tm, n), lambda b, i: (b, i, 0)),
        out_shape=jax.ShapeDtypeStruct((nb, s, n), F32),
        compiler_params=_cparams("arbitrary", "arbitrary"),
        name="norm_mod_matmul",
    )(x, mod, gamma, w_bf16)


def _proj_res_body(m_ref, w_ref, x_ref, mod_ref, o_ref, *, gi):
    y = jnp.dot(m_ref[0].astype(BF16), w_ref[...], preferred_element_type=F32)
    o_ref[0] = x_ref[0] + mod_ref[0, gi:gi + 1, :] * y


def _proj_residual(m, w_bf16, x, mod, gi, tm=512):
    nb, s, k = m.shape
    return pl.pallas_call(
        functools.partial(_proj_res_body, gi=gi),
        grid=(nb, s // tm),
        in_specs=[
            pl.BlockSpec((1, tm, k), lambda b, i: (b, i, 0)),
            pl.BlockSpec((k, D_MODEL), lambda b, i: (0, 0)),
            pl.BlockSpec((1, tm, D_MODEL), lambda b, i: (b, i, 0)),
            pl.BlockSpec((1, 6, D_MODEL), lambda b, i: (b, 0, 0)),
        ],
        out_specs=pl.BlockSpec((1, tm, D_MODEL), lambda b, i: (b, i, 0)),
        out_shape=jax.ShapeDtypeStruct((nb, s, D_MODEL), F32),
        compiler_params=_cparams("arbitrary", "arbitrary"),
        name="proj_residual",
    )(m, w_bf16, x, mod)


def _scan_tile(a, b, carry, reverse):
    tt = a.shape[0]
    row = lax.broadcasted_iota(I32, (tt, LANES), 0) % SUBLANES
    for d in (1, 2, 4):
        if reverse:
            shift, keep = tt - d, row < SUBLANES - d
        else:
            shift, keep = d, row >= d
        ap = pltpu.roll(a, shift, 0)
        bp = pltpu.roll(b, shift, 0)
        b = jnp.where(keep, a * bp + b, b)
        a = jnp.where(keep, a * ap, a)
    groups = tt // SUBLANES
    hs = [None] * groups
    order = range(groups - 1, -1, -1) if reverse else range(groups)
    for g in order:
        lo = g * SUBLANES
        h = a[lo:lo + SUBLANES] * carry + b[lo:lo + SUBLANES]
        hs[g] = h
        last = h[0:1] if reverse else h[SUBLANES - 1:SUBLANES]
        carry = jnp.broadcast_to(last, (SUBLANES, LANES))
    return jnp.concatenate(hs, axis=0), carry


def _rglru_body(y_ref, x_ref, cw_ref, cb_ref, gaw_ref, gab_ref, gxw_ref, gxb_ref, nc_ref,
                o_ref, xp, hf, *, s, tt):
    nt = s // tt
    pad = SUBLANES
    zeros = jnp.zeros((pad, LANES), F32)
    xp[0:pad, :] = zeros
    xp[s + pad:s + 2 * pad, :] = zeros

    def copy_tile(i, c):
        r0 = pl.multiple_of(i * tt, tt)
        xp[pl.ds(pl.multiple_of(r0 + pad, SUBLANES), tt), :] = x_ref[0, pl.ds(r0, tt), :]
        return c

    lax.fori_loop(0, nt, copy_tile, 0)

    cw = cw_ref[...]
    cb = cb_ref[...]

    def conv_tile(r0):
        win = xp[pl.ds(r0, tt + 2 * pad), :]
        acc = cb + cw[2:3] * win[pad:pad + tt]
        for k in (0, 1, 3):
            shifted = pltpu.roll(win, (2 - k) % (tt + 2 * pad), 0)
            acc = acc + cw[k:k + 1] * shifted[pad:pad + tt]
        return acc

    def gates(d, xc):
        xb = xc.astype(BF16)
        r = jax.nn.sigmoid(jnp.dot(xb, gaw_ref[d, 0], preferred_element_type=F32) + gab_ref[d])
        ig = jax.nn.sigmoid(jnp.dot(xb, gxw_ref[d, 0], preferred_element_type=F32) + gxb_ref[d])
        a = jnp.exp(nc_ref[d] * r)
        b = jnp.sqrt(1.0 - a * a) * (ig * xc)
        return a, b

    carry0 = jnp.zeros((SUBLANES, LANES), F32)

    def fwd(i, carry):
        r0 = pl.multiple_of(i * tt, tt)
        a, b = gates(0, conv_tile(r0))
        h, carry = _scan_tile(a, b, carry, False)
        hf[pl.ds(r0, tt), :] = h
        return carry

    lax.fori_loop(0, nt, fwd, carry0)

    def bwd(ii, carry):
        r0 = pl.multiple_of((nt - 1 - ii) * tt, tt)
        a, b = gates(1, conv_tile(r0))
        h, carry = _scan_tile(a, b, carry, True)
        o_ref[0, pl.ds(r0, tt), :] = (hf[pl.ds(r0, tt), :] + h) * _gelu(y_ref[0, pl.ds(r0, tt), :])
        return carry

    lax.fori_loop(0, nt, bwd, carry0)


def _rglru_core(u, conv_w, conv_b, gaw, gab, gxw, gxb, negc, tt=256):
    nb, s, _ = u.shape
    ng = RNN_WIDTH // LANES
    return pl.pallas_call(
        functools.partial(_rglru_body, s=s, tt=tt),
        grid=(nb, ng),
        in_specs=[
            pl.BlockSpec((1, s, LANES), lambda b, j: (b, 0, j)),
            pl.BlockSpec((1, s, LANES), lambda b, j: (b, 0, ng + j)),
            pl.BlockSpec((CONV_W, LANES), lambda b, j: (0, j)),
            pl.BlockSpec((1, LANES), lambda b, j: (0, j)),
            pl.BlockSpec((2, 1, LANES, LANES), lambda b, j: (0, j, 0, 0)),
            pl.BlockSpec((2, 1, LANES), lambda b, j: (0, 0, j)),
            pl.BlockSpec((2, 1, LANES, LANES), lambda b, j: (0, j, 0, 0)),
            pl.BlockSpec((2, 1, LANES), lambda b, j: (0, 0, j)),
            pl.BlockSpec((2, 1, LANES), lambda b, j: (0, 0, j)),
        ],
        out_specs=pl.BlockSpec((1, s, LANES), lambda b, j: (b, 0, j)),
        out_shape=jax.ShapeDtypeStruct((nb, s, RNN_WIDTH), F32),
        scratch_shapes=[pltpu.VMEM((s + 2 * SUBLANES, LANES), F32), pltpu.VMEM((s, LANES), F32)],
        compiler_params=_cparams("arbitrary", "arbitrary"),
        name="rglru_core",
    )(u, u, conv_w, conv_b, gaw, gab, gxw, gxb, negc)


def _block_diag_groups(w):
    ng = RNN_WIDTH // LANES
    w = w.reshape(ng, 2, RNN_BLOCK_W, RNN_BLOCK_W)
    out = jnp.zeros((ng, LANES, LANES), w.dtype)
    out = out.at[:, :RNN_BLOCK_W, :RNN_BLOCK_W].set(w[:, 0])
    out = out.at[:, RNN_BLOCK_W:, RNN_BLOCK_W:].set(w[:, 1])
    return out


def _rglru_layer(x, mod, gamma, w_in, conv_w, conv_b, ga_w, ga_b, gx_w, gx_b, lam, w_out):
    u = _norm_mod_matmul(x, mod, gamma, w_in.astype(BF16), sh=0, sc=1)
    gaw = jnp.stack([_block_diag_groups(ga_w[d]) for d in range(2)]).astype(BF16)
    gxw = jnp.stack([_block_diag_groups(gx_w[d]) for d in range(2)]).astype(BF16)
    gab = ga_b.reshape(2, 1, RNN_WIDTH)
    gxb = gx_b.reshape(2, 1, RNN_WIDTH)
    negc = (-LRU_C * jax.nn.softplus(-lam)).reshape(2, 1, RNN_WIDTH)
    m = _rglru_core(u, conv_w, conv_b.reshape(1, RNN_WIDTH), gaw, gab, gxw, gxb, negc)
    return _proj_residual(m, w_out.astype(BF16), x, mod, gi=2)


def _seg_mean(x2, seg_ref):
    hi = x2.astype(BF16)
    lo = (x2 - hi.astype(F32)).astype(BF16)
    return (jnp.dot(hi, seg_ref[...], preferred_element_type=F32)
            + jnp.dot(lo, seg_ref[...], preferred_element_type=F32))


def _rope(x, cos, sin, lane_lo):
    outs = []
    for j in range(x.shape[1] // LANES):
        xt = x[:, j * LANES:(j + 1) * LANES]
        rot = jnp.where(lane_lo, pltpu.roll(xt, LANES - AXIS_DIM // 2, 1), pltpu.roll(xt, AXIS_DIM // 2, 1))
        outs.append(xt * cos + rot * sin)
    return jnp.concatenate(outs, axis=1) if len(outs) > 1 else outs[0]


def _qkv_body(x_ref, mod_ref, g_ref, w_ref, segq_ref, segk_ref, qg_ref, kg_ref, cosq_ref, sinq_ref,
              cosk_ref, sink_ref, q_ref, kt_ref, v_ref, *, tm):
    h = _norm_mod(x_ref[0], g_ref[...], mod_ref[0, 1:2, :], mod_ref[0, 0:1, :])
    qkv = jnp.dot(h.astype(BF16), w_ref[...], preferred_element_type=F32)
    nq = N_HEADS * HEAD_DIM
    nk = N_KV_HEADS * HEAD_DIM
    q = qkv[:, :nq]
    k = qkv[:, nq:nq + nk]
    v = qkv[:, nq + nk:]
    lane = lax.broadcasted_iota(I32, (tm, LANES), 1)
    lane_lo = (lane % AXIS_DIM) < (AXIS_DIM // 2)
    q = q * lax.rsqrt(_seg_mean(q * q, segq_ref) + EPS) * qg_ref[...]
    k = k * lax.rsqrt(_seg_mean(k * k, segk_ref) + EPS) * kg_ref[...]
    q = _rope(q, cosq_ref[...], sinq_ref[...], lane_lo)
    k = _rope(k, cosk_ref[...], sink_ref[...], lane_lo)
    q_ref[0] = q.astype(BF16)
    kt = k.T.astype(BF16)
    for g in range(N_KV_HEADS):
        kt_ref[0, g] = kt[g * HEAD_DIM:(g + 1) * HEAD_DIM, :]
        v_ref[0, g] = v[:, g * HEAD_DIM:(g + 1) * HEAD_DIM].astype(BF16)


def _rope_tables(s):
    rows = s // GRID_W
    row = jnp.repeat(jnp.arange(rows, dtype=F32), GRID_W)
    col = jnp.tile(jnp.arange(GRID_W, dtype=F32), rows)
    inv = ROPE_THETA ** (-jnp.arange(0, AXIS_DIM, 2, dtype=F32) / AXIS_DIM)
    ar = row[:, None] * inv
    ac = col[:, None] * inv
    cos = jnp.concatenate([jnp.cos(ar), jnp.cos(ar), jnp.cos(ac), jnp.cos(ac)], axis=1)
    sin = jnp.concatenate([-jnp.sin(ar), jnp.sin(ar), -jnp.sin(ac), jnp.sin(ac)], axis=1)
    return jnp.tile(cos, (1, LANES // HEAD_DIM)), jnp.tile(sin, (1, LANES // HEAD_DIM))


def _attn_body(q_ref, kt_ref, v_ref, o_ref):
    kt = kt_ref[0, 0]
    v = v_ref[0, 0]
    outs = []
    for hh in range(GROUP):
        qh = q_ref[0, :, hh * HEAD_DIM:(hh + 1) * HEAD_DIM]
        sc = jnp.dot(qh, kt, preferred_element_type=F32)
        m = jnp.max(sc, axis=-1, keepdims=True)
        p = jnp.exp(sc - m)
        l = jnp.sum(p, axis=-1, keepdims=True)
        o = jnp.dot(p.astype(BF16), v, preferred_element_type=F32)
        outs.append(o / l)
    o_ref[0] = jnp.concatenate(outs, axis=1).astype(BF16)


def _attention_layer(x, mod, gamma, w_qkv, q_g, k_g, w_o, tm=256, tq=128):
    nb, s, _ = x.shape
    nq = N_HEADS * HEAD_DIM
    nk = N_KV_HEADS * HEAD_DIM
    seg = jnp.kron(jnp.eye(N_HEADS, dtype=F32), jnp.full((HEAD_DIM, HEAD_DIM), 1.0 / HEAD_DIM, F32)).astype(BF16)
    segk = seg[:nk, :nk]
    cos, sin = _rope_tables(s)
    scale = HEAD_DIM ** -0.5
    q, kt, v = pl.pallas_call(
        functools.partial(_qkv_body, tm=tm),
        grid=(nb, s // tm),
        in_specs=[
            pl.BlockSpec((1, tm, D_MODEL), lambda b, i: (b, i, 0)),
            pl.BlockSpec((1, 6, D_MODEL), lambda b, i: (b, 0, 0)),
            pl.BlockSpec((1, D_MODEL), lambda b, i: (0, 0)),
            pl.BlockSpec((D_MODEL, nq + 2 * nk), lambda b, i: (0, 0)),
            pl.BlockSpec((nq, nq), lambda b, i: (0, 0)),
            pl.BlockSpec((nk, nk), lambda b, i: (0, 0)),
            pl.BlockSpec((1, nq), lambda b, i: (0, 0)),
            pl.BlockSpec((1, nk), lambda b, i: (0, 0)),
            pl.BlockSpec((tm, LANES), lambda b, i: (i, 0)),
            pl.BlockSpec((tm, LANES), lambda b, i: (i, 0)),
            pl.BlockSpec((tm, LANES), lambda b, i: (i, 0)),
            pl.BlockSpec((tm, LANES), lambda b, i: (i, 0)),
        ],
        out_specs=[
            pl.BlockSpec((1, tm, nq), lambda b, i: (b, i, 0)),
            pl.BlockSpec((1, N_KV_HEADS, HEAD_DIM, tm), lambda b, i: (b, 0, 0, i)),
            pl.BlockSpec((1, N_KV_HEADS, tm, HEAD_DIM), lambda b, i: (b, 0, i, 0)),
        ],
        out_shape=[
            jax.ShapeDtypeStruct((nb, s, nq), BF16),
            jax.ShapeDtypeStruct((nb, N_KV_HEADS, HEAD_DIM, s), BF16),
            jax.ShapeDtypeStruct((nb, N_KV_HEADS, s, HEAD_DIM), BF16),
        ],
        compiler_params=_cparams("arbitrary", "arbitrary"),
        name="qkv_rope",
    )(x, mod, gamma, w_qkv.astype(BF16), seg, segk,
      jnp.tile(q_g, N_HEADS).reshape(1, nq), jnp.tile(k_g, N_KV_HEADS).reshape(1, nk),
      cos * scale, sin * scale, cos, sin)

    gw = GROUP * HEAD_DIM
    o = pl.pallas_call(
        _attn_body,
        grid=(nb, N_KV_HEADS, s // tq),
        in_specs=[
            pl.BlockSpec((1, tq, gw), lambda b, g, i: (b, i, g)),
            pl.BlockSpec((1, 1, HEAD_DIM, s), lambda b, g, i: (b, g, 0, 0)),
            pl.BlockSpec((1, 1, s, HEAD_DIM), lambda b, g, i: (b, g, 0, 0)),
        ],
        out_specs=pl.BlockSpec((1, tq, gw), lambda b, g, i: (b, i, g)),
        out_shape=jax.ShapeDtypeStruct((nb, s, nq), BF16),
        compiler_params=_cparams("arbitrary", "arbitrary", "arbitrary"),
        name="attention",
    )(q, kt, v)
    return _proj_residual(o, w_o.astype(BF16), x, mod, gi=2)


def _topk_rows(s, ids, k, id_bound):
    vals, picks = [], []
    for _ in range(k):
        m = jnp.max(s, axis=0, keepdims=True)
        i = jnp.min(jnp.where(s == m, ids, id_bound), axis=0, keepdims=True)
        vals.append(m)
        picks.append(i)
        s = jnp.where(ids == i, -jnp.inf, s)
    return jnp.concatenate(vals, axis=0), jnp.concatenate(picks, axis=0)


def _route_body(x_ref, mod_ref, g_ref, wq_ref, keys_ref, h_ref, e_ref, gate_ref, hb, *, tm):
    hd = pl.program_id(2)

    @pl.when(hd == 0)
    def _():
        h = _norm_mod(x_ref[0], g_ref[...], mod_ref[0, 4:5, :], mod_ref[0, 3:4, :])
        h_ref[0] = h
        hb[...] = h.astype(BF16)

    q = jnp.dot(hb[...], wq_ref[...], preferred_element_type=F32)
    key_ids = lax.broadcasted_iota(I32, (N_KEYS, tm), 0)
    tops = []
    for p in range(2):
        qp = q[:, p * PEER_HALF:(p + 1) * PEER_HALF].astype(BF16)
        st = lax.dot_general(keys_ref[p], qp, (((1,), (1,)), ((), ())), preferred_element_type=F32)
        tops.append(_topk_rows(st, key_ids, PEER_TOPK, N_KEYS))
    (v0, i0), (v1, i1) = tops

    r16 = lax.broadcasted_iota(I32, (PEER_TOPK, tm), 0)
    cand, fid, eid = [], [], []
    for a in range(4):
        cand.append(v0[a:a + 1] + v1)
        fid.append(a * PEER_TOPK + r16)
        eid.append(i0[a:a + 1] * N_KEYS + i1)
    for b in range(3):
        cand.append(jnp.where(r16 >= 4, v0 + v1[b:b + 1], -jnp.inf))
        fid.append(r16 * PEER_TOPK + b)
        eid.append(i0 * N_KEYS + i1[b:b + 1])
    cand = jnp.concatenate(cand, axis=0)
    fid = jnp.concatenate(fid, axis=0)
    eid = jnp.concatenate(eid, axis=0)

    best, chosen = [], []
    for _ in range(PEER_TOPK):
        m = jnp.max(cand, axis=0, keepdims=True)
        f = jnp.min(jnp.where(cand == m, fid, PEER_TOPK * PEER_TOPK), axis=0, keepdims=True)
        hit = fid == f
        best.append(m)
        chosen.append(jnp.max(jnp.where(hit, eid, -1), axis=0, keepdims=True))
        cand = jnp.where(hit, -jnp.inf, cand)
    best = jnp.concatenate(best, axis=0)
    ex = jnp.exp(best - best[0:1])
    gate_ref[0, 0] = ex / jnp.sum(ex, axis=0, keepdims=True)
    e_ref[0, 0] = jnp.concatenate(chosen, axis=0)


def _peer_route(x, mod, gamma, w_query, sub_keys, tm=128):
    nb, s, _ = x.shape
    return pl.pallas_call(
        functools.partial(_route_body, tm=tm),
        grid=(nb, s // tm, PEER_HEADS),
        in_specs=[
            pl.BlockSpec((1, tm, D_MODEL), lambda b, i, h: (b, i, 0)),
            pl.BlockSpec((1, 6, D_MODEL), lambda b, i, h: (b, 0, 0)),
            pl.BlockSpec((1, D_MODEL), lambda b, i, h: (0, 0)),
            pl.BlockSpec((D_MODEL, PEER_QDIM), lambda b, i, h: (0, h)),
            pl.BlockSpec((2, N_KEYS, PEER_HALF), lambda b, i, h: (0, 0, 0)),
        ],
        out_specs=[
            pl.BlockSpec((1, tm, D_MODEL), lambda b, i, h: (b, i, 0)),
            pl.BlockSpec((1, 1, PEER_TOPK, tm), lambda b, i, h: (b, h, 0, i)),
            pl.BlockSpec((1, 1, PEER_TOPK, tm), lambda b, i, h: (b, h, 0, i)),
        ],
        out_shape=[
            jax.ShapeDtypeStruct((nb, s, D_MODEL), F32),
            jax.ShapeDtypeStruct((nb, PEER_HEADS, PEER_TOPK, s), I32),
            jax.ShapeDtypeStruct((nb, PEER_HEADS, PEER_TOPK, s), F32),
        ],
        scratch_shapes=[pltpu.VMEM((tm, D_MODEL), BF16)],
        compiler_params=_cparams("arbitrary", "arbitrary", "arbitrary"),
        name="peer_route",
    )(x, mod, gamma, w_query.astype(BF16), sub_keys.astype(BF16))


def _pack_table(u, v):
    ne = u.shape[0]
    pairs = jnp.stack([u.astype(BF16).reshape(ne * ROWS_PER_EXPERT, LANES),
                       v.astype(BF16).reshape(ne * ROWS_PER_EXPERT, LANES)], axis=-1)
    return lax.bitcast_convert_type(pairs, I32)


def _gather_body(eid_ref, h_ref, g_ref, tab_ref, x_ref, mod_ref, fg_ref, o_ref, buf, sem, po,
                 *, tb, final_norm):
    ne = PEER_PICKS
    rows = ROWS_PER_EXPERT

    ns = GATHER_SLOTS
    ahead = ns - 1

    def issue(t, slot, lo, hi):
        for k in range(lo, hi):
            src = tab_ref.at[pl.ds(pl.multiple_of(eid_ref[t, k] * rows, rows), rows), :]
            dst = buf.at[slot, pl.ds(GATHER_PITCH * k, rows), :]
            pltpu.make_async_copy(src, dst, sem.at[slot]).start(priority=k % 2)

    def wait(slot):
        pltpu.make_async_copy(tab_ref.at[pl.ds(0, ne * rows), :], buf.at[slot, pl.ds(0, ne * rows), :],
                              sem.at[slot]).wait()

    for q in range(ahead):
        issue(q, q, 0, ne)
    early = 4
    halves = 2
    per_half = ne // halves
    even = lax.broadcasted_iota(I32, (1, 2 * LANES), 1) % 2 == 0
    row_id = lax.broadcasted_iota(I32, (SUBLANES, LANES), 0)

    def chunk(slot, c, first=0, count=ne):
        words = buf[slot, pl.ds(first * GATHER_PITCH + c, count, stride=GATHER_PITCH), :]
        return pltpu.bitcast(words, BF16)

    def score_half(t8, j, hf, c, acc):
        xrow = h_ref[pl.ds(t8, SUBLANES), c * LANES:(c + 1) * LANES][j:j + 1]
        return acc + chunk(j % ns, c, hf * per_half, per_half).astype(F32) * xrow

    def lane_sums(acc):
        return jnp.sum(acc.T, axis=0, keepdims=True)

    def coefficients(parts, t8, j):
        s = jnp.concatenate(parts, axis=1)
        g = g_ref[pl.ds(t8, SUBLANES), :][j:j + 1]
        coef = pltpu.roll(jnp.where(even, _gelu(s) * g, 0.0), 1, 1)
        return jnp.broadcast_to(coef, (SUBLANES, 2 * LANES))

    wait(0)
    parts = []
    for hf in range(halves):
        a = jnp.zeros((2 * per_half, LANES), F32)
        for c in range(rows):
            a = score_half(0, 0, hf, c, a)
        parts.append(lane_sums(a))
    coef0 = coefficients(parts, 0, 0)

    def body(it, coef):
        t8 = pl.multiple_of(it * SUBLANES, SUBLANES)
        t8_next = pl.multiple_of(jnp.minimum(t8 + SUBLANES, tb - SUBLANES), SUBLANES)
        acc = [None] * rows
        for j in range(SUBLANES):
            jn = (j + 1) % SUBLANES
            t8n = t8 if j + 1 < SUBLANES else t8_next
            nxt = jnp.minimum(t8 + j + ahead, tb - 1)
            nslot = (j + ahead) % ns
            wait((j + 1) % ns)
            coef_b = coef.astype(BF16)
            step = 0
            parts = []
            for hf in range(halves):
                a = jnp.zeros((2 * per_half, LANES), F32)
                for c in range(rows):
                    issue(nxt, nslot, step * early, (step + 1) * early)
                    step += 1
                    a = score_half(t8n, jn, hf, c, a)
                parts.append(lane_sums(a))
            for c in range(rows):
                issue(nxt, nslot, step * early, (step + 1) * early)
                step += 1
                o = jnp.dot(coef_b, chunk(j % ns, c), preferred_element_type=F32)
                acc[c] = o if j == 0 else jnp.where(row_id == j, o, acc[c])
            issue(nxt, nslot, step * early, ne)
            coef = coefficients(parts, t8n, jn)
        for c in range(rows):
            po[pl.ds(t8, SUBLANES), c * LANES:(c + 1) * LANES] = acc[c]
        return coef

    lax.fori_loop(0, tb // SUBLANES, body, coef0)
    for q in range(1, ahead):
        wait((tb + q) % ns)
    y = x_ref[...] + mod_ref[0, 5:6, :] * po[...]
    if final_norm:
        y = y * lax.rsqrt(jnp.mean(y * y, axis=-1, keepdims=True) + EPS) * fg_ref[...]
    o_ref[...] = y


def _peer_gather(eid, h, g, table, x, mod, final_g, seq_len, final_norm, tok_offset, tb=GATHER_TOKENS):
    t = h.shape[0]
    return pl.pallas_call(
        functools.partial(_gather_body, tb=tb, final_norm=final_norm),
        grid=(t // tb,),
        in_specs=[
            pl.BlockSpec((tb, PEER_PICKS), lambda i: (i, 0), memory_space=pltpu.SMEM),
            pl.BlockSpec((tb, D_MODEL), lambda i: (i, 0)),
            pl.BlockSpec((tb, 2 * PEER_PICKS), lambda i: (i, 0)),
            pl.BlockSpec(memory_space=pl.ANY),
            pl.BlockSpec((tb, D_MODEL), lambda i: (i, 0)),
            pl.BlockSpec((1, 6, D_MODEL), lambda i: ((tok_offset + i * tb) // seq_len, 0, 0)),
            pl.BlockSpec((1, D_MODEL), lambda i: (0, 0)),
        ],
        out_specs=pl.BlockSpec((tb, D_MODEL), lambda i: (i, 0)),
        out_shape=jax.ShapeDtypeStruct((t, D_MODEL), F32),
        scratch_shapes=[
            pltpu.VMEM((GATHER_SLOTS, PEER_PICKS * GATHER_PITCH, LANES), I32),
            pltpu.SemaphoreType.DMA((GATHER_SLOTS,)),
            pltpu.VMEM((tb, D_MODEL), F32),
        ],
        compiler_params=_cparams("arbitrary"),
        name="peer_gather",
    )(eid, h, g, table, x, mod, final_g)


SC_LANES = 16
SC_WORKERS = 32
SC_GROUP = 8
SC_TOKENS = ((8192, 14336), (9216, 12288))
SC_COST_SCALE = 1
TRUNK_ORDER = ((0, 1), (0, 1))
SC_CHUNKS = D_MODEL // SC_LANES


def _sc_body(tab_hbm, eid_hbm, g_hbm, h_hbm, out_hbm, idx_v, g_v, x_v, o_v, rows0, rows1, rows2, rows3,
             sem0, sem1, sem2, sem3, *, tpw):
    nl = SC_LANES
    wid = lax.axis_index("s") * 2 + lax.axis_index("c")
    lane = lax.iota(I32, nl)

    def permute(x, idx):
        return jnp.take_along_axis(x, idx, axis=0, mode="promise_in_bounds")
    rows = (rows0, rows1, rows2, rows3)
    sems = (sem0, sem1, sem2, sem3)
    ns = len(rows)
    units = SC_GROUP * PEER_HEADS

    def gather(tok, hd, slot):
        return pltpu.make_async_copy(tab_hbm.at[idx_v.at[tok, hd]], rows[slot], sems[slot])

    def compute(tok, hd, slot, t):
        rv = rows[slot]

        @pl.when(hd == 0)
        def _():
            def zero(c, cc):
                o_v[pl.ds(c * nl, nl)] = jnp.zeros((nl,), F32)
                return cc
            lax.fori_loop(0, SC_CHUNKS, zero, 0)

        def score(c, accs):
            x = x_v[tok, pl.ds(c * nl, nl)]
            out = []
            for kk in range(PEER_TOPK):
                u = plsc.bitcast(lax.shift_left(rv[kk, pl.ds(c * nl, nl)], 16), F32)
                out.append(accs[kk] + u * x)
            return tuple(out)

        accs = lax.fori_loop(0, SC_CHUNKS, score, tuple(jnp.zeros((nl,), F32) for _ in range(PEER_TOPK)))
        vecs = list(accs)
        d = 1
        while len(vecs) > 1:
            partner = lane ^ d
            take_lo = (lane & d) == 0
            nxt_vecs = []
            for i in range(0, len(vecs), 2):
                a = vecs[i] + permute(vecs[i], partner)
                b = vecs[i + 1] + permute(vecs[i + 1], partner)
                nxt_vecs.append(jnp.where(take_lo, a, b))
            vecs = nxt_vecs
            d *= 2
        s = vecs[0]
        z = 0.7978845608028654 * (s + 0.044715 * s * s * s)
        act = s * (1.0 - 1.0 / (jnp.exp(2.0 * z) + 1.0))
        coef = act * g_v[tok, hd, :]
        coefs = [permute(coef, jnp.full((nl,), kk, I32)) for kk in range(PEER_TOPK)]

        def combine(c, cc):
            parts = []
            for q in range(4):
                acc = None
                for kk in range(q * PEER_TOPK // 4, (q + 1) * PEER_TOPK // 4):
                    v = plsc.bitcast(lax.bitwise_and(rv[kk, pl.ds(c * nl, nl)], jnp.int32(-65536)), F32)
                    term = coefs[kk] * v
                    acc = term if acc is None else acc + term
                parts.append(acc)
            o_v[pl.ds(c * nl, nl)] = o_v[pl.ds(c * nl, nl)] + ((parts[0] + parts[1]) + (parts[2] + parts[3]))
            return cc

        lax.fori_loop(0, SC_CHUNKS, combine, 0)

        @pl.when(hd == PEER_HEADS - 1)
        def _():
            pltpu.sync_copy(o_v, out_hbm.at[t])

    def group(gi, carry):
        t0 = wid * tpw + gi * SC_GROUP
        pltpu.sync_copy(eid_hbm.at[pl.ds(t0, SC_GROUP)], idx_v)
        pltpu.sync_copy(g_hbm.at[pl.ds(t0, SC_GROUP)], g_v)
        pltpu.sync_copy(h_hbm.at[pl.ds(t0, SC_GROUP)], x_v)
        for q in range(ns - 1):
            gather(0, q, q).start()

        def ring(ri, cc):
            j0 = ri * ns
            tok, hd0 = j0 // PEER_HEADS, j0 % PEER_HEADS
            for q in range(ns):
                jn = j0 + q + ns - 1

                @pl.when(jn < units)
                def _():
                    gather(jn // PEER_HEADS, jn % PEER_HEADS, (q + ns - 1) % ns).start()

                gather(tok, hd0 + q, q).wait()
                compute(tok, hd0 + q, q, t0 + tok)
            return cc

        lax.fori_loop(0, units // ns, ring, 0)
        return carry

    lax.fori_loop(0, tpw // SC_GROUP, group, 0)


def _peer_sc(table, eid, gate, h):
    t = h.shape[0]
    tpw = t // SC_WORKERS
    cp = pltpu.CompilerParams()
    if "needs_layout_passes" in pltpu.CompilerParams.__dataclass_fields__:
        cp = dataclasses.replace(cp, needs_layout_passes=False)
    run = pl.kernel(
        functools.partial(_sc_body, tpw=tpw),
        out_type=jax.ShapeDtypeStruct((t, D_MODEL), F32),
        mesh=plsc.VectorSubcoreMesh(core_axis_name="c", subcore_axis_name="s"),
        scratch_types=[
            pltpu.VMEM((SC_GROUP, PEER_HEADS, PEER_TOPK), I32),
            pltpu.VMEM((SC_GROUP, PEER_HEADS, PEER_TOPK), F32),
            pltpu.VMEM((SC_GROUP, D_MODEL), F32),
            pltpu.VMEM((D_MODEL,), F32),
            pltpu.VMEM((PEER_TOPK, D_MODEL), I32),
            pltpu.VMEM((PEER_TOPK, D_MODEL), I32),
            pltpu.VMEM((PEER_TOPK, D_MODEL), I32),
            pltpu.VMEM((PEER_TOPK, D_MODEL), I32),
            pltpu.SemaphoreType.DMA,
            pltpu.SemaphoreType.DMA,
            pltpu.SemaphoreType.DMA,
            pltpu.SemaphoreType.DMA,
        ],
        compiler_params=cp,
        cost_estimate=pl.CostEstimate(
            flops=SC_COST_SCALE * 4 * t * PEER_PICKS * D_MODEL,
            transcendentals=t * PEER_PICKS,
            bytes_accessed=SC_COST_SCALE * (t * PEER_PICKS * D_MODEL * 4 + 2 * t * D_MODEL * 4),
        ),
        name="peer_sc",
    )
    return run(table, eid.reshape(t, PEER_HEADS, PEER_TOPK), gate.reshape(t, PEER_HEADS, PEER_TOPK), h)


def _finish_body(x_ref, po_ref, mod_ref, fg_ref, o_ref, *, final_norm):
    y = x_ref[...] + mod_ref[0, 5:6, :] * po_ref[...]
    if final_norm:
        y = y * lax.rsqrt(jnp.mean(y * y, axis=-1, keepdims=True) + EPS) * fg_ref[...]
    o_ref[...] = y


def _peer_finish(x, po, mod, final_g, seq_len, final_norm, tm=512):
    t = x.shape[0]
    return pl.pallas_call(
        functools.partial(_finish_body, final_norm=final_norm),
        grid=(t // tm,),
        in_specs=[
            pl.BlockSpec((tm, D_MODEL), lambda i: (i, 0)),
            pl.BlockSpec((tm, D_MODEL), lambda i: (i, 0)),
            pl.BlockSpec((1, 6, D_MODEL), lambda i: ((i * tm) // seq_len, 0, 0)),
            pl.BlockSpec((1, D_MODEL), lambda i: (0, 0)),
        ],
        out_specs=pl.BlockSpec((tm, D_MODEL), lambda i: (i, 0)),
        out_shape=jax.ShapeDtypeStruct((t, D_MODEL), F32),
        compiler_params=_cparams("arbitrary"),
        name="peer_finish",
    )(x, po, mod, final_g)


def _peer_route_phase(x, mod, gamma, w_query, sub_keys, t_sc):
    nb, s, _ = x.shape
    t = nb * s
    h, e_t, g_t = _peer_route(x, mod, gamma, w_query, sub_keys)
    eid = e_t.transpose(0, 3, 1, 2).reshape(t, PEER_PICKS)
    gate = g_t.transpose(0, 3, 1, 2).reshape(t, PEER_PICKS)
    hf = h.reshape(t, D_MODEL)
    xf = x.reshape(t, D_MODEL)
    gate_tc = gate[t_sc:]
    gate_tc = jnp.stack([gate_tc, jnp.zeros_like(gate_tc)], axis=-1).reshape(t - t_sc, 2 * PEER_PICKS)
    return dict(sc=(eid[:t_sc], gate[:t_sc], hf[:t_sc]), x_sc=xf[:t_sc],
                tc=(eid[t_sc:], hf[t_sc:], gate_tc), x_tc=xf[t_sc:])


def _peer_retrieve_tc(ops, shape, mod, table, final_g, final_norm, t_sc):
    return _peer_gather(*ops["tc"], table, ops["x_tc"], mod, final_g, shape[1], final_norm, t_sc)


def _peer_retrieve_finish(ops, po, y_tc, shape, mod, final_g, final_norm):
    nb, s, _ = shape
    y_sc = _peer_finish(ops["x_sc"], po, mod, final_g, s, final_norm)
    return jnp.concatenate([y_sc, y_tc], axis=0).reshape(nb, s, D_MODEL)


def kernel(x_prompt, x_sample, c_prompt, c_sample, ln_mix_g, ln_ffn_g, w_mod, b_mod, rec_w_in, rec_conv_w,
           rec_conv_b, rec_ga_w, rec_ga_b, rec_gx_w, rec_gx_b, rec_lam, rec_w_out, att_w_qkv, att_q_g,
           att_k_g, att_w_o, peer_w_query, peer_sub_keys, peer_u, peer_v, final_g):
    tables = [_pack_table(peer_u[i], peer_v[i]) for i in range(DEPTH)]
    tables_sc = [tb.reshape(N_EXPERTS, D_MODEL) for tb in tables]
    fg = final_g.reshape(1, D_MODEL)

    xs = [x_prompt, x_sample]
    cs = [c_prompt, c_sample]
    for i in range(DEPTH):
        gm = ln_mix_g[i].reshape(1, D_MODEL)
        gf = ln_ffn_g[i].reshape(1, D_MODEL)
        j = i // 2
        last = i == DEPTH - 1
        for tr in TRUNK_ORDER[i]:
            x = xs[tr]
            mod = _mod(cs[tr], w_mod[i], b_mod[i])
            if i % 2 == 0:
                x = _rglru_layer(x, mod, gm, rec_w_in[j], rec_conv_w[j], rec_conv_b[j], rec_ga_w[j],
                                 rec_ga_b[j], rec_gx_w[j], rec_gx_b[j], rec_lam[j], rec_w_out[j])
            else:
                x = _attention_layer(x, mod, gm, att_w_qkv[j], att_q_g[j], att_k_g[j], att_w_o[j])
            t_sc = SC_TOKENS[i][tr]
            routed = _peer_route_phase(x, mod, gf, peer_w_query[i], peer_sub_keys[i], t_sc)
            y_tc = _peer_retrieve_tc(routed, x.shape, mod, tables[i], fg, last, t_sc)
            po = _peer_sc(tables_sc[i], *routed["sc"])
            xs[tr] = _peer_retrieve_finish(routed, po, y_tc, x.shape, mod, fg, last)
    return (xs[0], xs[1])
```

```python
import dataclasses
import functools

import jax
import jax.numpy as jnp
from jax import lax
from jax.experimental import pallas as pl
from jax.experimental.pallas import tpu as pltpu
from jax.experimental.pallas import tpu_sc as plsc

F32 = jnp.float32
BF16 = jnp.bfloat16
I32 = jnp.int32

D_MODEL = 1024
DEPTH = 2
GRID_W = 64
EPS = 1e-6
RNN_WIDTH = D_MODEL
RNN_BLOCKS = 16
RNN_BLOCK_W = RNN_WIDTH // RNN_BLOCKS
CONV_W = 4
LRU_C = 8.0
N_HEADS = 16
N_KV_HEADS = 4
HEAD_DIM = D_MODEL // N_HEADS
GROUP = N_HEADS // N_KV_HEADS
AXIS_DIM = HEAD_DIM // 2
ROPE_THETA = 10000.0
N_KEYS = 128
N_EXPERTS = N_KEYS * N_KEYS
PEER_HEADS = 8
PEER_TOPK = 16
PEER_QDIM = 256
PEER_HALF = PEER_QDIM // 2
PEER_PICKS = PEER_HEADS * PEER_TOPK

LANES = 128
SUBLANES = 8
VMEM_LIMIT = 48 * 1024 * 1024

GATHER_PITCH = 12
GATHER_TOKENS = 64
GATHER_SLOTS = 4
ROWS_PER_EXPERT = D_MODEL // LANES


def _cparams(*sem):
    return pltpu.CompilerParams(dimension_semantics=sem, vmem_limit_bytes=VMEM_LIMIT)


def _gelu(x):
    return jax.nn.gelu(x)


def _norm_mod(x, gamma, scale, shift):
    ms = jnp.mean(x * x, axis=-1, keepdims=True)
    y = x * lax.rsqrt(ms + EPS) * gamma
    return y * (1.0 + scale) + shift


def _mod_body(c_ref, w_ref, b_ref, o_ref):
    c = c_ref[...]
    s = c * jax.nn.sigmoid(c)
    o_ref[...] = jnp.dot(s, w_ref[...], preferred_element_type=F32) + b_ref[...]


def _mod(c, w, b):
    nb, n = c.shape[0], w.shape[1]
    tn = 1536
    out = pl.pallas_call(
        _mod_body,
        grid=(n // tn,),
        in_specs=[
            pl.BlockSpec((nb, D_MODEL), lambda j: (0, 0)),
            pl.BlockSpec((D_MODEL, tn), lambda j: (0, j)),
            pl.BlockSpec((1, tn), lambda j: (0, j)),
        ],
        out_specs=pl.BlockSpec((nb, tn), lambda j: (0, j)),
        out_shape=jax.ShapeDtypeStruct((nb, n), F32),
        compiler_params=_cparams("arbitrary"),
        name="adaln_mod",
    )(c, w, b.reshape(1, n))
    return out.reshape(nb, 6, D_MODEL)


def _nmm_body(x_ref, mod_ref, g_ref, w_ref, o_ref, *, sh, sc):
    h = _norm_mod(x_ref[0], g_ref[...], mod_ref[0, sc:sc + 1, :], mod_ref[0, sh:sh + 1, :])
    o_ref[0] = jnp.dot(h.astype(BF16), w_ref[...], preferred_element_type=F32)


def _norm_mod_matmul(x, mod, gamma, w_bf16, sh, sc, tm=512):
    nb, s, _ = x.shape
    n = w_bf16.shape[1]
    return pl.pallas_call(
        functools.partial(_nmm_body, sh=sh, sc=sc),
        grid=(nb, s // tm),
        in_specs=[
            pl.BlockSpec((1, tm, D_MODEL), lambda b, i: (b, i, 0)),
            pl.BlockSpec((1, 6, D_MODEL), lambda b, i: (b, 0, 0)),
            pl.BlockSpec((1, D_MODEL), lambda b, i: (0, 0)),
            pl.BlockSpec((D_MODEL, n), lambda b, i: (0, 0)),
        ],
        out_specs=pl.BlockSpec((1, tm, n), lambda b, i: (b, i, 0)),
        out_shape=jax.ShapeDtypeStruct((nb, s, n), F32),
        compiler_params=_cparams("arbitrary", "arbitrary"),
        name="norm_mod_matmul",
    )(x, mod, gamma, w_bf16)


def _proj_res_body(m_ref, w_ref, x_ref, mod_ref, o_ref, *, gi):
    y = jnp.dot(m_ref[0].astype(BF16), w_ref[...], preferred_element_type=F32)
    o_ref[0] = x_ref[0] + mod_ref[0, gi:gi + 1, :] * y


def _proj_residual(m, w_bf16, x, mod, gi, tm=512):
    nb, s, k = m.shape
    return pl.pallas_call(
        functools.partial(_proj_res_body, gi=gi),
        grid=(nb, s // tm),
        in_specs=[
            pl.BlockSpec((1, tm, k), lambda b, i: (b, i, 0)),
            pl.BlockSpec((k, D_MODEL), lambda b, i: (0, 0)),
            pl.BlockSpec((1, tm, D_MODEL), lambda b, i: (b, i, 0)),
            pl.BlockSpec((1, 6, D_MODEL), lambda b, i: (b, 0, 0)),
        ],
        out_specs=pl.BlockSpec((1, tm, D_MODEL), lambda b, i: (b, i, 0)),
        out_shape=jax.ShapeDtypeStruct((nb, s, D_MODEL), F32),
        compiler_params=_cparams("arbitrary", "arbitrary"),
        name="proj_residual",
    )(m, w_bf16, x, mod)


def _scan_tile(a, b, carry, reverse):
    tt = a.shape[0]
    row = lax.broadcasted_iota(I32, (tt, LANES), 0) % SUBLANES
    for d in (1, 2, 4):
        if reverse:
            shift, keep = tt - d, row < SUBLANES - d
        else:
            shift, keep = d, row >= d
        ap = pltpu.roll(a, shift, 0)
        bp = pltpu.roll(b, shift, 0)
        b = jnp.where(keep, a * bp + b, b)
        a = jnp.where(keep, a * ap, a)
    groups = tt // SUBLANES
    hs = [None] * groups
    order = range(groups - 1, -1, -1) if reverse else range(groups)
    for g in order:
        lo = g * SUBLANES
        h = a[lo:lo + SUBLANES] * carry + b[lo:lo + SUBLANES]
        hs[g] = h
        last = h[0:1] if reverse else h[SUBLANES - 1:SUBLANES]
        carry = jnp.broadcast_to(last, (SUBLANES, LANES))
    return jnp.concatenate(hs, axis=0), carry


def _rglru_body(y_ref, x_ref, cw_ref, cb_ref, gaw_ref, gab_ref, gxw_ref, gxb_ref, nc_ref,
                o_ref, xp, hf, *, s, tt):
    nt = s // tt
    pad = SUBLANES
    zeros = jnp.zeros((pad, LANES), F32)
    xp[0:pad, :] = zeros
    xp[s + pad:s + 2 * pad, :] = zeros

    def copy_tile(i, c):
        r0 = pl.multiple_of(i * tt, tt)
        xp[pl.ds(pl.multiple_of(r0 + pad, SUBLANES), tt), :] = x_ref[0, pl.ds(r0, tt), :]
        return c

    lax.fori_loop(0, nt, copy_tile, 0)

    cw = cw_ref[...]
    cb = cb_ref[...]

    def conv_tile(r0):
        win = xp[pl.ds(r0, tt + 2 * pad), :]
        acc = cb + cw[2:3] * win[pad:pad + tt]
        for k in (0, 1, 3):
            shifted = pltpu.roll(win, (2 - k) % (tt + 2 * pad), 0)
            acc = acc + cw[k:k + 1] * shifted[pad:pad + tt]
        return acc

    def gates(d, xc):
        xb = xc.astype(BF16)
        r = jax.nn.sigmoid(jnp.dot(xb, gaw_ref[d, 0], preferred_element_type=F32) + gab_ref[d])
        ig = jax.nn.sigmoid(jnp.dot(xb, gxw_ref[d, 0], preferred_element_type=F32) + gxb_ref[d])
        a = jnp.exp(nc_ref[d] * r)
        b = jnp.sqrt(1.0 - a * a) * (ig * xc)
        return a, b

    carry0 = jnp.zeros((SUBLANES, LANES), F32)

    def fwd(i, carry):
        r0 = pl.multiple_of(i * tt, tt)
        a, b = gates(0, conv_tile(r0))
        h, carry = _scan_tile(a, b, carry, False)
        hf[pl.ds(r0, tt), :] = h
        return carry

    lax.fori_loop(0, nt, fwd, carry0)

    def bwd(ii, carry):
        r0 = pl.multiple_of((nt - 1 - ii) * tt, tt)
        a, b = gates(1, conv_tile(r0))
        h, carry = _scan_tile(a, b, carry, True)
        o_ref[0, pl.ds(r0, tt), :] = (hf[pl.ds(r0, tt), :] + h) * _gelu(y_ref[0, pl.ds(r0, tt), :])
        return carry

    lax.fori_loop(0, nt, bwd, carry0)


def _rglru_core(u, conv_w, conv_b, gaw, gab, gxw, gxb, negc, tt=256):
    nb, s, _ = u.shape
    ng = RNN_WIDTH // LANES
    return pl.pallas_call(
        functools.partial(_rglru_body, s=s, tt=tt),
        grid=(nb, ng),
        in_specs=[
            pl.BlockSpec((1, s, LANES), lambda b, j: (b, 0, j)),
            pl.BlockSpec((1, s, LANES), lambda b, j: (b, 0, ng + j)),
            pl.BlockSpec((CONV_W, LANES), lambda b, j: (0, j)),
            pl.BlockSpec((1, LANES), lambda b, j: (0, j)),
            pl.BlockSpec((2, 1, LANES, LANES), lambda b, j: (0, j, 0, 0)),
            pl.BlockSpec((2, 1, LANES), lambda b, j: (0, 0, j)),
            pl.BlockSpec((2, 1, LANES, LANES), lambda b, j: (0, j, 0, 0)),
            pl.BlockSpec((2, 1, LANES), lambda b, j: (0, 0, j)),
            pl.BlockSpec((2, 1, LANES), lambda b, j: (0, 0, j)),
        ],
        out_specs=pl.BlockSpec((1, s, LANES), lambda b, j: (b, 0, j)),
        out_shape=jax.ShapeDtypeStruct((nb, s, RNN_WIDTH), F32),
        scratch_shapes=[pltpu.VMEM((s + 2 * SUBLANES, LANES), F32), pltpu.VMEM((s, LANES), F32)],
        compiler_params=_cparams("arbitrary", "arbitrary"),
        name="rglru_core",
    )(u, u, conv_w, conv_b, gaw, gab, gxw, gxb, negc)


def _block_diag_groups(w):
    ng = RNN_WIDTH // LANES
    w = w.reshape(ng, 2, RNN_BLOCK_W, RNN_BLOCK_W)
    out = jnp.zeros((ng, LANES, LANES), w.dtype)
    out = out.at[:, :RNN_BLOCK_W, :RNN_BLOCK_W].set(w[:, 0])
    out = out.at[:, RNN_BLOCK_W:, RNN_BLOCK_W:].set(w[:, 1])
    return out


def _rglru_layer(x, mod, gamma, w_in, conv_w, conv_b, ga_w, ga_b, gx_w, gx_b, lam, w_out):
    u = _norm_mod_matmul(x, mod, gamma, w_in.astype(BF16), sh=0, sc=1)
    gaw = jnp.stack([_block_diag_groups(ga_w[d]) for d in range(2)]).astype(BF16)
    gxw = jnp.stack([_block_diag_groups(gx_w[d]) for d in range(2)]).astype(BF16)
    gab = ga_b.reshape(2, 1, RNN_WIDTH)
    gxb = gx_b.reshape(2, 1, RNN_WIDTH)
    negc = (-LRU_C * jax.nn.softplus(-lam)).reshape(2, 1, RNN_WIDTH)
    m = _rglru_core(u, conv_w, conv_b.reshape(1, RNN_WIDTH), gaw, gab, gxw, gxb, negc)
    return _proj_residual(m, w_out.astype(BF16), x, mod, gi=2)


def _seg_mean(x2, seg_ref):
    hi = x2.astype(BF16)
    lo = (x2 - hi.astype(F32)).astype(BF16)
    return (jnp.dot(hi, seg_ref[...], preferred_element_type=F32)
            + jnp.dot(lo, seg_ref[...], preferred_element_type=F32))


def _rope(x, cos, sin, lane_lo):
    outs = []
    for j in range(x.shape[1] // LANES):
        xt = x[:, j * LANES:(j + 1) * LANES]
        rot = jnp.where(lane_lo, pltpu.roll(xt, LANES - AXIS_DIM // 2, 1), pltpu.roll(xt, AXIS_DIM // 2, 1))
        outs.append(xt * cos + rot * sin)
    return jnp.concatenate(outs, axis=1) if len(outs) > 1 else outs[0]


def _qkv_body(x_ref, mod_ref, g_ref, w_ref, segq_ref, segk_ref, qg_ref, kg_ref, cosq_ref, sinq_ref,
              cosk_ref, sink_ref, q_ref, kt_ref, v_ref, *, tm):
    h = _norm_mod(x_ref[0], g_ref[...], mod_ref[0, 1:2, :], mod_ref[0, 0:1, :])
    qkv = jnp.dot(h.astype(BF16), w_ref[...], preferred_element_type=F32)
    nq = N_HEADS * HEAD_DIM
    nk = N_KV_HEADS * HEAD_DIM
    q = qkv[:, :nq]
    k = qkv[:, nq:nq + nk]
    v = qkv[:, nq + nk:]
    lane = lax.broadcasted_iota(I32, (tm, LANES), 1)
    lane_lo = (lane % AXIS_DIM) < (AXIS_DIM // 2)
    q = q * lax.rsqrt(_seg_mean(q * q, segq_ref) + EPS) * qg_ref[...]
    k = k * lax.rsqrt(_seg_mean(k * k, segk_ref) + EPS) * kg_ref[...]
    q = _rope(q, cosq_ref[...], sinq_ref[...], lane_lo)
    k = _rope(k, cosk_ref[...], sink_ref[...], lane_lo)
    q_ref[0] = q.astype(BF16)
    kt = k.T.astype(BF16)
    for g in range(N_KV_HEADS):
        kt_ref[0, g] = kt[g * HEAD_DIM:(g + 1) * HEAD_DIM, :]
        v_ref[0, g] = v[:, g * HEAD_DIM:(g + 1) * HEAD_DIM].astype(BF16)


def _rope_tables(s):
    rows = s // GRID_W
    row = jnp.repeat(jnp.arange(rows, dtype=F32), GRID_W)
    col = jnp.tile(jnp.arange(GRID_W, dtype=F32), rows)
    inv = ROPE_THETA ** (-jnp.arange(0, AXIS_DIM, 2, dtype=F32) / AXIS_DIM)
    ar = row[:, None] * inv
    ac = col[:, None] * inv
    cos = jnp.concatenate([jnp.cos(ar), jnp.cos(ar), jnp.cos(ac), jnp.cos(ac)], axis=1)
    sin = jnp.concatenate([-jnp.sin(ar), jnp.sin(ar), -jnp.sin(ac), jnp.sin(ac)], axis=1)
    return jnp.tile(cos, (1, LANES // HEAD_DIM)), jnp.tile(sin, (1, LANES // HEAD_DIM))


def _attn_body(q_ref, kt_ref, v_ref, o_ref):
    kt = kt_ref[0, 0]
    v = v_ref[0, 0]
    outs = []
    for hh in range(GROUP):
        qh = q_ref[0, :, hh * HEAD_DIM:(hh + 1) * HEAD_DIM]
        sc = jnp.dot(qh, kt, preferred_element_type=F32)
        m = jnp.max(sc, axis=-1, keepdims=True)
        p = jnp.exp(sc - m)
        l = jnp.sum(p, axis=-1, keepdims=True)
        o = jnp.dot(p.astype(BF16), v, preferred_element_type=F32)
        outs.append(o / l)
    o_ref[0] = jnp.concatenate(outs, axis=1).astype(BF16)


def _attention_layer(x, mod, gamma, w_qkv, q_g, k_g, w_o, tm=256, tq=128):
    nb, s, _ = x.shape
    nq = N_HEADS * HEAD_DIM
    nk = N_KV_HEADS * HEAD_DIM
    seg = jnp.kron(jnp.eye(N_HEADS, dtype=F32), jnp.full((HEAD_DIM, HEAD_DIM), 1.0 / HEAD_DIM, F32)).astype(BF16)
    segk = seg[:nk, :nk]
    cos, sin = _rope_tables(s)
    scale = HEAD_DIM ** -0.5
    q, kt, v = pl.pallas_call(
        functools.partial(_qkv_body, tm=tm),
        grid=(nb, s // tm),
        in_specs=[
            pl.BlockSpec((1, tm, D_MODEL), lambda b, i: (b, i, 0)),
            pl.BlockSpec((1, 6, D_MODEL), lambda b, i: (b, 0, 0)),
            pl.BlockSpec((1, D_MODEL), lambda b, i: (0, 0)),
            pl.BlockSpec((D_MODEL, nq + 2 * nk), lambda b, i: (0, 0)),
            pl.BlockSpec((nq, nq), lambda b, i: (0, 0)),
            pl.BlockSpec((nk, nk), lambda b, i: (0, 0)),
            pl.BlockSpec((1, nq), lambda b, i: (0, 0)),
            pl.BlockSpec((1, nk), lambda b, i: (0, 0)),
            pl.BlockSpec((tm, LANES), lambda b, i: (i, 0)),
            pl.BlockSpec((tm, LANES), lambda b, i: (i, 0)),
            pl.BlockSpec((tm, LANES), lambda b, i: (i, 0)),
            pl.BlockSpec((tm, LANES), lambda b, i: (i, 0)),
        ],
        out_specs=[
            pl.BlockSpec((1, tm, nq), lambda b, i: (b, i, 0)),
            pl.BlockSpec((1, N_KV_HEADS, HEAD_DIM, tm), lambda b, i: (b, 0, 0, i)),
            pl.BlockSpec((1, N_KV_HEADS, tm, HEAD_DIM), lambda b, i: (b, 0, i, 0)),
        ],
        out_shape=[
            jax.ShapeDtypeStruct((nb, s, nq), BF16),
            jax.ShapeDtypeStruct((nb, N_KV_HEADS, HEAD_DIM, s), BF16),
            jax.ShapeDtypeStruct((nb, N_KV_HEADS, s, HEAD_DIM), BF16),
        ],
        compiler_params=_cparams("arbitrary", "arbitrary"),
        name="qkv_rope",
    )(x, mod, gamma, w_qkv.astype(BF16), seg, segk,
      jnp.tile(q_g, N_HEADS).reshape(1, nq), jnp.tile(k_g, N_KV_HEADS).reshape(1, nk),
      cos * scale, sin * scale, cos, sin)

    gw = GROUP * HEAD_DIM
    o = pl.pallas_call(
        _attn_body,
        grid=(nb, N_KV_HEADS, s // tq),
        in_specs=[
            pl.BlockSpec((1, tq, gw), lambda b, g, i: (b, i, g)),
            pl.BlockSpec((1, 1, HEAD_DIM, s), lambda b, g, i: (b, g, 0, 0)),
            pl.BlockSpec((1, 1, s, HEAD_DIM), lambda b, g, i: (b, g, 0, 0)),
        ],
        out_specs=pl.BlockSpec((1, tq, gw), lambda b, g, i: (b, i, g)),
        out_shape=jax.ShapeDtypeStruct((nb, s, nq), BF16),
        compiler_params=_cparams("arbitrary", "arbitrary", "arbitrary"),
        name="attention",
    )(q, kt, v)
    return _proj_residual(o, w_o.astype(BF16), x, mod, gi=2)


def _topk_rows(s, ids, k, id_bound):
    vals, picks = [], []
    for _ in range(k):
        m = jnp.max(s, axis=0, keepdims=True)
        i = jnp.min(jnp.where(s == m, ids, id_bound), axis=0, keepdims=True)
        vals.append(m)
        picks.append(i)
        s = jnp.where(ids == i, -jnp.inf, s)
    return jnp.concatenate(vals, axis=0), jnp.concatenate(picks, axis=0)


ROUTE_HEADS_PER_STEP = 8


def _route_body(x_ref, mod_ref, g_ref, wq_ref, keys_ref, h_ref, e_ref, gate_ref, hb, *, tm):
    hd = pl.program_id(2)

    @pl.when(hd == 0)
    def _():
        h = _norm_mod(x_ref[0], g_ref[...], mod_ref[0, 4:5, :], mod_ref[0, 3:4, :])
        h_ref[0] = h
        hb[...] = h.astype(BF16)

    q = jnp.dot(hb[...], wq_ref[...], preferred_element_type=F32)
    for hh in range(ROUTE_HEADS_PER_STEP):
        gate, eid = _route_head(q[:, hh * PEER_QDIM:(hh + 1) * PEER_QDIM], keys_ref, tm)
        gate_ref[0, hh] = gate
        e_ref[0, hh] = eid


def _route_head(q, keys_ref, tm):
    key_ids = lax.broadcasted_iota(I32, (N_KEYS, tm), 0)
    tops = []
    for p in range(2):
        qp = q[:, p * PEER_HALF:(p + 1) * PEER_HALF].astype(BF16)
        st = lax.dot_general(keys_ref[p], qp, (((1,), (1,)), ((), ())), preferred_element_type=F32)
        tops.append(_topk_rows(st, key_ids, PEER_TOPK, N_KEYS))
    (v0, i0), (v1, i1) = tops

    r16 = lax.broadcasted_iota(I32, (PEER_TOPK, tm), 0)
    cand, fid, eid = [], [], []
    for a in range(4):
        cand.append(v0[a:a + 1] + v1)
        fid.append(a * PEER_TOPK + r16)
        eid.append(i0[a:a + 1] * N_KEYS + i1)
    for b in range(3):
        cand.append(jnp.where(r16 >= 4, v0 + v1[b:b + 1], -jnp.inf))
        fid.append(r16 * PEER_TOPK + b)
        eid.append(i0 * N_KEYS + i1[b:b + 1])
    cand = jnp.concatenate(cand, axis=0)
    fid = jnp.concatenate(fid, axis=0)
    eid = jnp.concatenate(eid, axis=0)

    best, chosen = [], []
    for _ in range(PEER_TOPK):
        m = jnp.max(cand, axis=0, keepdims=True)
        f = jnp.min(jnp.where(cand == m, fid, PEER_TOPK * PEER_TOPK), axis=0, keepdims=True)
        hit = fid == f
        best.append(m)
        chosen.append(jnp.max(jnp.where(hit, eid, -1), axis=0, keepdims=True))
        cand = jnp.where(hit, -jnp.inf, cand)
    best = jnp.concatenate(best, axis=0)
    ex = jnp.exp(best - best[0:1])
    return ex / jnp.sum(ex, axis=0, keepdims=True), jnp.concatenate(chosen, axis=0)


def _peer_route(x, mod, gamma, w_query, sub_keys, tm=128):
    nb, s, _ = x.shape
    hps = ROUTE_HEADS_PER_STEP
    return pl.pallas_call(
        functools.partial(_route_body, tm=tm),
        grid=(nb, s // tm, PEER_HEADS // hps),
        in_specs=[
            pl.BlockSpec((1, tm, D_MODEL), lambda b, i, h: (b, i, 0)),
            pl.BlockSpec((1, 6, D_MODEL), lambda b, i, h: (b, 0, 0)),
            pl.BlockSpec((1, D_MODEL), lambda b, i, h: (0, 0)),
            pl.BlockSpec((D_MODEL, hps * PEER_QDIM), lambda b, i, h: (0, h)),
            pl.BlockSpec((2, N_KEYS, PEER_HALF), lambda b, i, h: (0, 0, 0)),
        ],
        out_specs=[
            pl.BlockSpec((1, tm, D_MODEL), lambda b, i, h: (b, i, 0)),
            pl.BlockSpec((1, hps, PEER_TOPK, tm), lambda b, i, h: (b, h, 0, i)),
            pl.BlockSpec((1, hps, PEER_TOPK, tm), lambda b, i, h: (b, h, 0, i)),
        ],
        out_shape=[
            jax.ShapeDtypeStruct((nb, s, D_MODEL), F32),
            jax.ShapeDtypeStruct((nb, PEER_HEADS, PEER_TOPK, s), I32),
            jax.ShapeDtypeStruct((nb, PEER_HEADS, PEER_TOPK, s), F32),
        ],
        scratch_shapes=[pltpu.VMEM((tm, D_MODEL), BF16)],
        compiler_params=_cparams("arbitrary", "arbitrary", "arbitrary"),
        name="peer_route",
    )(x, mod, gamma, w_query.astype(BF16), sub_keys.astype(BF16))


def _pack_table(u, v):
    ne = u.shape[0]
    pairs = jnp.stack([u.astype(BF16).reshape(ne * ROWS_PER_EXPERT, LANES),
                       v.astype(BF16).reshape(ne * ROWS_PER_EXPERT, LANES)], axis=-1)
    return lax.bitcast_convert_type(pairs, I32)


def _gather_body(eid_ref, h_ref, g_ref, tab_ref, x_ref, mod_ref, fg_ref, o_ref, buf, sem, po,
                 *, tb, final_norm):
    ne = PEER_PICKS
    rows = ROWS_PER_EXPERT

    ns = GATHER_SLOTS
    ahead = ns - 1

    def issue(t, slot, lo, hi):
        for k in range(lo, hi):
            src = tab_ref.at[pl.ds(pl.multiple_of(eid_ref[t, k] * rows, rows), rows), :]
            dst = buf.at[slot, pl.ds(GATHER_PITCH * k, rows), :]
            pltpu.make_async_copy(src, dst, sem.at[slot]).start(priority=k % 2)

    def wait(slot):
        pltpu.make_async_copy(tab_ref.at[pl.ds(0, ne * rows), :], buf.at[slot, pl.ds(0, ne * rows), :],
                              sem.at[slot]).wait()

    for q in range(ahead):
        issue(q, q, 0, ne)
    early = 4
    halves = 2
    per_half = ne // halves
    even = lax.broadcasted_iota(I32, (1, 2 * LANES), 1) % 2 == 0
    row_id = lax.broadcasted_iota(I32, (SUBLANES, LANES), 0)

    def chunk(slot, c, first=0, count=ne):
        words = buf[slot, pl.ds(first * GATHER_PITCH + c, count, stride=GATHER_PITCH), :]
        return pltpu.bitcast(words, BF16)

    def score_half(t8, j, hf, c, acc):
        xrow = h_ref[pl.ds(t8, SUBLANES), c * LANES:(c + 1) * LANES][j:j + 1]
        return acc + chunk(j % ns, c, hf * per_half, per_half).astype(F32) * xrow

    def lane_sums(acc):
        return jnp.sum(acc.T, axis=0, keepdims=True)

    def coefficients(parts, t8, j):
        s = jnp.concatenate(parts, axis=1)
        g = g_ref[pl.ds(t8, SUBLANES), :][j:j + 1]
        coef = pltpu.roll(jnp.where(even, _gelu(s) * g, 0.0), 1, 1)
        return jnp.broadcast_to(coef, (SUBLANES, 2 * LANES))

    wait(0)
    parts = []
    for hf in range(halves):
        a = jnp.zeros((2 * per_half, LANES), F32)
        for c in range(rows):
            a = score_half(0, 0, hf, c, a)
        parts.append(lane_sums(a))
    coef0 = coefficients(parts, 0, 0)

    def body(it, coef):
        t8 = pl.multiple_of(it * SUBLANES, SUBLANES)
        t8_next = pl.multiple_of(jnp.minimum(t8 + SUBLANES, tb - SUBLANES), SUBLANES)
        acc = [None] * rows
        for j in range(SUBLANES):
            jn = (j + 1) % SUBLANES
            t8n = t8 if j + 1 < SUBLANES else t8_next
            nxt = jnp.minimum(t8 + j + ahead, tb - 1)
            nslot = (j + ahead) % ns
            wait((j + 1) % ns)
            coef_b = coef.astype(BF16)
            step = 0
            parts = []
            for hf in range(halves):
                a = jnp.zeros((2 * per_half, LANES), F32)
                for c in range(rows):
                    issue(nxt, nslot, step * early, (step + 1) * early)
                    step += 1
                    a = score_half(t8n, jn, hf, c, a)
                parts.append(lane_sums(a))
            for c in range(rows):
                issue(nxt, nslot, step * early, (step + 1) * early)
                step += 1
                o = jnp.dot(coef_b, chunk(j % ns, c), preferred_element_type=F32)
                acc[c] = o if j == 0 else jnp.where(row_id == j, o, acc[c])
            issue(nxt, nslot, step * early, ne)
            coef = coefficients(parts, t8n, jn)
        for c in range(rows):
            po[pl.ds(t8, SUBLANES), c * LANES:(c + 1) * LANES] = acc[c]
        return coef

    lax.fori_loop(0, tb // SUBLANES, body, coef0)
    for q in range(1, ahead):
        wait((tb + q) % ns)
    y = x_ref[...] + mod_ref[0, 5:6, :] * po[...]
    if final_norm:
        y = y * lax.rsqrt(jnp.mean(y * y, axis=-1, keepdims=True) + EPS) * fg_ref[...]
    o_ref[...] = y


def _peer_gather(eid, h, g, table, x, mod, final_g, seq_len, final_norm, tok_offset, tb=GATHER_TOKENS):
    t = h.shape[0]
    return pl.pallas_call(
        functools.partial(_gather_body, tb=tb, final_norm=final_norm),
        grid=(t // tb,),
        in_specs=[
            pl.BlockSpec((tb, PEER_PICKS), lambda i: (i, 0), memory_space=pltpu.SMEM),
            pl.BlockSpec((tb, D_MODEL), lambda i: (i, 0)),
            pl.BlockSpec((tb, 2 * PEER_PICKS), lambda i: (i, 0)),
            pl.BlockSpec(memory_space=pl.ANY),
            pl.BlockSpec((tb, D_MODEL), lambda i: (i, 0)),
            pl.BlockSpec((1, 6, D_MODEL), lambda i: ((tok_offset + i * tb) // seq_len, 0, 0)),
            pl.BlockSpec((1, D_MODEL), lambda i: (0, 0)),
        ],
        out_specs=pl.BlockSpec((tb, D_MODEL), lambda i: (i, 0)),
        out_shape=jax.ShapeDtypeStruct((t, D_MODEL), F32),
        scratch_shapes=[
            pltpu.VMEM((GATHER_SLOTS, PEER_PICKS * GATHER_PITCH, LANES), I32),
            pltpu.SemaphoreType.DMA((GATHER_SLOTS,)),
            pltpu.VMEM((tb, D_MODEL), F32),
        ],
        compiler_params=_cparams("arbitrary"),
        name="peer_gather",
    )(eid, h, g, table, x, mod, final_g)


SC_LANES = 16
SC_WORKERS = 32
SC_GROUP = 8
SC_TOKENS = ((8192, 14336), (9216, 12288))
SC_COST_SCALE = 1
TRUNK_ORDER = ((0, 1), (0, 1))
SC_CHUNKS = D_MODEL // SC_LANES


def _sc_body(tab_hbm, eid_hbm, g_hbm, h_hbm, out_hbm, idx_v, g_v, x_v, o_v, rows0, rows1, rows2, rows3,
             sem0, sem1, sem2, sem3, *, tpw):
    nl = SC_LANES
    wid = lax.axis_index("s") * 2 + lax.axis_index("c")
    lane = lax.iota(I32, nl)

    def permute(x, idx):
        return jnp.take_along_axis(x, idx, axis=0, mode="promise_in_bounds")
    rows = (rows0, rows1, rows2, rows3)
    sems = (sem0, sem1, sem2, sem3)
    ns = len(rows)
    units = SC_GROUP * PEER_HEADS

    def gather(tok, hd, slot):
        return pltpu.make_async_copy(tab_hbm.at[idx_v.at[tok, hd]], rows[slot], sems[slot])

    def compute(tok, hd, slot, t):
        rv = rows[slot]

        @pl.when(hd == 0)
        def _():
            def zero(c, cc):
                o_v[pl.ds(c * nl, nl)] = jnp.zeros((nl,), F32)
                return cc
            lax.fori_loop(0, SC_CHUNKS, zero, 0)

        def score(c, accs):
            x = x_v[tok, pl.ds(c * nl, nl)]
            out = []
            for kk in range(PEER_TOPK):
                u = plsc.bitcast(lax.shift_left(rv[kk, pl.ds(c * nl, nl)], 16), F32)
                out.append(accs[kk] + u * x)
            return tuple(out)

        accs = lax.fori_loop(0, SC_CHUNKS, score, tuple(jnp.zeros((nl,), F32) for _ in range(PEER_TOPK)))
        vecs = list(accs)
        d = 1
        while len(vecs) > 1:
            partner = lane ^ d
            take_lo = (lane & d) == 0
            nxt_vecs = []
            for i in range(0, len(vecs), 2):
                a = vecs[i] + permute(vecs[i], partner)
                b = vecs[i + 1] + permute(vecs[i + 1], partner)
                nxt_vecs.append(jnp.where(take_lo, a, b))
            vecs = nxt_vecs
            d *= 2
        s = vecs[0]
        z = 0.7978845608028654 * (s + 0.044715 * s * s * s)
        act = s * (1.0 - 1.0 / (jnp.exp(2.0 * z) + 1.0))
        coef = act * g_v[tok, hd, :]
        coefs = [permute(coef, jnp.full((nl,), kk, I32)) for kk in range(PEER_TOPK)]

        def combine(c, cc):
            parts = []
            for q in range(4):
                acc = None
                for kk in range(q * PEER_TOPK // 4, (q + 1) * PEER_TOPK // 4):
                    v = plsc.bitcast(lax.bitwise_and(rv[kk, pl.ds(c * nl, nl)], jnp.int32(-65536)), F32)
                    term = coefs[kk] * v
                    acc = term if acc is None else acc + term
                parts.append(acc)
            o_v[pl.ds(c * nl, nl)] = o_v[pl.ds(c * nl, nl)] + ((parts[0] + parts[1]) + (parts[2] + parts[3]))
            return cc

        lax.fori_loop(0, SC_CHUNKS, combine, 0)

        @pl.when(hd == PEER_HEADS - 1)
        def _():
            pltpu.sync_copy(o_v, out_hbm.at[t])

    def group(gi, carry):
        t0 = wid * tpw + gi * SC_GROUP
        pltpu.sync_copy(eid_hbm.at[pl.ds(t0, SC_GROUP)], idx_v)
        pltpu.sync_copy(g_hbm.at[pl.ds(t0, SC_GROUP)], g_v)
        pltpu.sync_copy(h_hbm.at[pl.ds(t0, SC_GROUP)], x_v)
        for q in range(ns - 1):
            gather(0, q, q).start()

        def ring(ri, cc):
            j0 = ri * ns
            tok, hd0 = j0 // PEER_HEADS, j0 % PEER_HEADS
            for q in range(ns):
                jn = j0 + q + ns - 1

                @pl.when(jn < units)
                def _():
                    gather(jn // PEER_HEADS, jn % PEER_HEADS, (q + ns - 1) % ns).start()

                gather(tok, hd0 + q, q).wait()
                compute(tok, hd0 + q, q, t0 + tok)
            return cc

        lax.fori_loop(0, units // ns, ring, 0)
        return carry

    lax.fori_loop(0, tpw // SC_GROUP, group, 0)


def _peer_sc(table, eid, gate, h):
    t = h.shape[0]
    tpw = t // SC_WORKERS
    cp = pltpu.CompilerParams()
    if "needs_layout_passes" in pltpu.CompilerParams.__dataclass_fields__:
        cp = dataclasses.replace(cp, needs_layout_passes=False)
    run = pl.kernel(
        functools.partial(_sc_body, tpw=tpw),
        out_type=jax.ShapeDtypeStruct((t, D_MODEL), F32),
        mesh=plsc.VectorSubcoreMesh(core_axis_name="c", subcore_axis_name="s"),
        scratch_types=[
            pltpu.VMEM((SC_GROUP, PEER_HEADS, PEER_TOPK), I32),
            pltpu.VMEM((SC_GROUP, PEER_HEADS, PEER_TOPK), F32),
            pltpu.VMEM((SC_GROUP, D_MODEL), F32),
            pltpu.VMEM((D_MODEL,), F32),
            pltpu.VMEM((PEER_TOPK, D_MODEL), I32),
            pltpu.VMEM((PEER_TOPK, D_MODEL), I32),
            pltpu.VMEM((PEER_TOPK, D_MODEL), I32),
            pltpu.VMEM((PEER_TOPK, D_MODEL), I32),
            pltpu.SemaphoreType.DMA,
            pltpu.SemaphoreType.DMA,
            pltpu.SemaphoreType.DMA,
            pltpu.SemaphoreType.DMA,
        ],
        compiler_params=cp,
        cost_estimate=pl.CostEstimate(
            flops=SC_COST_SCALE * 4 * t * PEER_PICKS * D_MODEL,
            transcendentals=t * PEER_PICKS,
            bytes_accessed=SC_COST_SCALE * (t * PEER_PICKS * D_MODEL * 4 + 2 * t * D_MODEL * 4),
        ),
        name="peer_sc",
    )
    return run(table, eid.reshape(t, PEER_HEADS, PEER_TOPK), gate.reshape(t, PEER_HEADS, PEER_TOPK), h)


def _finish_body(x_ref, po_ref, mod_ref, fg_ref, o_ref, *, final_norm):
    y = x_ref[...] + mod_ref[0, 5:6, :] * po_ref[...]
    if final_norm:
        y = y * lax.rsqrt(jnp.mean(y * y, axis=-1, keepdims=True) + EPS) * fg_ref[...]
    o_ref[...] = y


def _peer_finish(x, po, mod, final_g, seq_len, final_norm, tm=512):
    t = x.shape[0]
    return pl.pallas_call(
        functools.partial(_finish_body, final_norm=final_norm),
        grid=(t // tm,),
        in_specs=[
            pl.BlockSpec((tm, D_MODEL), lambda i: (i, 0)),
            pl.BlockSpec((tm, D_MODEL), lambda i: (i, 0)),
            pl.BlockSpec((1, 6, D_MODEL), lambda i: ((i * tm) // seq_len, 0, 0)),
            pl.BlockSpec((1, D_MODEL), lambda i: (0, 0)),
        ],
        out_specs=pl.BlockSpec((tm, D_MODEL), lambda i: (i, 0)),
        out_shape=jax.ShapeDtypeStruct((t, D_MODEL), F32),
        compiler_params=_cparams("arbitrary"),
        name="peer_finish",
    )(x, po, mod, final_g)


def _peer_route_phase(x, mod, gamma, w_query, sub_keys, t_sc):
    nb, s, _ = x.shape
    t = nb * s
    h, e_t, g_t = _peer_route(x, mod, gamma, w_query, sub_keys)
    eid = e_t.transpose(0, 3, 1, 2).reshape(t, PEER_PICKS)
    gate = g_t.transpose(0, 3, 1, 2).reshape(t, PEER_PICKS)
    hf = h.reshape(t, D_MODEL)
    xf = x.reshape(t, D_MODEL)
    gate_tc = gate[t_sc:]
    gate_tc = jnp.stack([gate_tc, jnp.zeros_like(gate_tc)], axis=-1).reshape(t - t_sc, 2 * PEER_PICKS)
    return dict(sc=(eid[:t_sc], gate[:t_sc], hf[:t_sc]), x_sc=xf[:t_sc],
                tc=(eid[t_sc:], hf[t_sc:], gate_tc), x_tc=xf[t_sc:])


def _peer_retrieve_tc(ops, shape, mod, table, final_g, final_norm, t_sc):
    return _peer_gather(*ops["tc"], table, ops["x_tc"], mod, final_g, shape[1], final_norm, t_sc)


def _peer_retrieve_finish(ops, po, y_tc, shape, mod, final_g, final_norm):
    nb, s, _ = shape
    y_sc = _peer_finish(ops["x_sc"], po, mod, final_g, s, final_norm)
    return jnp.concatenate([y_sc, y_tc], axis=0).reshape(nb, s, D_MODEL)


def kernel(x_prompt, x_sample, c_prompt, c_sample, ln_mix_g, ln_ffn_g, w_mod, b_mod, rec_w_in, rec_conv_w,
           rec_conv_b, rec_ga_w, rec_ga_b, rec_gx_w, rec_gx_b, rec_lam, rec_w_out, att_w_qkv, att_q_g,
           att_k_g, att_w_o, peer_w_query, peer_sub_keys, peer_u, peer_v, final_g):
    tables = [_pack_table(peer_u[i], peer_v[i]) for i in range(DEPTH)]
    tables_sc = [tb.reshape(N_EXPERTS, D_MODEL) for tb in tables]
    fg = final_g.reshape(1, D_MODEL)

    xs = [x_prompt, x_sample]
    cs = [c_prompt, c_sample]
    for i in range(DEPTH):
        gm = ln_mix_g[i].reshape(1, D_MODEL)
        gf = ln_ffn_g[i].reshape(1, D_MODEL)
        j = i // 2
        last = i == DEPTH - 1
        for tr in TRUNK_ORDER[i]:
            x = xs[tr]
            mod = _mod(cs[tr], w_mod[i], b_mod[i])
            if i % 2 == 0:
                x = _rglru_layer(x, mod, gm, rec_w_in[j], rec_conv_w[j], rec_conv_b[j], rec_ga_w[j],
                                 rec_ga_b[j], rec_gx_w[j], rec_gx_b[j], rec_lam[j], rec_w_out[j])
            else:
                x = _attention_layer(x, mod, gm, att_w_qkv[j], att_q_g[j], att_k_g[j], att_w_o[j])
            t_sc = SC_TOKENS[i][tr]
            routed = _peer_route_phase(x, mod, gf, peer_w_query[i], peer_sub_keys[i], t_sc)
            y_tc = _peer_retrieve_tc(routed, x.shape, mod, tables[i], fg, last, t_sc)
            po = _peer_sc(tables_sc[i], *routed["sc"])
            xs[tr] = _peer_retrieve_finish(routed, po, y_tc, x.shape, mod, fg, last)
    return (xs[0], xs[1])
```

```python
import dataclasses
import functools

import jax
import jax.numpy as jnp
from jax import lax
from jax.experimental import pallas as pl
from jax.experimental.pallas import tpu as pltpu
from jax.experimental.pallas import tpu_sc as plsc

F32 = jnp.float32
BF16 = jnp.bfloat16
I32 = jnp.int32

D_MODEL = 1024
DEPTH = 2
GRID_W = 64
EPS = 1e-6
RNN_WIDTH = D_MODEL
RNN_BLOCKS = 16
RNN_BLOCK_W = RNN_WIDTH // RNN_BLOCKS
CONV_W = 4
LRU_C = 8.0
N_HEADS = 16
N_KV_HEADS = 4
HEAD_DIM = D_MODEL // N_HEADS
GROUP = N_HEADS // N_KV_HEADS
AXIS_DIM = HEAD_DIM // 2
ROPE_THETA = 10000.0
N_KEYS = 128
N_EXPERTS = N_KEYS * N_KEYS
PEER_HEADS = 8
PEER_TOPK = 16
PEER_QDIM = 256
PEER_HALF = PEER_QDIM // 2
PEER_PICKS = PEER_HEADS * PEER_TOPK

LANES = 128
SUBLANES = 8
VMEM_LIMIT = 48 * 1024 * 1024

GATHER_PITCH = 12
GATHER_TOKENS = 64
GATHER_SLOTS = 4
ROWS_PER_EXPERT = D_MODEL // LANES


def _cparams(*sem):
    return pltpu.CompilerParams(dimension_semantics=sem, vmem_limit_bytes=VMEM_LIMIT)


def _gelu(x):
    return jax.nn.gelu(x)


def _norm_mod(x, gamma, scale, shift):
    ms = jnp.mean(x * x, axis=-1, keepdims=True)
    y = x * lax.rsqrt(ms + EPS) * gamma
    return y * (1.0 + scale) + shift


def _mod_body(c_ref, w_ref, b_ref, o_ref):
    c = c_ref[...]
    s = c * jax.nn.sigmoid(c)
    o_ref[...] = jnp.dot(s, w_ref[...], preferred_element_type=F32) + b_ref[...]


def _mod(c, w, b):
    nb, n = c.shape[0], w.shape[1]
    tn = 1536
    out = pl.pallas_call(
        _mod_body,
        grid=(n // tn,),
        in_specs=[
            pl.BlockSpec((nb, D_MODEL), lambda j: (0, 0)),
            pl.BlockSpec((D_MODEL, tn), lambda j: (0, j)),
            pl.BlockSpec((1, tn), lambda j: (0, j)),
        ],
        out_specs=pl.BlockSpec((nb, tn), lambda j: (0, j)),
        out_shape=jax.ShapeDtypeStruct((nb, n), F32),
        compiler_params=_cparams("arbitrary"),
        name="adaln_mod",
    )(c, w, b.reshape(1, n))
    return out.reshape(nb, 6, D_MODEL)


def _nmm_body(x_ref, mod_ref, g_ref, w_ref, o_ref, *, sh, sc):
    h = _norm_mod(x_ref[0], g_ref[...], mod_ref[0, sc:sc + 1, :], mod_ref[0, sh:sh + 1, :])
    o_ref[0] = jnp.dot(h.astype(BF16), w_ref[...], preferred_element_type=F32)


def _norm_mod_matmul(x, mod, gamma, w_bf16, sh, sc, tm=512):
    nb, s, _ = x.shape
    n = w_bf16.shape[1]
    return pl.pallas_call(
        functools.partial(_nmm_body, sh=sh, sc=sc),
        grid=(nb, s // tm),
        in_specs=[
            pl.BlockSpec((1, tm, D_MODEL), lambda b, i: (b, i, 0)),
            pl.BlockSpec((1, 6, D_MODEL), lambda b, i: (b, 0, 0)),
            pl.BlockSpec((1, D_MODEL), lambda b, i: (0, 0)),
            pl.BlockSpec((D_MODEL, n), lambda b, i: (0, 0)),
        ],
        out_specs=pl.BlockSpec((1, tm, n), lambda b, i: (b, i, 0)),
        out_shape=jax.ShapeDtypeStruct((nb, s, n), F32),
        compiler_params=_cparams("arbitrary", "arbitrary"),
        name="norm_mod_matmul",
    )(x, mod, gamma, w_bf16)


def _proj_res_body(m_ref, w_ref, x_ref, mod_ref, o_ref, *, gi):
    y = jnp.dot(m_ref[0].astype(BF16), w_ref[...], preferred_element_type=F32)
    o_ref[0] = x_ref[0] + mod_ref[0, gi:gi + 1, :] * y


def _proj_residual(m, w_bf16, x, mod, gi, tm=512):
    nb, s, k = m.shape
    return pl.pallas_call(
        functools.partial(_proj_res_body, gi=gi),
        grid=(nb, s // tm),
        in_specs=[
            pl.BlockSpec((1, tm, k), lambda b, i: (b, i, 0)),
            pl.BlockSpec((k, D_MODEL), lambda b, i: (0, 0)),
            pl.BlockSpec((1, tm, D_MODEL), lambda b, i: (b, i, 0)),
            pl.BlockSpec((1, 6, D_MODEL), lambda b, i: (b, 0, 0)),
        ],
        out_specs=pl.BlockSpec((1, tm, D_MODEL), lambda b, i: (b, i, 0)),
        out_shape=jax.ShapeDtypeStruct((nb, s, D_MODEL), F32),
        compiler_params=_cparams("arbitrary", "arbitrary"),
        name="proj_residual",
    )(m, w_bf16, x, mod)


def _scan_tile(a, b, carry, reverse):
    tt = a.shape[0]
    row = lax.broadcasted_iota(I32, (tt, LANES), 0) % SUBLANES
    for d in (1, 2, 4):
        if reverse:
            shift, keep = tt - d, row < SUBLANES - d
        else:
            shift, keep = d, row >= d
        ap = pltpu.roll(a, shift, 0)
        bp = pltpu.roll(b, shift, 0)
        b = jnp.where(keep, a * bp + b, b)
        a = jnp.where(keep, a * ap, a)
    groups = tt // SUBLANES
    hs = [None] * groups
    order = range(groups - 1, -1, -1) if reverse else range(groups)
    for g in order:
        lo = g * SUBLANES
        h = a[lo:lo + SUBLANES] * carry + b[lo:lo + SUBLANES]
        hs[g] = h
        last = h[0:1] if reverse else h[SUBLANES - 1:SUBLANES]
        carry = jnp.broadcast_to(last, (SUBLANES, LANES))
    return jnp.concatenate(hs, axis=0), carry


def _rglru_body(y_ref, x_ref, cw_ref, cb_ref, gaw_ref, gab_ref, gxw_ref, gxb_ref, nc_ref,
                o_ref, xp, hf, *, s, tt):
    nt = s // tt
    pad = SUBLANES
    zeros = jnp.zeros((pad, LANES), F32)
    xp[0:pad, :] = zeros
    xp[s + pad:s + 2 * pad, :] = zeros

    def copy_tile(i, c):
        r0 = pl.multiple_of(i * tt, tt)
        xp[pl.ds(pl.multiple_of(r0 + pad, SUBLANES), tt), :] = x_ref[0, pl.ds(r0, tt), :]
        return c

    lax.fori_loop(0, nt, copy_tile, 0)

    cw = cw_ref[...]
    cb = cb_ref[...]

    def conv_tile(r0):
        win = xp[pl.ds(r0, tt + 2 * pad), :]
        acc = cb + cw[2:3] * win[pad:pad + tt]
        for k in (0, 1, 3):
            shifted = pltpu.roll(win, (2 - k) % (tt + 2 * pad), 0)
            acc = acc + cw[k:k + 1] * shifted[pad:pad + tt]
        return acc

    def gates(d, xc):
        xb = xc.astype(BF16)
        r = jax.nn.sigmoid(jnp.dot(xb, gaw_ref[d, 0], preferred_element_type=F32) + gab_ref[d])
        ig = jax.nn.sigmoid(jnp.dot(xb, gxw_ref[d, 0], preferred_element_type=F32) + gxb_ref[d])
        a = jnp.exp(nc_ref[d] * r)
        b = jnp.sqrt(1.0 - a * a) * (ig * xc)
        return a, b

    carry0 = jnp.zeros((SUBLANES, LANES), F32)

    def fwd(i, carry):
        r0 = pl.multiple_of(i * tt, tt)
        a, b = gates(0, conv_tile(r0))
        h, carry = _scan_tile(a, b, carry, False)
        hf[pl.ds(r0, tt), :] = h
        return carry

    lax.fori_loop(0, nt, fwd, carry0)

    def bwd(ii, carry):
        r0 = pl.multiple_of((nt - 1 - ii) * tt, tt)
        a, b = gates(1, conv_tile(r0))
        h, carry = _scan_tile(a, b, carry, True)
        o_ref[0, pl.ds(r0, tt), :] = (hf[pl.ds(r0, tt), :] + h) * _gelu(y_ref[0, pl.ds(r0, tt), :])
        return carry

    lax.fori_loop(0, nt, bwd, carry0)


def _rglru_core(u, conv_w, conv_b, gaw, gab, gxw, gxb, negc, tt=256):
    nb, s, _ = u.shape
    ng = RNN_WIDTH // LANES
    return pl.pallas_call(
        functools.partial(_rglru_body, s=s, tt=tt),
        grid=(nb, ng),
        in_specs=[
            pl.BlockSpec((1, s, LANES), lambda b, j: (b, 0, j)),
            pl.BlockSpec((1, s, LANES), lambda b, j: (b, 0, ng + j)),
            pl.BlockSpec((CONV_W, LANES), lambda b, j: (0, j)),
            pl.BlockSpec((1, LANES), lambda b, j: (0, j)),
            pl.BlockSpec((2, 1, LANES, LANES), lambda b, j: (0, j, 0, 0)),
            pl.BlockSpec((2, 1, LANES), lambda b, j: (0, 0, j)),
            pl.BlockSpec((2, 1, LANES, LANES), lambda b, j: (0, j, 0, 0)),
            pl.BlockSpec((2, 1, LANES), lambda b, j: (0, 0, j)),
            pl.BlockSpec((2, 1, LANES), lambda b, j: (0, 0, j)),
        ],
        out_specs=pl.BlockSpec((1, s, LANES), lambda b, j: (b, 0, j)),
        out_shape=jax.ShapeDtypeStruct((nb, s, RNN_WIDTH), F32),
        scratch_shapes=[pltpu.VMEM((s + 2 * SUBLANES, LANES), F32), pltpu.VMEM((s, LANES), F32)],
        compiler_params=_cparams("arbitrary", "arbitrary"),
        name="rglru_core",
    )(u, u, conv_w, conv_b, gaw, gab, gxw, gxb, negc)


def _block_diag_groups(w):
    ng = RNN_WIDTH // LANES
    w = w.reshape(ng, 2, RNN_BLOCK_W, RNN_BLOCK_W)
    out = jnp.zeros((ng, LANES, LANES), w.dtype)
    out = out.at[:, :RNN_BLOCK_W, :RNN_BLOCK_W].set(w[:, 0])
    out = out.at[:, RNN_BLOCK_W:, RNN_BLOCK_W:].set(w[:, 1])
    return out


def _rglru_layer(x, mod, gamma, w_in, conv_w, conv_b, ga_w, ga_b, gx_w, gx_b, lam, w_out):
    u = _norm_mod_matmul(x, mod, gamma, w_in.astype(BF16), sh=0, sc=1)
    gaw = jnp.stack([_block_diag_groups(ga_w[d]) for d in range(2)]).astype(BF16)
    gxw = jnp.stack([_block_diag_groups(gx_w[d]) for d in range(2)]).astype(BF16)
    gab = ga_b.reshape(2, 1, RNN_WIDTH)
    gxb = gx_b.reshape(2, 1, RNN_WIDTH)
    negc = (-LRU_C * jax.nn.softplus(-lam)).reshape(2, 1, RNN_WIDTH)
    m = _rglru_core(u, conv_w, conv_b.reshape(1, RNN_WIDTH), gaw, gab, gxw, gxb, negc)
    return _proj_residual(m, w_out.astype(BF16), x, mod, gi=2)


def _seg_mean(x2, seg_ref):
    hi = x2.astype(BF16)
    lo = (x2 - hi.astype(F32)).astype(BF16)
    return (jnp.dot(hi, seg_ref[...], preferred_element_type=F32)
            + jnp.dot(lo, seg_ref[...], preferred_element_type=F32))


def _rope(x, cos, sin, lane_lo):
    outs = []
    for j in range(x.shape[1] // LANES):
        xt = x[:, j * LANES:(j + 1) * LANES]
        rot = jnp.where(lane_lo, pltpu.roll(xt, LANES - AXIS_DIM // 2, 1), pltpu.roll(xt, AXIS_DIM // 2, 1))
        outs.append(xt * cos + rot * sin)
    return jnp.concatenate(outs, axis=1) if len(outs) > 1 else outs[0]


def _qkv_body(x_ref, mod_ref, g_ref, w_ref, segq_ref, segk_ref, qg_ref, kg_ref, cosq_ref, sinq_ref,
              cosk_ref, sink_ref, q_ref, kt_ref, v_ref, *, tm):
    h = _norm_mod(x_ref[0], g_ref[...], mod_ref[0, 1:2, :], mod_ref[0, 0:1, :])
    qkv = jnp.dot(h.astype(BF16), w_ref[...], preferred_element_type=F32)
    nq = N_HEADS * HEAD_DIM
    nk = N_KV_HEADS * HEAD_DIM
    q = qkv[:, :nq]
    k = qkv[:, nq:nq + nk]
    v = qkv[:, nq + nk:]
    lane = lax.broadcasted_iota(I32, (tm, LANES), 1)
    lane_lo = (lane % AXIS_DIM) < (AXIS_DIM // 2)
    q = q * lax.rsqrt(_seg_mean(q * q, segq_ref) + EPS) * qg_ref[...]
    k = k * lax.rsqrt(_seg_mean(k * k, segk_ref) + EPS) * kg_ref[...]
    q = _rope(q, cosq_ref[...], sinq_ref[...], lane_lo)
    k = _rope(k, cosk_ref[...], sink_ref[...], lane_lo)
    q_ref[0] = q.astype(BF16)
    kt = k.T.astype(BF16)
    for g in range(N_KV_HEADS):
        kt_ref[0, g] = kt[g * HEAD_DIM:(g + 1) * HEAD_DIM, :]
        v_ref[0, g] = v[:, g * HEAD_DIM:(g + 1) * HEAD_DIM].astype(BF16)


def _rope_tables(s):
    rows = s // GRID_W
    row = jnp.repeat(jnp.arange(rows, dtype=F32), GRID_W)
    col = jnp.tile(jnp.arange(GRID_W, dtype=F32), rows)
    inv = ROPE_THETA ** (-jnp.arange(0, AXIS_DIM, 2, dtype=F32) / AXIS_DIM)
    ar = row[:, None] * inv
    ac = col[:, None] * inv
    cos = jnp.concatenate([jnp.cos(ar), jnp.cos(ar), jnp.cos(ac), jnp.cos(ac)], axis=1)
    sin = jnp.concatenate([-jnp.sin(ar), jnp.sin(ar), -jnp.sin(ac), jnp.sin(ac)], axis=1)
    return jnp.tile(cos, (1, LANES // HEAD_DIM)), jnp.tile(sin, (1, LANES // HEAD_DIM))


def _attn_body(q_ref, kt_ref, v_ref, o_ref):
    kt = kt_ref[0, 0]
    v = v_ref[0, 0]
    outs = []
    for hh in range(GROUP):
        qh = q_ref[0, :, hh * HEAD_DIM:(hh + 1) * HEAD_DIM]
        sc = jnp.dot(qh, kt, preferred_element_type=F32)
        m = jnp.max(sc, axis=-1, keepdims=True)
        p = jnp.exp(sc - m)
        l = jnp.sum(p, axis=-1, keepdims=True)
        o = jnp.dot(p.astype(BF16), v, preferred_element_type=F32)
        outs.append(o / l)
    o_ref[0] = jnp.concatenate(outs, axis=1).astype(BF16)


def _attention_layer(x, mod, gamma, w_qkv, q_g, k_g, w_o, tm=256, tq=128):
    nb, s, _ = x.shape
    nq = N_HEADS * HEAD_DIM
    nk = N_KV_HEADS * HEAD_DIM
    seg = jnp.kron(jnp.eye(N_HEADS, dtype=F32), jnp.full((HEAD_DIM, HEAD_DIM), 1.0 / HEAD_DIM, F32)).astype(BF16)
    segk = seg[:nk, :nk]
    cos, sin = _rope_tables(s)
    scale = HEAD_DIM ** -0.5
    q, kt, v = pl.pallas_call(
        functools.partial(_qkv_body, tm=tm),
        grid=(nb, s // tm),
        in_specs=[
            pl.BlockSpec((1, tm, D_MODEL), lambda b, i: (b, i, 0)),
            pl.BlockSpec((1, 6, D_MODEL), lambda b, i: (b, 0, 0)),
            pl.BlockSpec((1, D_MODEL), lambda b, i: (0, 0)),
            pl.BlockSpec((D_MODEL, nq + 2 * nk), lambda b, i: (0, 0)),
            pl.BlockSpec((nq, nq), lambda b, i: (0, 0)),
            pl.BlockSpec((nk, nk), lambda b, i: (0, 0)),
            pl.BlockSpec((1, nq), lambda b, i: (0, 0)),
            pl.BlockSpec((1, nk), lambda b, i: (0, 0)),
            pl.BlockSpec((tm, LANES), lambda b, i: (i, 0)),
            pl.BlockSpec((tm, LANES), lambda b, i: (i, 0)),
            pl.BlockSpec((tm, LANES), lambda b, i: (i, 0)),
            pl.BlockSpec((tm, LANES), lambda b, i: (i, 0)),
        ],
        out_specs=[
            pl.BlockSpec((1, tm, nq), lambda b, i: (b, i, 0)),
            pl.BlockSpec((1, N_KV_HEADS, HEAD_DIM, tm), lambda b, i: (b, 0, 0, i)),
            pl.BlockSpec((1, N_KV_HEADS, tm, HEAD_DIM), lambda b, i: (b, 0, i, 0)),
        ],
        out_shape=[
            jax.ShapeDtypeStruct((nb, s, nq), BF16),
            jax.ShapeDtypeStruct((nb, N_KV_HEADS, HEAD_DIM, s), BF16),
            jax.ShapeDtypeStruct((nb, N_KV_HEADS, s, HEAD_DIM), BF16),
        ],
        compiler_params=_cparams("arbitrary", "arbitrary"),
        name="qkv_rope",
    )(x, mod, gamma, w_qkv.astype(BF16), seg, segk,
      jnp.tile(q_g, N_HEADS).reshape(1, nq), jnp.tile(k_g, N_KV_HEADS).reshape(1, nk),
      cos * scale, sin * scale, cos, sin)

    gw = GROUP * HEAD_DIM
    o = pl.pallas_call(
        _attn_body,
        grid=(nb, N_KV_HEADS, s // tq),
        in_specs=[
            pl.BlockSpec((1, tq, gw), lambda b, g, i: (b, i, g)),
            pl.BlockSpec((1, 1, HEAD_DIM, s), lambda b, g, i: (b, g, 0, 0)),
            pl.BlockSpec((1, 1, s, HEAD_DIM), lambda b, g, i: (b, g, 0, 0)),
        ],
        out_specs=pl.BlockSpec((1, tq, gw), lambda b, g, i: (b, i, g)),
        out_shape=jax.ShapeDtypeStruct((nb, s, nq), BF16),
        compiler_params=_cparams("arbitrary", "arbitrary", "arbitrary"),
        name="attention",
    )(q, kt, v)
    return _proj_residual(o, w_o.astype(BF16), x, mod, gi=2)


def _topk_rows(s, ids, k, id_bound):
    vals, picks = [], []
    for _ in range(k):
        m = jnp.max(s, axis=0, keepdims=True)
        i = jnp.min(jnp.where(s == m, ids, id_bound), axis=0, keepdims=True)
        vals.append(m)
        picks.append(i)
        s = jnp.where(ids == i, -jnp.inf, s)
    return jnp.concatenate(vals, axis=0), jnp.concatenate(picks, axis=0)


ROUTE_HEADS_PER_STEP = 8


def _route_body(x_ref, mod_ref, g_ref, wq_ref, keys_ref, h_ref, e_ref, gate_ref, hb, *, tm):
    hd = pl.program_id(2)

    @pl.when(hd == 0)
    def _():
        h = _norm_mod(x_ref[0], g_ref[...], mod_ref[0, 4:5, :], mod_ref[0, 3:4, :])
        h_ref[0] = h
        hb[...] = h.astype(BF16)

    q = jnp.dot(hb[...], wq_ref[...], preferred_element_type=F32)
    for hh in range(ROUTE_HEADS_PER_STEP):
        gate, eid = _route_head(q[:, hh * PEER_QDIM:(hh + 1) * PEER_QDIM], keys_ref, tm)
        gate_ref[0, hh] = gate
        e_ref[0, hh] = eid


def _route_head(q, keys_ref, tm):
    key_ids = lax.broadcasted_iota(I32, (N_KEYS, tm), 0)
    tops = []
    for p in range(2):
        qp = q[:, p * PEER_HALF:(p + 1) * PEER_HALF].astype(BF16)
        st = lax.dot_general(keys_ref[p], qp, (((1,), (1,)), ((), ())), preferred_element_type=F32)
        tops.append(_topk_rows(st, key_ids, PEER_TOPK, N_KEYS))
    (v0, i0), (v1, i1) = tops

    r16 = lax.broadcasted_iota(I32, (PEER_TOPK, tm), 0)
    cand, fid, eid = [], [], []
    for a in range(4):
        cand.append(v0[a:a + 1] + v1)
        fid.append(a * PEER_TOPK + r16)
        eid.append(i0[a:a + 1] * N_KEYS + i1)
    for b in range(3):
        cand.append(jnp.where(r16 >= 4, v0 + v1[b:b + 1], -jnp.inf))
        fid.append(r16 * PEER_TOPK + b)
        eid.append(i0 * N_KEYS + i1[b:b + 1])
    cand = jnp.concatenate(cand, axis=0)
    fid = jnp.concatenate(fid, axis=0)
    eid = jnp.concatenate(eid, axis=0)

    best, chosen = [], []
    for _ in range(PEER_TOPK):
        m = jnp.max(cand, axis=0, keepdims=True)
        f = jnp.min(jnp.where(cand == m, fid, PEER_TOPK * PEER_TOPK), axis=0, keepdims=True)
        hit = fid == f
        best.append(m)
        chosen.append(jnp.max(jnp.where(hit, eid, -1), axis=0, keepdims=True))
        cand = jnp.where(hit, -jnp.inf, cand)
    best = jnp.concatenate(best, axis=0)
    ex = jnp.exp(best - best[0:1])
    return ex / jnp.sum(ex, axis=0, keepdims=True), jnp.concatenate(chosen, axis=0)


def _peer_route(x, mod, gamma, w_query, sub_keys, tm=128):
    nb, s, _ = x.shape
    hps = ROUTE_HEADS_PER_STEP
    return pl.pallas_call(
        functools.partial(_route_body, tm=tm),
        grid=(nb, s // tm, PEER_HEADS // hps),
        in_specs=[
            pl.BlockSpec((1, tm, D_MODEL), lambda b, i, h: (b, i, 0)),
            pl.BlockSpec((1, 6, D_MODEL), lambda b, i, h: (b, 0, 0)),
            pl.BlockSpec((1, D_MODEL), lambda b, i, h: (0, 0)),
            pl.BlockSpec((D_MODEL, hps * PEER_QDIM), lambda b, i, h: (0, h)),
            pl.BlockSpec((2, N_KEYS, PEER_HALF), lambda b, i, h: (0, 0, 0)),
        ],
        out_specs=[
            pl.BlockSpec((1, tm, D_MODEL), lambda b, i, h: (b, i, 0)),
            pl.BlockSpec((1, hps, PEER_TOPK, tm), lambda b, i, h: (b, h, 0, i)),
            pl.BlockSpec((1, hps, PEER_TOPK, tm), lambda b, i, h: (b, h, 0, i)),
        ],
        out_shape=[
            jax.ShapeDtypeStruct((nb, s, D_MODEL), F32),
            jax.ShapeDtypeStruct((nb, PEER_HEADS, PEER_TOPK, s), I32),
            jax.ShapeDtypeStruct((nb, PEER_HEADS, PEER_TOPK, s), F32),
        ],
        scratch_shapes=[pltpu.VMEM((tm, D_MODEL), BF16)],
        compiler_params=_cparams("arbitrary", "arbitrary", "arbitrary"),
        name="peer_route",
    )(x, mod, gamma, w_query.astype(BF16), sub_keys.astype(BF16))


def _pack_body(u_ref, v_ref, o_ref):
    o_ref[...] = pltpu.pack_elementwise([u_ref[...], v_ref[...]], packed_dtype=BF16).astype(I32)


def _pack_table(u, v, te=512):
    ne = u.shape[0]
    return pl.pallas_call(
        _pack_body,
        grid=(ne // te,),
        in_specs=[pl.BlockSpec((te, D_MODEL), lambda i: (i, 0))] * 2,
        out_specs=pl.BlockSpec((te, D_MODEL), lambda i: (i, 0)),
        out_shape=jax.ShapeDtypeStruct((ne, D_MODEL), I32),
        compiler_params=_cparams("arbitrary"),
        name="peer_pack",
    )(u, v)


def _gather_body(eid_ref, h_ref, g_ref, tab_ref, x_ref, mod_ref, fg_ref, o_ref, buf, sem, po,
                 *, tb, final_norm):
    ne = PEER_PICKS
    rows = ROWS_PER_EXPERT

    ns = GATHER_SLOTS
    ahead = ns - 1

    def issue(t, slot, lo, hi):
        for k in range(lo, hi):
            src = tab_ref.at[pl.ds(pl.multiple_of(eid_ref[t, k] * rows, rows), rows), :]
            dst = buf.at[slot, pl.ds(GATHER_PITCH * k, rows), :]
            pltpu.make_async_copy(src, dst, sem.at[slot]).start(priority=k % 2)

    def wait(slot):
        pltpu.make_async_copy(tab_ref.at[pl.ds(0, ne * rows), :], buf.at[slot, pl.ds(0, ne * rows), :],
                              sem.at[slot]).wait()

    for q in range(ahead):
        issue(q, q, 0, ne)
    early = 4
    halves = 2
    per_half = ne // halves
    even = lax.broadcasted_iota(I32, (1, 2 * LANES), 1) % 2 == 0
    row_id = lax.broadcasted_iota(I32, (SUBLANES, LANES), 0)

    def chunk(slot, c, first=0, count=ne):
        words = buf[slot, pl.ds(first * GATHER_PITCH + c, count, stride=GATHER_PITCH), :]
        return pltpu.bitcast(words, BF16)

    def score_half(t8, j, hf, c, acc):
        xrow = h_ref[pl.ds(t8, SUBLANES), c * LANES:(c + 1) * LANES][j:j + 1]
        return acc + chunk(j % ns, c, hf * per_half, per_half).astype(F32) * xrow

    def lane_sums(acc):
        return jnp.sum(acc.T, axis=0, keepdims=True)

    def coefficients(parts, t8, j):
        s = jnp.concatenate(parts, axis=1)
        g = g_ref[pl.ds(t8, SUBLANES), :][j:j + 1]
        coef = pltpu.roll(jnp.where(even, _gelu(s) * g, 0.0), 1, 1)
        return jnp.broadcast_to(coef, (SUBLANES, 2 * LANES))

    wait(0)
    parts = []
    for hf in range(halves):
        a = jnp.zeros((2 * per_half, LANES), F32)
        for c in range(rows):
            a = score_half(0, 0, hf, c, a)
        parts.append(lane_sums(a))
    coef0 = coefficients(parts, 0, 0)

    def body(it, coef):
        t8 = pl.multiple_of(it * SUBLANES, SUBLANES)
        t8_next = pl.multiple_of(jnp.minimum(t8 + SUBLANES, tb - SUBLANES), SUBLANES)
        acc = [None] * rows
        for j in range(SUBLANES):
            jn = (j + 1) % SUBLANES
            t8n = t8 if j + 1 < SUBLANES else t8_next
            nxt = jnp.minimum(t8 + j + ahead, tb - 1)
            nslot = (j + ahead) % ns
            wait((j + 1) % ns)
            coef_b = coef.astype(BF16)
            step = 0
            parts = []
            for hf in range(halves):
                a = jnp.zeros((2 * per_half, LANES), F32)
                for c in range(rows):
                    issue(nxt, nslot, step * early, (step + 1) * early)
                    step += 1
                    a = score_half(t8n, jn, hf, c, a)
                parts.append(lane_sums(a))
            for c in range(rows):
                issue(nxt, nslot, step * early, (step + 1) * early)
                step += 1
                o = jnp.dot(coef_b, chunk(j % ns, c), preferred_element_type=F32)
                acc[c] = o if j == 0 else jnp.where(row_id == j, o, acc[c])
            issue(nxt, nslot, step * early, ne)
            coef = coefficients(parts, t8n, jn)
        for c in range(rows):
            po[pl.ds(t8, SUBLANES), c * LANES:(c + 1) * LANES] = acc[c]
        return coef

    lax.fori_loop(0, tb // SUBLANES, body, coef0)
    for q in range(1, ahead):
        wait((tb + q) % ns)
    y = x_ref[...] + mod_ref[0, 5:6, :] * po[...]
    if final_norm:
        y = y * lax.rsqrt(jnp.mean(y * y, axis=-1, keepdims=True) + EPS) * fg_ref[...]
    o_ref[...] = y


def _peer_gather(eid, h, g, table, x, mod, final_g, seq_len, final_norm, tok_offset, tb=GATHER_TOKENS):
    t = eid.shape[0]
    off = tok_offset // tb
    return pl.pallas_call(
        functools.partial(_gather_body, tb=tb, final_norm=final_norm),
        grid=(t // tb,),
        in_specs=[
            pl.BlockSpec((tb, PEER_PICKS), lambda i: (i, 0), memory_space=pltpu.SMEM),
            pl.BlockSpec((tb, D_MODEL), lambda i: (off + i, 0)),
            pl.BlockSpec((tb, 2 * PEER_PICKS), lambda i: (i, 0)),
            pl.BlockSpec(memory_space=pl.ANY),
            pl.BlockSpec((tb, D_MODEL), lambda i: (off + i, 0)),
            pl.BlockSpec((1, 6, D_MODEL), lambda i: ((tok_offset + i * tb) // seq_len, 0, 0)),
            pl.BlockSpec((1, D_MODEL), lambda i: (0, 0)),
        ],
        out_specs=pl.BlockSpec((tb, D_MODEL), lambda i: (i, 0)),
        out_shape=jax.ShapeDtypeStruct((t, D_MODEL), F32),
        scratch_shapes=[
            pltpu.VMEM((GATHER_SLOTS, PEER_PICKS * GATHER_PITCH, LANES), I32),
            pltpu.SemaphoreType.DMA((GATHER_SLOTS,)),
            pltpu.VMEM((tb, D_MODEL), F32),
        ],
        compiler_params=_cparams("arbitrary"),
        name="peer_gather",
    )(eid, h, g, table, x, mod, final_g)


SC_LANES = 16
SC_WORKERS = 32
SC_GROUP = 8
SC_TOKENS = ((8192, 14336), (9216, 12288))
SC_COST_SCALE = 1
TRUNK_ORDER = ((0, 1), (0, 1))
SC_CHUNKS = D_MODEL // SC_LANES


def _sc_body(tab_hbm, eid_hbm, g_hbm, h_hbm, out_hbm, idx_v, g_v, x_v, o_v, rows0, rows1, rows2, rows3,
             sem0, sem1, sem2, sem3, *, tpw):
    nl = SC_LANES
    wid = lax.axis_index("s") * 2 + lax.axis_index("c")
    lane = lax.iota(I32, nl)

    def permute(x, idx):
        return jnp.take_along_axis(x, idx, axis=0, mode="promise_in_bounds")
    rows = (rows0, rows1, rows2, rows3)
    sems = (sem0, sem1, sem2, sem3)
    ns = len(rows)
    units = SC_GROUP * PEER_HEADS

    def gather(tok, hd, slot):
        return pltpu.make_async_copy(tab_hbm.at[idx_v.at[tok, hd]], rows[slot], sems[slot])

    def compute(tok, hd, slot, t):
        rv = rows[slot]

        @pl.when(hd == 0)
        def _():
            def zero(c, cc):
                o_v[pl.ds(c * nl, nl)] = jnp.zeros((nl,), F32)
                return cc
            lax.fori_loop(0, SC_CHUNKS, zero, 0)

        def score(c, accs):
            x = x_v[tok, pl.ds(c * nl, nl)]
            out = []
            for kk in range(PEER_TOPK):
                u = plsc.bitcast(lax.shift_left(rv[kk, pl.ds(c * nl, nl)], 16), F32)
                out.append(accs[kk] + u * x)
            return tuple(out)

        accs = lax.fori_loop(0, SC_CHUNKS, score, tuple(jnp.zeros((nl,), F32) for _ in range(PEER_TOPK)))
        vecs = list(accs)
        d = 1
        while len(vecs) > 1:
            partner = lane ^ d
            take_lo = (lane & d) == 0
            nxt_vecs = []
            for i in range(0, len(vecs), 2):
                a = vecs[i] + permute(vecs[i], partner)
                b = vecs[i + 1] + permute(vecs[i + 1], partner)
                nxt_vecs.append(jnp.where(take_lo, a, b))
            vecs = nxt_vecs
            d *= 2
        s = vecs[0]
        z = 0.7978845608028654 * (s + 0.044715 * s * s * s)
        act = s * (1.0 - 1.0 / (jnp.exp(2.0 * z) + 1.0))
        coef = act * g_v[tok, hd, :]
        coefs = [permute(coef, jnp.full((nl,), kk, I32)) for kk in range(PEER_TOPK)]

        def combine(c, cc):
            parts = []
            for q in range(4):
                acc = None
                for kk in range(q * PEER_TOPK // 4, (q + 1) * PEER_TOPK // 4):
                    v = plsc.bitcast(lax.bitwise_and(rv[kk, pl.ds(c * nl, nl)], jnp.int32(-65536)), F32)
                    term = coefs[kk] * v
                    acc = term if acc is None else acc + term
                parts.append(acc)
            o_v[pl.ds(c * nl, nl)] = o_v[pl.ds(c * nl, nl)] + ((parts[0] + parts[1]) + (parts[2] + parts[3]))
            return cc

        lax.fori_loop(0, SC_CHUNKS, combine, 0)

        @pl.when(hd == PEER_HEADS - 1)
        def _():
            pltpu.sync_copy(o_v, out_hbm.at[t])

    def group(gi, carry):
        t0 = wid * tpw + gi * SC_GROUP
        pltpu.sync_copy(eid_hbm.at[pl.ds(t0, SC_GROUP)], idx_v)
        pltpu.sync_copy(g_hbm.at[pl.ds(t0, SC_GROUP)], g_v)
        pltpu.sync_copy(h_hbm.at[pl.ds(t0, SC_GROUP)], x_v)
        for q in range(ns - 1):
            gather(0, q, q).start()

        def ring(ri, cc):
            j0 = ri * ns
            tok, hd0 = j0 // PEER_HEADS, j0 % PEER_HEADS
            for q in range(ns):
                jn = j0 + q + ns - 1

                @pl.when(jn < units)
                def _():
                    gather(jn // PEER_HEADS, jn % PEER_HEADS, (q + ns - 1) % ns).start()

                gather(tok, hd0 + q, q).wait()
                compute(tok, hd0 + q, q, t0 + tok)
            return cc

        lax.fori_loop(0, units // ns, ring, 0)
        return carry

    lax.fori_loop(0, tpw // SC_GROUP, group, 0)


def _peer_sc(table, eid, gate, h):
    t = eid.shape[0]
    tpw = t // SC_WORKERS
    cp = pltpu.CompilerParams()
    if "needs_layout_passes" in pltpu.CompilerParams.__dataclass_fields__:
        cp = dataclasses.replace(cp, needs_layout_passes=False)
    run = pl.kernel(
        functools.partial(_sc_body, tpw=tpw),
        out_type=jax.ShapeDtypeStruct((t, D_MODEL), F32),
        mesh=plsc.VectorSubcoreMesh(core_axis_name="c", subcore_axis_name="s"),
        scratch_types=[
            pltpu.VMEM((SC_GROUP, PEER_HEADS, PEER_TOPK), I32),
            pltpu.VMEM((SC_GROUP, PEER_HEADS, PEER_TOPK), F32),
            pltpu.VMEM((SC_GROUP, D_MODEL), F32),
            pltpu.VMEM((D_MODEL,), F32),
            pltpu.VMEM((PEER_TOPK, D_MODEL), I32),
            pltpu.VMEM((PEER_TOPK, D_MODEL), I32),
            pltpu.VMEM((PEER_TOPK, D_MODEL), I32),
            pltpu.VMEM((PEER_TOPK, D_MODEL), I32),
            pltpu.SemaphoreType.DMA,
            pltpu.SemaphoreType.DMA,
            pltpu.SemaphoreType.DMA,
            pltpu.SemaphoreType.DMA,
        ],
        compiler_params=cp,
        cost_estimate=pl.CostEstimate(
            flops=SC_COST_SCALE * 4 * t * PEER_PICKS * D_MODEL,
            transcendentals=t * PEER_PICKS,
            bytes_accessed=SC_COST_SCALE * (t * PEER_PICKS * D_MODEL * 4 + 2 * t * D_MODEL * 4),
        ),
        name="peer_sc",
    )
    return run(table, eid.reshape(t, PEER_HEADS, PEER_TOPK), gate.reshape(t, PEER_HEADS, PEER_TOPK), h)


def _finish_body(x_ref, po_ref, mod_ref, fg_ref, o_ref, *, final_norm):
    y = x_ref[...] + mod_ref[0, 5:6, :] * po_ref[...]
    if final_norm:
        y = y * lax.rsqrt(jnp.mean(y * y, axis=-1, keepdims=True) + EPS) * fg_ref[...]
    o_ref[...] = y


def _peer_finish(x, po, mod, final_g, seq_len, final_norm, tm=512):
    t = po.shape[0]
    return pl.pallas_call(
        functools.partial(_finish_body, final_norm=final_norm),
        grid=(t // tm,),
        in_specs=[
            pl.BlockSpec((tm, D_MODEL), lambda i: (i, 0)),
            pl.BlockSpec((tm, D_MODEL), lambda i: (i, 0)),
            pl.BlockSpec((1, 6, D_MODEL), lambda i: ((i * tm) // seq_len, 0, 0)),
            pl.BlockSpec((1, D_MODEL), lambda i: (0, 0)),
        ],
        out_specs=pl.BlockSpec((tm, D_MODEL), lambda i: (i, 0)),
        out_shape=jax.ShapeDtypeStruct((t, D_MODEL), F32),
        compiler_params=_cparams("arbitrary"),
        name="peer_finish",
    )(x, po, mod, final_g)


def _peer_route_phase(x, mod, gamma, w_query, sub_keys, t_sc):
    nb, s, _ = x.shape
    t = nb * s
    h, e_t, g_t = _peer_route(x, mod, gamma, w_query, sub_keys)
    eid = e_t.transpose(0, 3, 1, 2).reshape(t, PEER_PICKS)
    gate = g_t.transpose(0, 3, 1, 2).reshape(t, PEER_PICKS)
    hf = h.reshape(t, D_MODEL)
    xf = x.reshape(t, D_MODEL)
    gate_tc = gate[t_sc:]
    gate_tc = jnp.stack([gate_tc, jnp.zeros_like(gate_tc)], axis=-1).reshape(t - t_sc, 2 * PEER_PICKS)
    return dict(sc=(eid[:t_sc], gate[:t_sc], hf), x_sc=xf, tc=(eid[t_sc:], hf, gate_tc), x_tc=xf)


def _peer_retrieve_tc(ops, shape, mod, table, final_g, final_norm, t_sc):
    return _peer_gather(*ops["tc"], table, ops["x_tc"], mod, final_g, shape[1], final_norm, t_sc)


def _peer_retrieve_finish(ops, po, y_tc, shape, mod, final_g, final_norm):
    nb, s, _ = shape
    y_sc = _peer_finish(ops["x_sc"], po, mod, final_g, s, final_norm)
    return jnp.concatenate([y_sc, y_tc], axis=0).reshape(nb, s, D_MODEL)


def kernel(x_prompt, x_sample, c_prompt, c_sample, ln_mix_g, ln_ffn_g, w_mod, b_mod, rec_w_in, rec_conv_w,
           rec_conv_b, rec_ga_w, rec_ga_b, rec_gx_w, rec_gx_b, rec_lam, rec_w_out, att_w_qkv, att_q_g,
           att_k_g, att_w_o, peer_w_query, peer_sub_keys, peer_u, peer_v, final_g):
    tables_sc = [_pack_table(peer_u[i], peer_v[i]) for i in range(DEPTH)]
    tables = [tb.reshape(N_EXPERTS * ROWS_PER_EXPERT, LANES) for tb in tables_sc]
    fg = final_g.reshape(1, D_MODEL)

    xs = [x_prompt, x_sample]
    cs = [c_prompt, c_sample]
    for i in range(DEPTH):
        gm = ln_mix_g[i].reshape(1, D_MODEL)
        gf = ln_ffn_g[i].reshape(1, D_MODEL)
        j = i // 2
        last = i == DEPTH - 1
        for tr in TRUNK_ORDER[i]:
            x = xs[tr]
            mod = _mod(cs[tr], w_mod[i], b_mod[i])
            if i % 2 == 0:
                x = _rglru_layer(x, mod, gm, rec_w_in[j], rec_conv_w[j], rec_conv_b[j], rec_ga_w[j],
                                 rec_ga_b[j], rec_gx_w[j], rec_gx_b[j], rec_lam[j], rec_w_out[j])
            else:
                x = _attention_layer(x, mod, gm, att_w_qkv[j], att_q_g[j], att_k_g[j], att_w_o[j])
            t_sc = SC_TOKENS[i][tr]
            routed = _peer_route_phase(x, mod, gf, peer_w_query[i], peer_sub_keys[i], t_sc)
            y_tc = _peer_retrieve_tc(routed, x.shape, mod, tables[i], fg, last, t_sc)
            po = _peer_sc(tables_sc[i], *routed["sc"])
            xs[tr] = _peer_retrieve_finish(routed, po, y_tc, x.shape, mod, fg, last)
    return (xs[0], xs[1])
```

```python
import dataclasses
import functools

import jax
import jax.numpy as jnp
from jax import lax
from jax.experimental import pallas as pl
from jax.experimental.pallas import tpu as pltpu
from jax.experimental.pallas import tpu_sc as plsc

F32 = jnp.float32
BF16 = jnp.bfloat16
I32 = jnp.int32

D_MODEL = 1024
DEPTH = 2
GRID_W = 64
EPS = 1e-6
RNN_WIDTH = D_MODEL
RNN_BLOCKS = 16
RNN_BLOCK_W = RNN_WIDTH // RNN_BLOCKS
CONV_W = 4
LRU_C = 8.0
N_HEADS = 16
N_KV_HEADS = 4
HEAD_DIM = D_MODEL // N_HEADS
GROUP = N_HEADS // N_KV_HEADS
AXIS_DIM = HEAD_DIM // 2
ROPE_THETA = 10000.0
N_KEYS = 128
N_EXPERTS = N_KEYS * N_KEYS
PEER_HEADS = 8
PEER_TOPK = 16
PEER_QDIM = 256
PEER_HALF = PEER_QDIM // 2
PEER_PICKS = PEER_HEADS * PEER_TOPK

LANES = 128
SUBLANES = 8
VMEM_LIMIT = 48 * 1024 * 1024

GATHER_PITCH = 12
GATHER_TOKENS = 64
GATHER_SLOTS = 4
ROWS_PER_EXPERT = D_MODEL // LANES


def _cparams(*sem):
    return pltpu.CompilerParams(dimension_semantics=sem, vmem_limit_bytes=VMEM_LIMIT)


def _gelu(x):
    return jax.nn.gelu(x)


def _norm_mod(x, gamma, scale, shift):
    ms = jnp.mean(x * x, axis=-1, keepdims=True)
    y = x * lax.rsqrt(ms + EPS) * gamma
    return y * (1.0 + scale) + shift


def _mod_body(c_ref, w_ref, b_ref, o_ref):
    c = c_ref[...]
    s = c * jax.nn.sigmoid(c)
    o_ref[...] = jnp.dot(s, w_ref[...], preferred_element_type=F32) + b_ref[...]


def _mod(c, w, b):
    nb, n = c.shape[0], w.shape[1]
    tn = 1536
    out = pl.pallas_call(
        _mod_body,
        grid=(n // tn,),
        in_specs=[
            pl.BlockSpec((nb, D_MODEL), lambda j: (0, 0)),
            pl.BlockSpec((D_MODEL, tn), lambda j: (0, j)),
            pl.BlockSpec((1, tn), lambda j: (0, j)),
        ],
        out_specs=pl.BlockSpec((nb, tn), lambda j: (0, j)),
        out_shape=jax.ShapeDtypeStruct((nb, n), F32),
        compiler_params=_cparams("arbitrary"),
        name="adaln_mod",
    )(c, w, b.reshape(1, n))
    return out.reshape(nb, 6, D_MODEL)


def _nmm_body(x_ref, mod_ref, g_ref, w_ref, o_ref, *, sh, sc):
    h = _norm_mod(x_ref[0], g_ref[...], mod_ref[0, sc:sc + 1, :], mod_ref[0, sh:sh + 1, :])
    o_ref[0] = jnp.dot(h.astype(BF16), w_ref[...], preferred_element_type=F32)


def _norm_mod_matmul(x, mod, gamma, w_bf16, sh, sc, tm=512):
    nb, s, _ = x.shape
    n = w_bf16.shape[1]
    return pl.pallas_call(
        functools.partial(_nmm_body, sh=sh, sc=sc),
        grid=(nb, s // tm),
        in_specs=[
            pl.BlockSpec((1, tm, D_MODEL), lambda b, i: (b, i, 0)),
            pl.BlockSpec((1, 6, D_MODEL), lambda b, i: (b, 0, 0)),
            pl.BlockSpec((1, D_MODEL), lambda b, i: (0, 0)),
            pl.BlockSpec((D_MODEL, n), lambda b, i: (0, 0)),
        ],
        out_specs=pl.BlockSpec((1, tm, n), lambda b, i: (b, i, 0)),
        out_shape=jax.ShapeDtypeStruct((nb, s, n), F32),
        compiler_params=_cparams("arbitrary", "arbitrary"),
        name="norm_mod_matmul",
    )(x, mod, gamma, w_bf16)


def _proj_res_body(m_ref, w_ref, x_ref, mod_ref, o_ref, *, gi):
    y = jnp.dot(m_ref[0].astype(BF16), w_ref[...], preferred_element_type=F32)
    o_ref[0] = x_ref[0] + mod_ref[0, gi:gi + 1, :] * y


def _proj_residual(m, w_bf16, x, mod, gi, tm=512):
    nb, s, k = m.shape
    return pl.pallas_call(
        functools.partial(_proj_res_body, gi=gi),
        grid=(nb, s // tm),
        in_specs=[
            pl.BlockSpec((1, tm, k), lambda b, i: (b, i, 0)),
            pl.BlockSpec((k, D_MODEL), lambda b, i: (0, 0)),
            pl.BlockSpec((1, tm, D_MODEL), lambda b, i: (b, i, 0)),
            pl.BlockSpec((1, 6, D_MODEL), lambda b, i: (b, 0, 0)),
        ],
        out_specs=pl.BlockSpec((1, tm, D_MODEL), lambda b, i: (b, i, 0)),
        out_shape=jax.ShapeDtypeStruct((nb, s, D_MODEL), F32),
        compiler_params=_cparams("arbitrary", "arbitrary"),
        name="proj_residual",
    )(m, w_bf16, x, mod)


def _scan_tile(a, b, carry, reverse):
    tt = a.shape[0]
    row = lax.broadcasted_iota(I32, (tt, LANES), 0) % SUBLANES
    for d in (1, 2, 4):
        if reverse:
            shift, keep = tt - d, row < SUBLANES - d
        else:
            shift, keep = d, row >= d
        ap = pltpu.roll(a, shift, 0)
        bp = pltpu.roll(b, shift, 0)
        b = jnp.where(keep, a * bp + b, b)
        a = jnp.where(keep, a * ap, a)
    groups = tt // SUBLANES
    hs = [None] * groups
    order = range(groups - 1, -1, -1) if reverse else range(groups)
    for g in order:
        lo = g * SUBLANES
        h = a[lo:lo + SUBLANES] * carry + b[lo:lo + SUBLANES]
        hs[g] = h
        last = h[0:1] if reverse else h[SUBLANES - 1:SUBLANES]
        carry = jnp.broadcast_to(last, (SUBLANES, LANES))
    return jnp.concatenate(hs, axis=0), carry


def _rglru_body(y_ref, x_ref, cw_ref, cb_ref, gaw_ref, gab_ref, gxw_ref, gxb_ref, nc_ref,
                o_ref, xp, hf, *, s, tt):
    nt = s // tt
    pad = SUBLANES
    zeros = jnp.zeros((pad, LANES), F32)
    xp[0:pad, :] = zeros
    xp[s + pad:s + 2 * pad, :] = zeros

    def copy_tile(i, c):
        r0 = pl.multiple_of(i * tt, tt)
        xp[pl.ds(pl.multiple_of(r0 + pad, SUBLANES), tt), :] = x_ref[0, pl.ds(r0, tt), :]
        return c

    lax.fori_loop(0, nt, copy_tile, 0)

    cw = cw_ref[...]
    cb = cb_ref[...]

    def conv_tile(r0):
        win = xp[pl.ds(r0, tt + 2 * pad), :]
        acc = cb + cw[2:3] * win[pad:pad + tt]
        for k in (0, 1, 3):
            shifted = pltpu.roll(win, (2 - k) % (tt + 2 * pad), 0)
            acc = acc + cw[k:k + 1] * shifted[pad:pad + tt]
        return acc

    def gates(d, xc):
        xb = xc.astype(BF16)
        r = jax.nn.sigmoid(jnp.dot(xb, gaw_ref[d, 0], preferred_element_type=F32) + gab_ref[d])
        ig = jax.nn.sigmoid(jnp.dot(xb, gxw_ref[d, 0], preferred_element_type=F32) + gxb_ref[d])
        a = jnp.exp(nc_ref[d] * r)
        b = jnp.sqrt(1.0 - a * a) * (ig * xc)
        return a, b

    carry0 = jnp.zeros((SUBLANES, LANES), F32)

    def fwd(i, carry):
        r0 = pl.multiple_of(i * tt, tt)
        a, b = gates(0, conv_tile(r0))
        h, carry = _scan_tile(a, b, carry, False)
        hf[pl.ds(r0, tt), :] = h
        return carry

    lax.fori_loop(0, nt, fwd, carry0)

    def bwd(ii, carry):
        r0 = pl.multiple_of((nt - 1 - ii) * tt, tt)
        a, b = gates(1, conv_tile(r0))
        h, carry = _scan_tile(a, b, carry, True)
        o_ref[0, pl.ds(r0, tt), :] = (hf[pl.ds(r0, tt), :] + h) * _gelu(y_ref[0, pl.ds(r0, tt), :])
        return carry

    lax.fori_loop(0, nt, bwd, carry0)


def _rglru_core(u, conv_w, conv_b, gaw, gab, gxw, gxb, negc, tt=256):
    nb, s, _ = u.shape
    ng = RNN_WIDTH // LANES
    return pl.pallas_call(
        functools.partial(_rglru_body, s=s, tt=tt),
        grid=(nb, ng),
        in_specs=[
            pl.BlockSpec((1, s, LANES), lambda b, j: (b, 0, j)),
            pl.BlockSpec((1, s, LANES), lambda b, j: (b, 0, ng + j)),
            pl.BlockSpec((CONV_W, LANES), lambda b, j: (0, j)),
            pl.BlockSpec((1, LANES), lambda b, j: (0, j)),
            pl.BlockSpec((2, 1, LANES, LANES), lambda b, j: (0, j, 0, 0)),
            pl.BlockSpec((2, 1, LANES), lambda b, j: (0, 0, j)),
            pl.BlockSpec((2, 1, LANES, LANES), lambda b, j: (0, j, 0, 0)),
            pl.BlockSpec((2, 1, LANES), lambda b, j: (0, 0, j)),
            pl.BlockSpec((2, 1, LANES), lambda b, j: (0, 0, j)),
        ],
        out_specs=pl.BlockSpec((1, s, LANES), lambda b, j: (b, 0, j)),
        out_shape=jax.ShapeDtypeStruct((nb, s, RNN_WIDTH), F32),
        scratch_shapes=[pltpu.VMEM((s + 2 * SUBLANES, LANES), F32), pltpu.VMEM((s, LANES), F32)],
        compiler_params=_cparams("arbitrary", "arbitrary"),
        name="rglru_core",
    )(u, u, conv_w, conv_b, gaw, gab, gxw, gxb, negc)


def _block_diag_groups(w):
    ng = RNN_WIDTH // LANES
    w = w.reshape(ng, 2, RNN_BLOCK_W, RNN_BLOCK_W)
    out = jnp.zeros((ng, LANES, LANES), w.dtype)
    out = out.at[:, :RNN_BLOCK_W, :RNN_BLOCK_W].set(w[:, 0])
    out = out.at[:, RNN_BLOCK_W:, RNN_BLOCK_W:].set(w[:, 1])
    return out


def _rglru_layer(x, mod, gamma, w_in, conv_w, conv_b, ga_w, ga_b, gx_w, gx_b, lam, w_out):
    u = _norm_mod_matmul(x, mod, gamma, w_in.astype(BF16), sh=0, sc=1)
    gaw = jnp.stack([_block_diag_groups(ga_w[d]) for d in range(2)]).astype(BF16)
    gxw = jnp.stack([_block_diag_groups(gx_w[d]) for d in range(2)]).astype(BF16)
    gab = ga_b.reshape(2, 1, RNN_WIDTH)
    gxb = gx_b.reshape(2, 1, RNN_WIDTH)
    negc = (-LRU_C * jax.nn.softplus(-lam)).reshape(2, 1, RNN_WIDTH)
    m = _rglru_core(u, conv_w, conv_b.reshape(1, RNN_WIDTH), gaw, gab, gxw, gxb, negc)
    return _proj_residual(m, w_out.astype(BF16), x, mod, gi=2)


def _seg_mean(x2, seg_ref):
    hi = x2.astype(BF16)
    lo = (x2 - hi.astype(F32)).astype(BF16)
    return (jnp.dot(hi, seg_ref[...], preferred_element_type=F32)
            + jnp.dot(lo, seg_ref[...], preferred_element_type=F32))


def _rope(x, cos, sin, lane_lo):
    outs = []
    for j in range(x.shape[1] // LANES):
        xt = x[:, j * LANES:(j + 1) * LANES]
        rot = jnp.where(lane_lo, pltpu.roll(xt, LANES - AXIS_DIM // 2, 1), pltpu.roll(xt, AXIS_DIM // 2, 1))
        outs.append(xt * cos + rot * sin)
    return jnp.concatenate(outs, axis=1) if len(outs) > 1 else outs[0]


def _qkv_body(x_ref, mod_ref, g_ref, w_ref, segq_ref, segk_ref, qg_ref, kg_ref, cosq_ref, sinq_ref,
              cosk_ref, sink_ref, q_ref, kt_ref, v_ref, *, tm):
    h = _norm_mod(x_ref[0], g_ref[...], mod_ref[0, 1:2, :], mod_ref[0, 0:1, :])
    qkv = jnp.dot(h.astype(BF16), w_ref[...], preferred_element_type=F32)
    nq = N_HEADS * HEAD_DIM
    nk = N_KV_HEADS * HEAD_DIM
    q = qkv[:, :nq]
    k = qkv[:, nq:nq + nk]
    v = qkv[:, nq + nk:]
    lane = lax.broadcasted_iota(I32, (tm, LANES), 1)
    lane_lo = (lane % AXIS_DIM) < (AXIS_DIM // 2)
    q = q * lax.rsqrt(_seg_mean(q * q, segq_ref) + EPS) * qg_ref[...]
    k = k * lax.rsqrt(_seg_mean(k * k, segk_ref) + EPS) * kg_ref[...]
    q = _rope(q, cosq_ref[...], sinq_ref[...], lane_lo)
    k = _rope(k, cosk_ref[...], sink_ref[...], lane_lo)
    q_ref[0] = q.astype(BF16)
    kt = k.T.astype(BF16)
    for g in range(N_KV_HEADS):
        kt_ref[0, g] = kt[g * HEAD_DIM:(g + 1) * HEAD_DIM, :]
        v_ref[0, g] = v[:, g * HEAD_DIM:(g + 1) * HEAD_DIM].astype(BF16)


def _rope_tables(s):
    rows = s // GRID_W
    row = jnp.repeat(jnp.arange(rows, dtype=F32), GRID_W)
    col = jnp.tile(jnp.arange(GRID_W, dtype=F32), rows)
    inv = ROPE_THETA ** (-jnp.arange(0, AXIS_DIM, 2, dtype=F32) / AXIS_DIM)
    ar = row[:, None] * inv
    ac = col[:, None] * inv
    cos = jnp.concatenate([jnp.cos(ar), jnp.cos(ar), jnp.cos(ac), jnp.cos(ac)], axis=1)
    sin = jnp.concatenate([-jnp.sin(ar), jnp.sin(ar), -jnp.sin(ac), jnp.sin(ac)], axis=1)
    return jnp.tile(cos, (1, LANES // HEAD_DIM)), jnp.tile(sin, (1, LANES // HEAD_DIM))


def _attn_body(q_ref, kt_ref, v_ref, o_ref):
    kt = kt_ref[0, 0]
    v = v_ref[0, 0]
    outs = []
    for hh in range(GROUP):
        qh = q_ref[0, :, hh * HEAD_DIM:(hh + 1) * HEAD_DIM]
        sc = jnp.dot(qh, kt, preferred_element_type=F32)
        m = jnp.max(sc, axis=-1, keepdims=True)
        p = jnp.exp(sc - m)
        l = jnp.sum(p, axis=-1, keepdims=True)
        o = jnp.dot(p.astype(BF16), v, preferred_element_type=F32)
        outs.append(o / l)
    o_ref[0] = jnp.concatenate(outs, axis=1).astype(BF16)


def _attention_layer(x, mod, gamma, w_qkv, q_g, k_g, w_o, tm=256, tq=128):
    nb, s, _ = x.shape
    nq = N_HEADS * HEAD_DIM
    nk = N_KV_HEADS * HEAD_DIM
    seg = jnp.kron(jnp.eye(N_HEADS, dtype=F32), jnp.full((HEAD_DIM, HEAD_DIM), 1.0 / HEAD_DIM, F32)).astype(BF16)
    segk = seg[:nk, :nk]
    cos, sin = _rope_tables(s)
    scale = HEAD_DIM ** -0.5
    q, kt, v = pl.pallas_call(
        functools.partial(_qkv_body, tm=tm),
        grid=(nb, s // tm),
        in_specs=[
            pl.BlockSpec((1, tm, D_MODEL), lambda b, i: (b, i, 0)),
            pl.BlockSpec((1, 6, D_MODEL), lambda b, i: (b, 0, 0)),
            pl.BlockSpec((1, D_MODEL), lambda b, i: (0, 0)),
            pl.BlockSpec((D_MODEL, nq + 2 * nk), lambda b, i: (0, 0)),
            pl.BlockSpec((nq, nq), lambda b, i: (0, 0)),
            pl.BlockSpec((nk, nk), lambda b, i: (0, 0)),
            pl.BlockSpec((1, nq), lambda b, i: (0, 0)),
            pl.BlockSpec((1, nk), lambda b, i: (0, 0)),
            pl.BlockSpec((tm, LANES), lambda b, i: (i, 0)),
            pl.BlockSpec((tm, LANES), lambda b, i: (i, 0)),
            pl.BlockSpec((tm, LANES), lambda b, i: (i, 0)),
            pl.BlockSpec((tm, LANES), lambda b, i: (i, 0)),
        ],
        out_specs=[
            pl.BlockSpec((1, tm, nq), lambda b, i: (b, i, 0)),
            pl.BlockSpec((1, N_KV_HEADS, HEAD_DIM, tm), lambda b, i: (b, 0, 0, i)),
            pl.BlockSpec((1, N_KV_HEADS, tm, HEAD_DIM), lambda b, i: (b, 0, i, 0)),
        ],
        out_shape=[
            jax.ShapeDtypeStruct((nb, s, nq), BF16),
            jax.ShapeDtypeStruct((nb, N_KV_HEADS, HEAD_DIM, s), BF16),
            jax.ShapeDtypeStruct((nb, N_KV_HEADS, s, HEAD_DIM), BF16),
        ],
        compiler_params=_cparams("arbitrary", "arbitrary"),
        name="qkv_rope",
    )(x, mod, gamma, w_qkv.astype(BF16), seg, segk,
      jnp.tile(q_g, N_HEADS).reshape(1, nq), jnp.tile(k_g, N_KV_HEADS).reshape(1, nk),
      cos * scale, sin * scale, cos, sin)

    gw = GROUP * HEAD_DIM
    o = pl.pallas_call(
        _attn_body,
        grid=(nb, N_KV_HEADS, s // tq),
        in_specs=[
            pl.BlockSpec((1, tq, gw), lambda b, g, i: (b, i, g)),
            pl.BlockSpec((1, 1, HEAD_DIM, s), lambda b, g, i: (b, g, 0, 0)),
            pl.BlockSpec((1, 1, s, HEAD_DIM), lambda b, g, i: (b, g, 0, 0)),
        ],
        out_specs=pl.BlockSpec((1, tq, gw), lambda b, g, i: (b, i, g)),
        out_shape=jax.ShapeDtypeStruct((nb, s, nq), BF16),
        compiler_params=_cparams("arbitrary", "arbitrary", "arbitrary"),
        name="attention",
    )(q, kt, v)
    return _proj_residual(o, w_o.astype(BF16), x, mod, gi=2)


def _topk_rows(s, ids, k, id_bound):
    vals, picks = [], []
    for _ in range(k):
        m = jnp.max(s, axis=0, keepdims=True)
        i = jnp.min(jnp.where(s == m, ids, id_bound), axis=0, keepdims=True)
        vals.append(m)
        picks.append(i)
        s = jnp.where(ids == i, -jnp.inf, s)
    return jnp.concatenate(vals, axis=0), jnp.concatenate(picks, axis=0)


ROUTE_HEADS_PER_STEP = 8


def _route_body(x_ref, mod_ref, g_ref, wq_ref, keys_ref, h_ref, e_ref, gate_ref, hb, *, tm):
    hd = pl.program_id(2)

    @pl.when(hd == 0)
    def _():
        h = _norm_mod(x_ref[0], g_ref[...], mod_ref[0, 4:5, :], mod_ref[0, 3:4, :])
        h_ref[0] = h
        hb[...] = h.astype(BF16)

    q = jnp.dot(hb[...], wq_ref[...], preferred_element_type=F32)
    for hh in range(ROUTE_HEADS_PER_STEP):
        gate, eid = _route_head(q[:, hh * PEER_QDIM:(hh + 1) * PEER_QDIM], keys_ref, tm)
        gate_ref[0, hh] = gate
        e_ref[0, hh] = eid


def _route_head(q, keys_ref, tm):
    key_ids = lax.broadcasted_iota(I32, (N_KEYS, tm), 0)
    tops = []
    for p in range(2):
        qp = q[:, p * PEER_HALF:(p + 1) * PEER_HALF].astype(BF16)
        st = lax.dot_general(keys_ref[p], qp, (((1,), (1,)), ((), ())), preferred_element_type=F32)
        tops.append(_topk_rows(st, key_ids, PEER_TOPK, N_KEYS))
    (v0, i0), (v1, i1) = tops

    r16 = lax.broadcasted_iota(I32, (PEER_TOPK, tm), 0)
    cand, fid, eid = [], [], []
    for a in range(4):
        cand.append(v0[a:a + 1] + v1)
        fid.append(a * PEER_TOPK + r16)
        eid.append(i0[a:a + 1] * N_KEYS + i1)
    for b in range(3):
        cand.append(jnp.where(r16 >= 4, v0 + v1[b:b + 1], -jnp.inf))
        fid.append(r16 * PEER_TOPK + b)
        eid.append(i0 * N_KEYS + i1[b:b + 1])
    cand = jnp.concatenate(cand, axis=0)
    fid = jnp.concatenate(fid, axis=0)
    eid = jnp.concatenate(eid, axis=0)

    best, chosen = [], []
    for _ in range(PEER_TOPK):
        m = jnp.max(cand, axis=0, keepdims=True)
        f = jnp.min(jnp.where(cand == m, fid, PEER_TOPK * PEER_TOPK), axis=0, keepdims=True)
        hit = fid == f
        best.append(m)
        chosen.append(jnp.max(jnp.where(hit, eid, -1), axis=0, keepdims=True))
        cand = jnp.where(hit, -jnp.inf, cand)
    best = jnp.concatenate(best, axis=0)
    ex = jnp.exp(best - best[0:1])
    return ex / jnp.sum(ex, axis=0, keepdims=True), jnp.concatenate(chosen, axis=0)


def _peer_route(x, mod, gamma, w_query, sub_keys, tm=128):
    nb, s, _ = x.shape
    hps = ROUTE_HEADS_PER_STEP
    return pl.pallas_call(
        functools.partial(_route_body, tm=tm),
        grid=(nb, s // tm, PEER_HEADS // hps),
        in_specs=[
            pl.BlockSpec((1, tm, D_MODEL), lambda b, i, h: (b, i, 0)),
            pl.BlockSpec((1, 6, D_MODEL), lambda b, i, h: (b, 0, 0)),
            pl.BlockSpec((1, D_MODEL), lambda b, i, h: (0, 0)),
            pl.BlockSpec((D_MODEL, hps * PEER_QDIM), lambda b, i, h: (0, h)),
            pl.BlockSpec((2, N_KEYS, PEER_HALF), lambda b, i, h: (0, 0, 0)),
        ],
        out_specs=[
            pl.BlockSpec((1, tm, D_MODEL), lambda b, i, h: (b, i, 0)),
            pl.BlockSpec((1, hps, PEER_TOPK, tm), lambda b, i, h: (b, h, 0, i)),
            pl.BlockSpec((1, hps, PEER_TOPK, tm), lambda b, i, h: (b, h, 0, i)),
        ],
        out_shape=[
            jax.ShapeDtypeStruct((nb, s, D_MODEL), F32),
            jax.ShapeDtypeStruct((nb, PEER_HEADS, PEER_TOPK, s), I32),
            jax.ShapeDtypeStruct((nb, PEER_HEADS, PEER_TOPK, s), F32),
        ],
        scratch_shapes=[pltpu.VMEM((tm, D_MODEL), BF16)],
        compiler_params=_cparams("arbitrary", "arbitrary", "arbitrary"),
        name="peer_route",
    )(x, mod, gamma, w_query.astype(BF16), sub_keys.astype(BF16))


def _pack_body(u_ref, v_ref, o_ref):
    o_ref[...] = pltpu.pack_elementwise([u_ref[...], v_ref[...]], packed_dtype=BF16).astype(I32)


def _pack_table(u, v, te=512):
    ne = u.shape[0]
    return pl.pallas_call(
        _pack_body,
        grid=(ne // te,),
        in_specs=[pl.BlockSpec((te, D_MODEL), lambda i: (i, 0))] * 2,
        out_specs=pl.BlockSpec((te, D_MODEL), lambda i: (i, 0)),
        out_shape=jax.ShapeDtypeStruct((ne, D_MODEL), I32),
        compiler_params=_cparams("arbitrary"),
        name="peer_pack",
    )(u, v)


def _gather_body(eid_ref, h_ref, g_ref, tab_ref, x_ref, mod_ref, fg_ref, o_ref, buf, sem, po,
                 *, tb, final_norm):
    ne = PEER_PICKS
    rows = ROWS_PER_EXPERT

    ns = GATHER_SLOTS
    ahead = ns - 1

    def issue(t, slot, lo, hi):
        for k in range(lo, hi):
            src = tab_ref.at[pl.ds(pl.multiple_of(eid_ref[t, k] * rows, rows), rows), :]
            dst = buf.at[slot, pl.ds(GATHER_PITCH * k, rows), :]
            pltpu.make_async_copy(src, dst, sem.at[slot]).start(priority=k % 2)

    def wait(slot):
        pltpu.make_async_copy(tab_ref.at[pl.ds(0, ne * rows), :], buf.at[slot, pl.ds(0, ne * rows), :],
                              sem.at[slot]).wait()

    for q in range(ahead):
        issue(q, q, 0, ne)
    early = 4
    halves = 2
    per_half = ne // halves
    even = lax.broadcasted_iota(I32, (1, 2 * LANES), 1) % 2 == 0
    row_id = lax.broadcasted_iota(I32, (SUBLANES, LANES), 0)

    def chunk(slot, c, first=0, count=ne):
        words = buf[slot, pl.ds(first * GATHER_PITCH + c, count, stride=GATHER_PITCH), :]
        return pltpu.bitcast(words, BF16)

    def score_half(t8, j, hf, c, acc):
        xrow = h_ref[pl.ds(t8, SUBLANES), c * LANES:(c + 1) * LANES][j:j + 1]
        return acc + chunk(j % ns, c, hf * per_half, per_half).astype(F32) * xrow

    def lane_sums(acc):
        return jnp.sum(acc.T, axis=0, keepdims=True)

    def coefficients(parts, t8, j):
        s = jnp.concatenate(parts, axis=1)
        g = g_ref[pl.ds(t8, SUBLANES), :][j:j + 1]
        coef = pltpu.roll(jnp.where(even, _gelu(s) * g, 0.0), 1, 1)
        return jnp.broadcast_to(coef, (SUBLANES, 2 * LANES))

    wait(0)
    parts = []
    for hf in range(halves):
        a = jnp.zeros((2 * per_half, LANES), F32)
        for c in range(rows):
            a = score_half(0, 0, hf, c, a)
        parts.append(lane_sums(a))
    coef0 = coefficients(parts, 0, 0)

    def body(it, coef):
        t8 = pl.multiple_of(it * SUBLANES, SUBLANES)
        t8_next = pl.multiple_of(jnp.minimum(t8 + SUBLANES, tb - SUBLANES), SUBLANES)
        acc = [None] * rows
        for j in range(SUBLANES):
            jn = (j + 1) % SUBLANES
            t8n = t8 if j + 1 < SUBLANES else t8_next
            nxt = jnp.minimum(t8 + j + ahead, tb - 1)
            nslot = (j + ahead) % ns
            wait((j + 1) % ns)
            coef_b = coef.astype(BF16)
            step = 0
            parts = []
            for hf in range(halves):
                a = jnp.zeros((2 * per_half, LANES), F32)
                for c in range(rows):
                    issue(nxt, nslot, step * early, (step + 1) * early)
                    step += 1
                    a = score_half(t8n, jn, hf, c, a)
                parts.append(lane_sums(a))
            for c in range(rows):
                issue(nxt, nslot, step * early, (step + 1) * early)
                step += 1
                o = jnp.dot(coef_b, chunk(j % ns, c), preferred_element_type=F32)
                acc[c] = o if j == 0 else jnp.where(row_id == j, o, acc[c])
            issue(nxt, nslot, step * early, ne)
            coef = coefficients(parts, t8n, jn)
        for c in range(rows):
            po[pl.ds(t8, SUBLANES), c * LANES:(c + 1) * LANES] = acc[c]
        return coef

    lax.fori_loop(0, tb // SUBLANES, body, coef0)
    for q in range(1, ahead):
        wait((tb + q) % ns)
    y = x_ref[...] + mod_ref[0, 5:6, :] * po[...]
    if final_norm:
        y = y * lax.rsqrt(jnp.mean(y * y, axis=-1, keepdims=True) + EPS) * fg_ref[...]
    o_ref[...] = y


def _peer_gather(eid, h, g, table, x, mod, final_g, seq_len, final_norm, tok_offset, tb=GATHER_TOKENS):
    t = eid.shape[0]
    off = tok_offset // tb
    return pl.pallas_call(
        functools.partial(_gather_body, tb=tb, final_norm=final_norm),
        grid=(t // tb,),
        in_specs=[
            pl.BlockSpec((tb, PEER_PICKS), lambda i: (i, 0), memory_space=pltpu.SMEM),
            pl.BlockSpec((tb, D_MODEL), lambda i: (off + i, 0)),
            pl.BlockSpec((tb, 2 * PEER_PICKS), lambda i: (i, 0)),
            pl.BlockSpec(memory_space=pl.ANY),
            pl.BlockSpec((tb, D_MODEL), lambda i: (off + i, 0)),
            pl.BlockSpec((1, 6, D_MODEL), lambda i: ((tok_offset + i * tb) // seq_len, 0, 0)),
            pl.BlockSpec((1, D_MODEL), lambda i: (0, 0)),
        ],
        out_specs=pl.BlockSpec((tb, D_MODEL), lambda i: (i, 0)),
        out_shape=jax.ShapeDtypeStruct((t, D_MODEL), F32),
        scratch_shapes=[
            pltpu.VMEM((GATHER_SLOTS, PEER_PICKS * GATHER_PITCH, LANES), I32),
            pltpu.SemaphoreType.DMA((GATHER_SLOTS,)),
            pltpu.VMEM((tb, D_MODEL), F32),
        ],
        compiler_params=_cparams("arbitrary"),
        name="peer_gather",
    )(eid, h, g, table, x, mod, final_g)


SC_LANES = 16
SC_WORKERS = 32
SC_GROUP = 8
SC_TOKENS = ((8192, 14336), (9216, 9216))
SC_COST_SCALE = 1
TRUNK_ORDER = ((0, 1), (0, 1))
SC_CHUNKS = D_MODEL // SC_LANES


def _sc_body(tab_hbm, eid_hbm, g_hbm, h_hbm, out_hbm, idx_v, g_v, x_v, o_v, rows0, rows1, rows2, rows3,
             sem0, sem1, sem2, sem3, *, tpw):
    nl = SC_LANES
    wid = lax.axis_index("s") * 2 + lax.axis_index("c")
    lane = lax.iota(I32, nl)

    def permute(x, idx):
        return jnp.take_along_axis(x, idx, axis=0, mode="promise_in_bounds")
    rows = (rows0, rows1, rows2, rows3)
    sems = (sem0, sem1, sem2, sem3)
    ns = len(rows)
    units = SC_GROUP * PEER_HEADS

    def gather(tok, hd, slot):
        return pltpu.make_async_copy(tab_hbm.at[idx_v.at[tok, hd]], rows[slot], sems[slot])

    def compute(tok, hd, slot, t):
        rv = rows[slot]

        @pl.when(hd == 0)
        def _():
            def zero(c, cc):
                o_v[pl.ds(c * nl, nl)] = jnp.zeros((nl,), F32)
                return cc
            lax.fori_loop(0, SC_CHUNKS, zero, 0)

        def score(c, accs):
            x = x_v[tok, pl.ds(c * nl, nl)]
            out = []
            for kk in range(PEER_TOPK):
                u = plsc.bitcast(lax.shift_left(rv[kk, pl.ds(c * nl, nl)], 16), F32)
                out.append(accs[kk] + u * x)
            return tuple(out)

        accs = lax.fori_loop(0, SC_CHUNKS, score, tuple(jnp.zeros((nl,), F32) for _ in range(PEER_TOPK)))
        vecs = list(accs)
        d = 1
        while len(vecs) > 1:
            partner = lane ^ d
            take_lo = (lane & d) == 0
            nxt_vecs = []
            for i in range(0, len(vecs), 2):
                a = vecs[i] + permute(vecs[i], partner)
                b = vecs[i + 1] + permute(vecs[i + 1], partner)
                nxt_vecs.append(jnp.where(take_lo, a, b))
            vecs = nxt_vecs
            d *= 2
        s = vecs[0]
        z = 0.7978845608028654 * (s + 0.044715 * s * s * s)
        act = s * (1.0 - 1.0 / (jnp.exp(2.0 * z) + 1.0))
        coef = act * g_v[tok, hd, :]
        coefs = [permute(coef, jnp.full((nl,), kk, I32)) for kk in range(PEER_TOPK)]

        def combine(c, cc):
            parts = []
            for q in range(4):
                acc = None
                for kk in range(q * PEER_TOPK // 4, (q + 1) * PEER_TOPK // 4):
                    v = plsc.bitcast(lax.bitwise_and(rv[kk, pl.ds(c * nl, nl)], jnp.int32(-65536)), F32)
                    term = coefs[kk] * v
                    acc = term if acc is None else acc + term
                parts.append(acc)
            o_v[pl.ds(c * nl, nl)] = o_v[pl.ds(c * nl, nl)] + ((parts[0] + parts[1]) + (parts[2] + parts[3]))
            return cc

        lax.fori_loop(0, SC_CHUNKS, combine, 0)

        @pl.when(hd == PEER_HEADS - 1)
        def _():
            pltpu.sync_copy(o_v, out_hbm.at[t])

    def group(gi, carry):
        t0 = wid * tpw + gi * SC_GROUP
        pltpu.sync_copy(eid_hbm.at[pl.ds(t0, SC_GROUP)], idx_v)
        pltpu.sync_copy(g_hbm.at[pl.ds(t0, SC_GROUP)], g_v)
        pltpu.sync_copy(h_hbm.at[pl.ds(t0, SC_GROUP)], x_v)
        for q in range(ns - 1):
            gather(0, q, q).start()

        def ring(ri, cc):
            j0 = ri * ns
            tok, hd0 = j0 // PEER_HEADS, j0 % PEER_HEADS
            for q in range(ns):
                jn = j0 + q + ns - 1

                @pl.when(jn < units)
                def _():
                    gather(jn // PEER_HEADS, jn % PEER_HEADS, (q + ns - 1) % ns).start()

                gather(tok, hd0 + q, q).wait()
                compute(tok, hd0 + q, q, t0 + tok)
            return cc

        lax.fori_loop(0, units // ns, ring, 0)
        return carry

    lax.fori_loop(0, tpw // SC_GROUP, group, 0)


def _peer_sc(table, eid, gate, h):
    t = eid.shape[0]
    tpw = t // SC_WORKERS
    cp = pltpu.CompilerParams()
    if "needs_layout_passes" in pltpu.CompilerParams.__dataclass_fields__:
        cp = dataclasses.replace(cp, needs_layout_passes=False)
    run = pl.kernel(
        functools.partial(_sc_body, tpw=tpw),
        out_type=jax.ShapeDtypeStruct((t, D_MODEL), F32),
        mesh=plsc.VectorSubcoreMesh(core_axis_name="c", subcore_axis_name="s"),
        scratch_types=[
            pltpu.VMEM((SC_GROUP, PEER_HEADS, PEER_TOPK), I32),
            pltpu.VMEM((SC_GROUP, PEER_HEADS, PEER_TOPK), F32),
            pltpu.VMEM((SC_GROUP, D_MODEL), F32),
            pltpu.VMEM((D_MODEL,), F32),
            pltpu.VMEM((PEER_TOPK, D_MODEL), I32),
            pltpu.VMEM((PEER_TOPK, D_MODEL), I32),
            pltpu.VMEM((PEER_TOPK, D_MODEL), I32),
            pltpu.VMEM((PEER_TOPK, D_MODEL), I32),
            pltpu.SemaphoreType.DMA,
            pltpu.SemaphoreType.DMA,
            pltpu.SemaphoreType.DMA,
            pltpu.SemaphoreType.DMA,
        ],
        compiler_params=cp,
        cost_estimate=pl.CostEstimate(
            flops=SC_COST_SCALE * 4 * t * PEER_PICKS * D_MODEL,
            transcendentals=t * PEER_PICKS,
            bytes_accessed=SC_COST_SCALE * (t * PEER_PICKS * D_MODEL * 4 + 2 * t * D_MODEL * 4),
        ),
        name="peer_sc",
    )
    return run(table, eid.reshape(t, PEER_HEADS, PEER_TOPK), gate.reshape(t, PEER_HEADS, PEER_TOPK), h)


def _finish_body(x_ref, po_ref, mod_ref, fg_ref, o_ref, *, final_norm):
    y = x_ref[...] + mod_ref[0, 5:6, :] * po_ref[...]
    if final_norm:
        y = y * lax.rsqrt(jnp.mean(y * y, axis=-1, keepdims=True) + EPS) * fg_ref[...]
    o_ref[...] = y


def _peer_finish(x, po, mod, final_g, seq_len, final_norm, tm=512):
    t = po.shape[0]
    return pl.pallas_call(
        functools.partial(_finish_body, final_norm=final_norm),
        grid=(t // tm,),
        in_specs=[
            pl.BlockSpec((tm, D_MODEL), lambda i: (i, 0)),
            pl.BlockSpec((tm, D_MODEL), lambda i: (i, 0)),
            pl.BlockSpec((1, 6, D_MODEL), lambda i: ((i * tm) // seq_len, 0, 0)),
            pl.BlockSpec((1, D_MODEL), lambda i: (0, 0)),
        ],
        out_specs=pl.BlockSpec((tm, D_MODEL), lambda i: (i, 0)),
        out_shape=jax.ShapeDtypeStruct((t, D_MODEL), F32),
        compiler_params=_cparams("arbitrary"),
        name="peer_finish",
    )(x, po, mod, final_g)


def _peer_route_phase(x, mod, gamma, w_query, sub_keys, t_sc):
    nb, s, _ = x.shape
    t = nb * s
    h, e_t, g_t = _peer_route(x, mod, gamma, w_query, sub_keys)
    eid = e_t.transpose(0, 3, 1, 2).reshape(t, PEER_PICKS)
    gate = g_t.transpose(0, 3, 1, 2).reshape(t, PEER_PICKS)
    hf = h.reshape(t, D_MODEL)
    xf = x.reshape(t, D_MODEL)
    gate_tc = gate[t_sc:]
    gate_tc = jnp.stack([gate_tc, jnp.zeros_like(gate_tc)], axis=-1).reshape(t - t_sc, 2 * PEER_PICKS)
    return dict(sc=(eid[:t_sc], gate[:t_sc], hf), x_sc=xf, tc=(eid[t_sc:], hf, gate_tc), x_tc=xf)


def _peer_retrieve_tc(ops, shape, mod, table, final_g, final_norm, t_sc):
    return _peer_gather(*ops["tc"], table, ops["x_tc"], mod, final_g, shape[1], final_norm, t_sc)


def _peer_retrieve_finish(ops, po, y_tc, shape, mod, final_g, final_norm):
    nb, s, _ = shape
    y_sc = _peer_finish(ops["x_sc"], po, mod, final_g, s, final_norm)
    return jnp.concatenate([y_sc, y_tc], axis=0).reshape(nb, s, D_MODEL)


def kernel(x_prompt, x_sample, c_prompt, c_sample, ln_mix_g, ln_ffn_g, w_mod, b_mod, rec_w_in, rec_conv_w,
           rec_conv_b, rec_ga_w, rec_ga_b, rec_gx_w, rec_gx_b, rec_lam, rec_w_out, att_w_qkv, att_q_g,
           att_k_g, att_w_o, peer_w_query, peer_sub_keys, peer_u, peer_v, final_g):
    tables_sc = [_pack_table(peer_u[i], peer_v[i]) for i in range(DEPTH)]
    tables = [tb.reshape(N_EXPERTS * ROWS_PER_EXPERT, LANES) for tb in tables_sc]
    fg = final_g.reshape(1, D_MODEL)

    xs = [x_prompt, x_sample]
    cs = [c_prompt, c_sample]
    for i in range(DEPTH):
        gm = ln_mix_g[i].reshape(1, D_MODEL)
        gf = ln_ffn_g[i].reshape(1, D_MODEL)
        j = i // 2
        last = i == DEPTH - 1
        for tr in TRUNK_ORDER[i]:
            x = xs[tr]
            mod = _mod(cs[tr], w_mod[i], b_mod[i])
            if i % 2 == 0:
                x = _rglru_layer(x, mod, gm, rec_w_in[j], rec_conv_w[j], rec_conv_b[j], rec_ga_w[j],
                                 rec_ga_b[j], rec_gx_w[j], rec_gx_b[j], rec_lam[j], rec_w_out[j])
            else:
                x = _attention_layer(x, mod, gm, att_w_qkv[j], att_q_g[j], att_k_g[j], att_w_o[j])
            t_sc = SC_TOKENS[i][tr]
            routed = _peer_route_phase(x, mod, gf, peer_w_query[i], peer_sub_keys[i], t_sc)
            y_tc = _peer_retrieve_tc(routed, x.shape, mod, tables[i], fg, last, t_sc)
            po = _peer_sc(tables_sc[i], *routed["sc"])
            xs[tr] = _peer_retrieve_finish(routed, po, y_tc, x.shape, mod, fg, last)
    return (xs[0], xs[1])
```

```python
import dataclasses
import functools

import jax
import jax.numpy as jnp
from jax import lax
from jax.experimental import pallas as pl
from jax.experimental.pallas import tpu as pltpu
from jax.experimental.pallas import tpu_sc as plsc

F32 = jnp.float32
BF16 = jnp.bfloat16
I32 = jnp.int32

D_MODEL = 1024
DEPTH = 2
GRID_W = 64
EPS = 1e-6
RNN_WIDTH = D_MODEL
RNN_BLOCKS = 16
RNN_BLOCK_W = RNN_WIDTH // RNN_BLOCKS
CONV_W = 4
LRU_C = 8.0
N_HEADS = 16
N_KV_HEADS = 4
HEAD_DIM = D_MODEL // N_HEADS
GROUP = N_HEADS // N_KV_HEADS
AXIS_DIM = HEAD_DIM // 2
ROPE_THETA = 10000.0
N_KEYS = 128
N_EXPERTS = N_KEYS * N_KEYS
PEER_HEADS = 8
PEER_TOPK = 16
PEER_QDIM = 256
PEER_HALF = PEER_QDIM // 2
PEER_PICKS = PEER_HEADS * PEER_TOPK

LANES = 128
SUBLANES = 8
VMEM_LIMIT = 48 * 1024 * 1024

GATHER_PITCH = 12
GATHER_TOKENS = 128
GATHER_SLOTS = 4
ROWS_PER_EXPERT = D_MODEL // LANES


def _cparams(*sem):
    return pltpu.CompilerParams(dimension_semantics=sem, vmem_limit_bytes=VMEM_LIMIT)


def _gelu(x):
    return jax.nn.gelu(x)


def _norm_mod(x, gamma, scale, shift):
    ms = jnp.mean(x * x, axis=-1, keepdims=True)
    y = x * lax.rsqrt(ms + EPS) * gamma
    return y * (1.0 + scale) + shift


def _mod_body(c_ref, w_ref, b_ref, o_ref):
    c = c_ref[...]
    s = c * jax.nn.sigmoid(c)
    o_ref[...] = jnp.dot(s, w_ref[...], preferred_element_type=F32) + b_ref[...]


def _mod(c, w, b):
    nb, n = c.shape[0], w.shape[1]
    tn = 1536
    out = pl.pallas_call(
        _mod_body,
        grid=(n // tn,),
        in_specs=[
            pl.BlockSpec((nb, D_MODEL), lambda j: (0, 0)),
            pl.BlockSpec((D_MODEL, tn), lambda j: (0, j)),
            pl.BlockSpec((1, tn), lambda j: (0, j)),
        ],
        out_specs=pl.BlockSpec((nb, tn), lambda j: (0, j)),
        out_shape=jax.ShapeDtypeStruct((nb, n), F32),
        compiler_params=_cparams("arbitrary"),
        name="adaln_mod",
    )(c, w, b.reshape(1, n))
    return out.reshape(nb, 6, D_MODEL)


def _nmm_body(x_ref, mod_ref, g_ref, w_ref, o_ref, *, sh, sc):
    h = _norm_mod(x_ref[0], g_ref[...], mod_ref[0, sc:sc + 1, :], mod_ref[0, sh:sh + 1, :])
    o_ref[0] = jnp.dot(h.astype(BF16), w_ref[...], preferred_element_type=F32)


def _norm_mod_matmul(x, mod, gamma, w_bf16, sh, sc, tm=512):
    nb, s, _ = x.shape
    n = w_bf16.shape[1]
    return pl.pallas_call(
        functools.partial(_nmm_body, sh=sh, sc=sc),
        grid=(nb, s // tm),
        in_specs=[
            pl.BlockSpec((1, tm, D_MODEL), lambda b, i: (b, i, 0)),
            pl.BlockSpec((1, 6, D_MODEL), lambda b, i: (b, 0, 0)),
            pl.BlockSpec((1, D_MODEL), lambda b, i: (0, 0)),
            pl.BlockSpec((D_MODEL, n), lambda b, i: (0, 0)),
        ],
        out_specs=pl.BlockSpec((1, tm, n), lambda b, i: (b, i, 0)),
        out_shape=jax.ShapeDtypeStruct((nb, s, n), F32),
        compiler_params=_cparams("arbitrary", "arbitrary"),
        name="norm_mod_matmul",
    )(x, mod, gamma, w_bf16)


def _proj_res_body(m_ref, w_ref, x_ref, mod_ref, o_ref, *, gi):
    y = jnp.dot(m_ref[0].astype(BF16), w_ref[...], preferred_element_type=F32)
    o_ref[0] = x_ref[0] + mod_ref[0, gi:gi + 1, :] * y


def _proj_residual(m, w_bf16, x, mod, gi, tm=512):
    nb, s, k = m.shape
    return pl.pallas_call(
        functools.partial(_proj_res_body, gi=gi),
        grid=(nb, s // tm),
        in_specs=[
            pl.BlockSpec((1, tm, k), lambda b, i: (b, i, 0)),
            pl.BlockSpec((k, D_MODEL), lambda b, i: (0, 0)),
            pl.BlockSpec((1, tm, D_MODEL), lambda b, i: (b, i, 0)),
            pl.BlockSpec((1, 6, D_MODEL), lambda b, i: (b, 0, 0)),
        ],
        out_specs=pl.BlockSpec((1, tm, D_MODEL), lambda b, i: (b, i, 0)),
        out_shape=jax.ShapeDtypeStruct((nb, s, D_MODEL), F32),
        compiler_params=_cparams("arbitrary", "arbitrary"),
        name="proj_residual",
    )(m, w_bf16, x, mod)


def _scan_tile(a, b, carry, reverse):
    tt = a.shape[0]
    row = lax.broadcasted_iota(I32, (tt, LANES), 0) % SUBLANES
    for d in (1, 2, 4):
        if reverse:
            shift, keep = tt - d, row < SUBLANES - d
        else:
            shift, keep = d, row >= d
        ap = pltpu.roll(a, shift, 0)
        bp = pltpu.roll(b, shift, 0)
        b = jnp.where(keep, a * bp + b, b)
        a = jnp.where(keep, a * ap, a)
    groups = tt // SUBLANES
    hs = [None] * groups
    order = range(groups - 1, -1, -1) if reverse else range(groups)
    for g in order:
        lo = g * SUBLANES
        h = a[lo:lo + SUBLANES] * carry + b[lo:lo + SUBLANES]
        hs[g] = h
        last = h[0:1] if reverse else h[SUBLANES - 1:SUBLANES]
        carry = jnp.broadcast_to(last, (SUBLANES, LANES))
    return jnp.concatenate(hs, axis=0), carry


def _rglru_body(y_ref, x_ref, cw_ref, cb_ref, gaw_ref, gab_ref, gxw_ref, gxb_ref, nc_ref,
                o_ref, xp, hf, *, s, tt):
    nt = s // tt
    pad = SUBLANES
    zeros = jnp.zeros((pad, LANES), F32)
    xp[0:pad, :] = zeros
    xp[s + pad:s + 2 * pad, :] = zeros

    def copy_tile(i, c):
        r0 = pl.multiple_of(i * tt, tt)
        xp[pl.ds(pl.multiple_of(r0 + pad, SUBLANES), tt), :] = x_ref[0, pl.ds(r0, tt), :]
        return c

    lax.fori_loop(0, nt, copy_tile, 0)

    cw = cw_ref[...]
    cb = cb_ref[...]

    def conv_tile(r0):
        win = xp[pl.ds(r0, tt + 2 * pad), :]
        acc = cb + cw[2:3] * win[pad:pad + tt]
        for k in (0, 1, 3):
            shifted = pltpu.roll(win, (2 - k) % (tt + 2 * pad), 0)
            acc = acc + cw[k:k + 1] * shifted[pad:pad + tt]
        return acc

    def gates(d, xc):
        xb = xc.astype(BF16)
        r = jax.nn.sigmoid(jnp.dot(xb, gaw_ref[d, 0], preferred_element_type=F32) + gab_ref[d])
        ig = jax.nn.sigmoid(jnp.dot(xb, gxw_ref[d, 0], preferred_element_type=F32) + gxb_ref[d])
        a = jnp.exp(nc_ref[d] * r)
        b = jnp.sqrt(1.0 - a * a) * (ig * xc)
        return a, b

    carry0 = jnp.zeros((SUBLANES, LANES), F32)

    def fwd(i, carry):
        r0 = pl.multiple_of(i * tt, tt)
        a, b = gates(0, conv_tile(r0))
        h, carry = _scan_tile(a, b, carry, False)
        hf[pl.ds(r0, tt), :] = h
        return carry

    lax.fori_loop(0, nt, fwd, carry0)

    def bwd(ii, carry):
        r0 = pl.multiple_of((nt - 1 - ii) * tt, tt)
        a, b = gates(1, conv_tile(r0))
        h, carry = _scan_tile(a, b, carry, True)
        o_ref[0, pl.ds(r0, tt), :] = (hf[pl.ds(r0, tt), :] + h) * _gelu(y_ref[0, pl.ds(r0, tt), :])
        return carry

    lax.fori_loop(0, nt, bwd, carry0)


def _rglru_core(u, conv_w, conv_b, gaw, gab, gxw, gxb, negc, tt=256):
    nb, s, _ = u.shape
    ng = RNN_WIDTH // LANES
    return pl.pallas_call(
        functools.partial(_rglru_body, s=s, tt=tt),
        grid=(nb, ng),
        in_specs=[
            pl.BlockSpec((1, s, LANES), lambda b, j: (b, 0, j)),
            pl.BlockSpec((1, s, LANES), lambda b, j: (b, 0, ng + j)),
            pl.BlockSpec((CONV_W, LANES), lambda b, j: (0, j)),
            pl.BlockSpec((1, LANES), lambda b, j: (0, j)),
            pl.BlockSpec((2, 1, LANES, LANES), lambda b, j: (0, j, 0, 0)),
            pl.BlockSpec((2, 1, LANES), lambda b, j: (0, 0, j)),
            pl.BlockSpec((2, 1, LANES, LANES), lambda b, j: (0, j, 0, 0)),
            pl.BlockSpec((2, 1, LANES), lambda b, j: (0, 0, j)),
            pl.BlockSpec((2, 1, LANES), lambda b, j: (0, 0, j)),
        ],
        out_specs=pl.BlockSpec((1, s, LANES), lambda b, j: (b, 0, j)),
        out_shape=jax.ShapeDtypeStruct((nb, s, RNN_WIDTH), F32),
        scratch_shapes=[pltpu.VMEM((s + 2 * SUBLANES, LANES), F32), pltpu.VMEM((s, LANES), F32)],
        compiler_params=_cparams("arbitrary", "arbitrary"),
        name="rglru_core",
    )(u, u, conv_w, conv_b, gaw, gab, gxw, gxb, negc)


def _block_diag_groups(w):
    ng = RNN_WIDTH // LANES
    w = w.reshape(ng, 2, RNN_BLOCK_W, RNN_BLOCK_W)
    out = jnp.zeros((ng, LANES, LANES), w.dtype)
    out = out.at[:, :RNN_BLOCK_W, :RNN_BLOCK_W].set(w[:, 0])
    out = out.at[:, RNN_BLOCK_W:, RNN_BLOCK_W:].set(w[:, 1])
    return out


def _rglru_layer(x, mod, gamma, w_in, conv_w, conv_b, ga_w, ga_b, gx_w, gx_b, lam, w_out):
    u = _norm_mod_matmul(x, mod, gamma, w_in.astype(BF16), sh=0, sc=1)
    gaw = jnp.stack([_block_diag_groups(ga_w[d]) for d in range(2)]).astype(BF16)
    gxw = jnp.stack([_block_diag_groups(gx_w[d]) for d in range(2)]).astype(BF16)
    gab = ga_b.reshape(2, 1, RNN_WIDTH)
    gxb = gx_b.reshape(2, 1, RNN_WIDTH)
    negc = (-LRU_C * jax.nn.softplus(-lam)).reshape(2, 1, RNN_WIDTH)
    m = _rglru_core(u, conv_w, conv_b.reshape(1, RNN_WIDTH), gaw, gab, gxw, gxb, negc)
    return _proj_residual(m, w_out.astype(BF16), x, mod, gi=2)


def _seg_mean(x2, seg_ref):
    hi = x2.astype(BF16)
    lo = (x2 - hi.astype(F32)).astype(BF16)
    return (jnp.dot(hi, seg_ref[...], preferred_element_type=F32)
            + jnp.dot(lo, seg_ref[...], preferred_element_type=F32))


def _rope(x, cos, sin, lane_lo):
    outs = []
    for j in range(x.shape[1] // LANES):
        xt = x[:, j * LANES:(j + 1) * LANES]
        rot = jnp.where(lane_lo, pltpu.roll(xt, LANES - AXIS_DIM // 2, 1), pltpu.roll(xt, AXIS_DIM // 2, 1))
        outs.append(xt * cos + rot * sin)
    return jnp.concatenate(outs, axis=1) if len(outs) > 1 else outs[0]


def _qkv_body(x_ref, mod_ref, g_ref, w_ref, segq_ref, segk_ref, qg_ref, kg_ref, cosq_ref, sinq_ref,
              cosk_ref, sink_ref, q_ref, kt_ref, v_ref, *, tm):
    h = _norm_mod(x_ref[0], g_ref[...], mod_ref[0, 1:2, :], mod_ref[0, 0:1, :])
    qkv = jnp.dot(h.astype(BF16), w_ref[...], preferred_element_type=F32)
    nq = N_HEADS * HEAD_DIM
    nk = N_KV_HEADS * HEAD_DIM
    q = qkv[:, :nq]
    k = qkv[:, nq:nq + nk]
    v = qkv[:, nq + nk:]
    lane = lax.broadcasted_iota(I32, (tm, LANES), 1)
    lane_lo = (lane % AXIS_DIM) < (AXIS_DIM // 2)
    q = q * lax.rsqrt(_seg_mean(q * q, segq_ref) + EPS) * qg_ref[...]
    k = k * lax.rsqrt(_seg_mean(k * k, segk_ref) + EPS) * kg_ref[...]
    q = _rope(q, cosq_ref[...], sinq_ref[...], lane_lo)
    k = _rope(k, cosk_ref[...], sink_ref[...], lane_lo)
    q_ref[0] = q.astype(BF16)
    kt = k.T.astype(BF16)
    for g in range(N_KV_HEADS):
        kt_ref[0, g] = kt[g * HEAD_DIM:(g + 1) * HEAD_DIM, :]
        v_ref[0, g] = v[:, g * HEAD_DIM:(g + 1) * HEAD_DIM].astype(BF16)


def _rope_tables(s):
    rows = s // GRID_W
    row = jnp.repeat(jnp.arange(rows, dtype=F32), GRID_W)
    col = jnp.tile(jnp.arange(GRID_W, dtype=F32), rows)
    inv = ROPE_THETA ** (-jnp.arange(0, AXIS_DIM, 2, dtype=F32) / AXIS_DIM)
    ar = row[:, None] * inv
    ac = col[:, None] * inv
    cos = jnp.concatenate([jnp.cos(ar), jnp.cos(ar), jnp.cos(ac), jnp.cos(ac)], axis=1)
    sin = jnp.concatenate([-jnp.sin(ar), jnp.sin(ar), -jnp.sin(ac), jnp.sin(ac)], axis=1)
    return jnp.tile(cos, (1, LANES // HEAD_DIM)), jnp.tile(sin, (1, LANES // HEAD_DIM))


def _attn_body(q_ref, kt_ref, v_ref, o_ref):
    kt = kt_ref[0, 0]
    v = v_ref[0, 0]
    outs = []
    for hh in range(GROUP):
        qh = q_ref[0, :, hh * HEAD_DIM:(hh + 1) * HEAD_DIM]
        sc = jnp.dot(qh, kt, preferred_element_type=F32)
        m = jnp.max(sc, axis=-1, keepdims=True)
        p = jnp.exp(sc - m)
        l = jnp.sum(p, axis=-1, keepdims=True)
        o = jnp.dot(p.astype(BF16), v, preferred_element_type=F32)
        outs.append(o / l)
    o_ref[0] = jnp.concatenate(outs, axis=1).astype(BF16)


def _attention_layer(x, mod, gamma, w_qkv, q_g, k_g, w_o, tm=256, tq=128):
    nb, s, _ = x.shape
    nq = N_HEADS * HEAD_DIM
    nk = N_KV_HEADS * HEAD_DIM
    seg = jnp.kron(jnp.eye(N_HEADS, dtype=F32), jnp.full((HEAD_DIM, HEAD_DIM), 1.0 / HEAD_DIM, F32)).astype(BF16)
    segk = seg[:nk, :nk]
    cos, sin = _rope_tables(s)
    scale = HEAD_DIM ** -0.5
    q, kt, v = pl.pallas_call(
        functools.partial(_qkv_body, tm=tm),
        grid=(nb, s // tm),
        in_specs=[
            pl.BlockSpec((1, tm, D_MODEL), lambda b, i: (b, i, 0)),
            pl.BlockSpec((1, 6, D_MODEL), lambda b, i: (b, 0, 0)),
            pl.BlockSpec((1, D_MODEL), lambda b, i: (0, 0)),
            pl.BlockSpec((D_MODEL, nq + 2 * nk), lambda b, i: (0, 0)),
            pl.BlockSpec((nq, nq), lambda b, i: (0, 0)),
            pl.BlockSpec((nk, nk), lambda b, i: (0, 0)),
            pl.BlockSpec((1, nq), lambda b, i: (0, 0)),
            pl.BlockSpec((1, nk), lambda b, i: (0, 0)),
            pl.BlockSpec((tm, LANES), lambda b, i: (i, 0)),
            pl.BlockSpec((tm, LANES), lambda b, i: (i, 0)),
            pl.BlockSpec((tm, LANES), lambda b, i: (i, 0)),
            pl.BlockSpec((tm, LANES), lambda b, i: (i, 0)),
        ],
        out_specs=[
            pl.BlockSpec((1, tm, nq), lambda b, i: (b, i, 0)),
            pl.BlockSpec((1, N_KV_HEADS, HEAD_DIM, tm), lambda b, i: (b, 0, 0, i)),
            pl.BlockSpec((1, N_KV_HEADS, tm, HEAD_DIM), lambda b, i: (b, 0, i, 0)),
        ],
        out_shape=[
            jax.ShapeDtypeStruct((nb, s, nq), BF16),
            jax.ShapeDtypeStruct((nb, N_KV_HEADS, HEAD_DIM, s), BF16),
            jax.ShapeDtypeStruct((nb, N_KV_HEADS, s, HEAD_DIM), BF16),
        ],
        compiler_params=_cparams("arbitrary", "arbitrary"),
        name="qkv_rope",
    )(x, mod, gamma, w_qkv.astype(BF16), seg, segk,
      jnp.tile(q_g, N_HEADS).reshape(1, nq), jnp.tile(k_g, N_KV_HEADS).reshape(1, nk),
      cos * scale, sin * scale, cos, sin)

    gw = GROUP * HEAD_DIM
    o = pl.pallas_call(
        _attn_body,
        grid=(nb, N_KV_HEADS, s // tq),
        in_specs=[
            pl.BlockSpec((1, tq, gw), lambda b, g, i: (b, i, g)),
            pl.BlockSpec((1, 1, HEAD_DIM, s), lambda b, g, i: (b, g, 0, 0)),
            pl.BlockSpec((1, 1, s, HEAD_DIM), lambda b, g, i: (b, g, 0, 0)),
        ],
        out_specs=pl.BlockSpec((1, tq, gw), lambda b, g, i: (b, i, g)),
        out_shape=jax.ShapeDtypeStruct((nb, s, nq), BF16),
        compiler_params=_cparams("arbitrary", "arbitrary", "arbitrary"),
        name="attention",
    )(q, kt, v)
    return _proj_residual(o, w_o.astype(BF16), x, mod, gi=2)


def _topk_rows(s, ids, k, id_bound):
    vals, picks = [], []
    for _ in range(k):
        m = jnp.max(s, axis=0, keepdims=True)
        i = jnp.min(jnp.where(s == m, ids, id_bound), axis=0, keepdims=True)
        vals.append(m)
        picks.append(i)
        s = jnp.where(ids == i, -jnp.inf, s)
    return jnp.concatenate(vals, axis=0), jnp.concatenate(picks, axis=0)


ROUTE_HEADS_PER_STEP = 8


def _route_body(x_ref, mod_ref, g_ref, wq_ref, keys_ref, h_ref, e_ref, gate_ref, hb, *, tm):
    hd = pl.program_id(2)

    @pl.when(hd == 0)
    def _():
        h = _norm_mod(x_ref[0], g_ref[...], mod_ref[0, 4:5, :], mod_ref[0, 3:4, :])
        h_ref[0] = h
        hb[...] = h.astype(BF16)

    q = jnp.dot(hb[...], wq_ref[...], preferred_element_type=F32)
    for hh in range(ROUTE_HEADS_PER_STEP):
        gate, eid = _route_head(q[:, hh * PEER_QDIM:(hh + 1) * PEER_QDIM], keys_ref, tm)
        gate_ref[0, hh] = gate
        e_ref[0, hh] = eid


def _route_head(q, keys_ref, tm):
    key_ids = lax.broadcasted_iota(I32, (N_KEYS, tm), 0)
    tops = []
    for p in range(2):
        qp = q[:, p * PEER_HALF:(p + 1) * PEER_HALF].astype(BF16)
        st = lax.dot_general(keys_ref[p], qp, (((1,), (1,)), ((), ())), preferred_element_type=F32)
        tops.append(_topk_rows(st, key_ids, PEER_TOPK, N_KEYS))
    (v0, i0), (v1, i1) = tops

    r16 = lax.broadcasted_iota(I32, (PEER_TOPK, tm), 0)
    cand, fid, eid = [], [], []
    for a in range(4):
        cand.append(v0[a:a + 1] + v1)
        fid.append(a * PEER_TOPK + r16)
        eid.append(i0[a:a + 1] * N_KEYS + i1)
    for b in range(3):
        cand.append(jnp.where(r16 >= 4, v0 + v1[b:b + 1], -jnp.inf))
        fid.append(r16 * PEER_TOPK + b)
        eid.append(i0 * N_KEYS + i1[b:b + 1])
    cand = jnp.concatenate(cand, axis=0)
    fid = jnp.concatenate(fid, axis=0)
    eid = jnp.concatenate(eid, axis=0)

    best, chosen = [], []
    for _ in range(PEER_TOPK):
        m = jnp.max(cand, axis=0, keepdims=True)
        f = jnp.min(jnp.where(cand == m, fid, PEER_TOPK * PEER_TOPK), axis=0, keepdims=True)
        hit = fid == f
        best.append(m)
        chosen.append(jnp.max(jnp.where(hit, eid, -1), axis=0, keepdims=True))
        cand = jnp.where(hit, -jnp.inf, cand)
    best = jnp.concatenate(best, axis=0)
    ex = jnp.exp(best - best[0:1])
    return ex / jnp.sum(ex, axis=0, keepdims=True), jnp.concatenate(chosen, axis=0)


def _peer_route(x, mod, gamma, w_query, sub_keys, tm=128):
    nb, s, _ = x.shape
    hps = ROUTE_HEADS_PER_STEP
    return pl.pallas_call(
        functools.partial(_route_body, tm=tm),
        grid=(nb, s // tm, PEER_HEADS // hps),
        in_specs=[
            pl.BlockSpec((1, tm, D_MODEL), lambda b, i, h: (b, i, 0)),
            pl.BlockSpec((1, 6, D_MODEL), lambda b, i, h: (b, 0, 0)),
            pl.BlockSpec((1, D_MODEL), lambda b, i, h: (0, 0)),
            pl.BlockSpec((D_MODEL, hps * PEER_QDIM), lambda b, i, h: (0, h)),
            pl.BlockSpec((2, N_KEYS, PEER_HALF), lambda b, i, h: (0, 0, 0)),
        ],
        out_specs=[
            pl.BlockSpec((1, tm, D_MODEL), lambda b, i, h: (b, i, 0)),
            pl.BlockSpec((1, hps, PEER_TOPK, tm), lambda b, i, h: (b, h, 0, i)),
            pl.BlockSpec((1, hps, PEER_TOPK, tm), lambda b, i, h: (b, h, 0, i)),
        ],
        out_shape=[
            jax.ShapeDtypeStruct((nb, s, D_MODEL), F32),
            jax.ShapeDtypeStruct((nb, PEER_HEADS, PEER_TOPK, s), I32),
            jax.ShapeDtypeStruct((nb, PEER_HEADS, PEER_TOPK, s), F32),
        ],
        scratch_shapes=[pltpu.VMEM((tm, D_MODEL), BF16)],
        compiler_params=_cparams("arbitrary", "arbitrary", "arbitrary"),
        name="peer_route",
    )(x, mod, gamma, w_query.astype(BF16), sub_keys.astype(BF16))


def _pack_body(u_ref, v_ref, o_ref):
    o_ref[...] = pltpu.pack_elementwise([u_ref[...], v_ref[...]], packed_dtype=BF16).astype(I32)


def _pack_table(u, v, te=512):
    ne = u.shape[0]
    return pl.pallas_call(
        _pack_body,
        grid=(ne // te,),
        in_specs=[pl.BlockSpec((te, D_MODEL), lambda i: (i, 0))] * 2,
        out_specs=pl.BlockSpec((te, D_MODEL), lambda i: (i, 0)),
        out_shape=jax.ShapeDtypeStruct((ne, D_MODEL), I32),
        compiler_params=_cparams("arbitrary"),
        name="peer_pack",
    )(u, v)


def _gather_body(eid_ref, h_ref, g_ref, tab_ref, x_ref, mod_ref, fg_ref, o_ref, buf, sem, po,
                 *, tb, final_norm):
    ne = PEER_PICKS
    rows = ROWS_PER_EXPERT

    ns = GATHER_SLOTS
    ahead = ns - 1

    def issue(t, slot, lo, hi):
        for k in range(lo, hi):
            src = tab_ref.at[pl.ds(pl.multiple_of(eid_ref[t, k] * rows, rows), rows), :]
            dst = buf.at[slot, pl.ds(GATHER_PITCH * k, rows), :]
            pltpu.make_async_copy(src, dst, sem.at[slot]).start(priority=k % 2)

    def wait(slot):
        pltpu.make_async_copy(tab_ref.at[pl.ds(0, ne * rows), :], buf.at[slot, pl.ds(0, ne * rows), :],
                              sem.at[slot]).wait()

    for q in range(ahead):
        issue(q, q, 0, ne)
    early = 4
    halves = 2
    per_half = ne // halves
    even = lax.broadcasted_iota(I32, (1, 2 * LANES), 1) % 2 == 0
    row_id = lax.broadcasted_iota(I32, (SUBLANES, LANES), 0)

    def chunk(slot, c, first=0, count=ne):
        words = buf[slot, pl.ds(first * GATHER_PITCH + c, count, stride=GATHER_PITCH), :]
        return pltpu.bitcast(words, BF16)

    def score_half(t8, j, hf, c, acc):
        xrow = h_ref[pl.ds(t8, SUBLANES), c * LANES:(c + 1) * LANES][j:j + 1]
        return acc + chunk(j % ns, c, hf * per_half, per_half).astype(F32) * xrow

    def lane_sums(acc):
        return jnp.sum(acc.T, axis=0, keepdims=True)

    def coefficients(parts, t8, j):
        s = jnp.concatenate(parts, axis=1)
        g = g_ref[pl.ds(t8, SUBLANES), :][j:j + 1]
        coef = pltpu.roll(jnp.where(even, _gelu(s) * g, 0.0), 1, 1)
        return jnp.broadcast_to(coef, (SUBLANES, 2 * LANES))

    wait(0)
    parts = []
    for hf in range(halves):
        a = jnp.zeros((2 * per_half, LANES), F32)
        for c in range(rows):
            a = score_half(0, 0, hf, c, a)
        parts.append(lane_sums(a))
    coef0 = coefficients(parts, 0, 0)

    def body(it, coef):
        t8 = pl.multiple_of(it * SUBLANES, SUBLANES)
        t8_next = pl.multiple_of(jnp.minimum(t8 + SUBLANES, tb - SUBLANES), SUBLANES)
        acc = [None] * rows
        for j in range(SUBLANES):
            jn = (j + 1) % SUBLANES
            t8n = t8 if j + 1 < SUBLANES else t8_next
            nxt = jnp.minimum(t8 + j + ahead, tb - 1)
            nslot = (j + ahead) % ns
            wait((j + 1) % ns)
            coef_b = coef.astype(BF16)
            step = 0
            parts = []
            for hf in range(halves):
                a = jnp.zeros((2 * per_half, LANES), F32)
                for c in range(rows):
                    issue(nxt, nslot, step * early, (step + 1) * early)
                    step += 1
                    a = score_half(t8n, jn, hf, c, a)
                parts.append(lane_sums(a))
            for c in range(rows):
                issue(nxt, nslot, step * early, (step + 1) * early)
                step += 1
                o = jnp.dot(coef_b, chunk(j % ns, c), preferred_element_type=F32)
                acc[c] = o if j == 0 else jnp.where(row_id == j, o, acc[c])
            issue(nxt, nslot, step * early, ne)
            coef = coefficients(parts, t8n, jn)
        for c in range(rows):
            po[pl.ds(t8, SUBLANES), c * LANES:(c + 1) * LANES] = acc[c]
        return coef

    lax.fori_loop(0, tb // SUBLANES, body, coef0)
    for q in range(1, ahead):
        wait((tb + q) % ns)
    y = x_ref[...] + mod_ref[0, 5:6, :] * po[...]
    if final_norm:
        y = y * lax.rsqrt(jnp.mean(y * y, axis=-1, keepdims=True) + EPS) * fg_ref[...]
    o_ref[...] = y


def _peer_gather(eid, h, g, table, x, mod, final_g, seq_len, final_norm, tok_offset, tb=GATHER_TOKENS):
    t = eid.shape[0]
    off = tok_offset // tb
    return pl.pallas_call(
        functools.partial(_gather_body, tb=tb, final_norm=final_norm),
        grid=(t // tb,),
        in_specs=[
            pl.BlockSpec((tb, PEER_PICKS), lambda i: (i, 0), memory_space=pltpu.SMEM),
            pl.BlockSpec((tb, D_MODEL), lambda i: (off + i, 0)),
            pl.BlockSpec((tb, 2 * PEER_PICKS), lambda i: (i, 0)),
            pl.BlockSpec(memory_space=pl.ANY),
            pl.BlockSpec((tb, D_MODEL), lambda i: (off + i, 0)),
            pl.BlockSpec((1, 6, D_MODEL), lambda i: ((tok_offset + i * tb) // seq_len, 0, 0)),
            pl.BlockSpec((1, D_MODEL), lambda i: (0, 0)),
        ],
        out_specs=pl.BlockSpec((tb, D_MODEL), lambda i: (i, 0)),
        out_shape=jax.ShapeDtypeStruct((t, D_MODEL), F32),
        scratch_shapes=[
            pltpu.VMEM((GATHER_SLOTS, PEER_PICKS * GATHER_PITCH, LANES), I32),
            pltpu.SemaphoreType.DMA((GATHER_SLOTS,)),
            pltpu.VMEM((tb, D_MODEL), F32),
        ],
        compiler_params=_cparams("arbitrary"),
        name="peer_gather",
    )(eid, h, g, table, x, mod, final_g)


SC_LANES = 16
SC_WORKERS = 32
SC_GROUP = 8
SC_TOKENS = ((8192, 14336), (9216, 9216))
SC_COST_SCALE = 1
TRUNK_ORDER = ((0, 1), (0, 1))
SC_CHUNKS = D_MODEL // SC_LANES


def _sc_body(tab_hbm, eid_hbm, g_hbm, h_hbm, out_hbm, idx_v, g_v, x_v, o_v, rows0, rows1, rows2, rows3,
             sem0, sem1, sem2, sem3, *, tpw):
    nl = SC_LANES
    wid = lax.axis_index("s") * 2 + lax.axis_index("c")
    lane = lax.iota(I32, nl)

    def permute(x, idx):
        return jnp.take_along_axis(x, idx, axis=0, mode="promise_in_bounds")
    rows = (rows0, rows1, rows2, rows3)
    sems = (sem0, sem1, sem2, sem3)
    ns = len(rows)
    units = SC_GROUP * PEER_HEADS

    def gather(tok, hd, slot):
        return pltpu.make_async_copy(tab_hbm.at[idx_v.at[tok, hd]], rows[slot], sems[slot])

    def compute(tok, hd, slot, t):
        rv = rows[slot]

        @pl.when(hd == 0)
        def _():
            def zero(c, cc):
                o_v[pl.ds(c * nl, nl)] = jnp.zeros((nl,), F32)
                return cc
            lax.fori_loop(0, SC_CHUNKS, zero, 0)

        def score(c, accs):
            x = x_v[tok, pl.ds(c * nl, nl)]
            out = []
            for kk in range(PEER_TOPK):
                u = plsc.bitcast(lax.shift_left(rv[kk, pl.ds(c * nl, nl)], 16), F32)
                out.append(accs[kk] + u * x)
            return tuple(out)

        accs = lax.fori_loop(0, SC_CHUNKS, score, tuple(jnp.zeros((nl,), F32) for _ in range(PEER_TOPK)))
        vecs = list(accs)
        d = 1
        while len(vecs) > 1:
            partner = lane ^ d
            take_lo = (lane & d) == 0
            nxt_vecs = []
            for i in range(0, len(vecs), 2):
                a = vecs[i] + permute(vecs[i], partner)
                b = vecs[i + 1] + permute(vecs[i + 1], partner)
                nxt_vecs.append(jnp.where(take_lo, a, b))
            vecs = nxt_vecs
            d *= 2
        s = vecs[0]
        z = 0.7978845608028654 * (s + 0.044715 * s * s * s)
        act = s * (1.0 - 1.0 / (jnp.exp(2.0 * z) + 1.0))
        coef = act * g_v[tok, hd, :]
        coefs = [permute(coef, jnp.full((nl,), kk, I32)) for kk in range(PEER_TOPK)]

        def combine(c, cc):
            parts = []
            for q in range(4):
                acc = None
                for kk in range(q * PEER_TOPK // 4, (q + 1) * PEER_TOPK // 4):
                    v = plsc.bitcast(lax.bitwise_and(rv[kk, pl.ds(c * nl, nl)], jnp.int32(-65536)), F32)
                    term = coefs[kk] * v
                    acc = term if acc is None else acc + term
                parts.append(acc)
            o_v[pl.ds(c * nl, nl)] = o_v[pl.ds(c * nl, nl)] + ((parts[0] + parts[1]) + (parts[2] + parts[3]))
            return cc

        lax.fori_loop(0, SC_CHUNKS, combine, 0)

        @pl.when(hd == PEER_HEADS - 1)
        def _():
            pltpu.sync_copy(o_v, out_hbm.at[t])

    def group(gi, carry):
        t0 = wid * tpw + gi * SC_GROUP
        pltpu.sync_copy(eid_hbm.at[pl.ds(t0, SC_GROUP)], idx_v)
        pltpu.sync_copy(g_hbm.at[pl.ds(t0, SC_GROUP)], g_v)
        pltpu.sync_copy(h_hbm.at[pl.ds(t0, SC_GROUP)], x_v)
        for q in range(ns - 1):
            gather(0, q, q).start()

        def ring(ri, cc):
            j0 = ri * ns
            tok, hd0 = j0 // PEER_HEADS, j0 % PEER_HEADS
            for q in range(ns):
                jn = j0 + q + ns - 1

                @pl.when(jn < units)
                def _():
                    gather(jn // PEER_HEADS, jn % PEER_HEADS, (q + ns - 1) % ns).start()

                gather(tok, hd0 + q, q).wait()
                compute(tok, hd0 + q, q, t0 + tok)
            return cc

        lax.fori_loop(0, units // ns, ring, 0)
        return carry

    lax.fori_loop(0, tpw // SC_GROUP, group, 0)


def _peer_sc(table, eid, gate, h):
    t = eid.shape[0]
    tpw = t // SC_WORKERS
    cp = pltpu.CompilerParams()
    if "needs_layout_passes" in pltpu.CompilerParams.__dataclass_fields__:
        cp = dataclasses.replace(cp, needs_layout_passes=False)
    run = pl.kernel(
        functools.partial(_sc_body, tpw=tpw),
        out_type=jax.ShapeDtypeStruct((t, D_MODEL), F32),
        mesh=plsc.VectorSubcoreMesh(core_axis_name="c", subcore_axis_name="s"),
        scratch_types=[
            pltpu.VMEM((SC_GROUP, PEER_HEADS, PEER_TOPK), I32),
            pltpu.VMEM((SC_GROUP, PEER_HEADS, PEER_TOPK), F32),
            pltpu.VMEM((SC_GROUP, D_MODEL), F32),
            pltpu.VMEM((D_MODEL,), F32),
            pltpu.VMEM((PEER_TOPK, D_MODEL), I32),
            pltpu.VMEM((PEER_TOPK, D_MODEL), I32),
            pltpu.VMEM((PEER_TOPK, D_MODEL), I32),
            pltpu.VMEM((PEER_TOPK, D_MODEL), I32),
            pltpu.SemaphoreType.DMA,
            pltpu.SemaphoreType.DMA,
            pltpu.SemaphoreType.DMA,
            pltpu.SemaphoreType.DMA,
        ],
        compiler_params=cp,
        cost_estimate=pl.CostEstimate(
            flops=SC_COST_SCALE * 4 * t * PEER_PICKS * D_MODEL,
            transcendentals=t * PEER_PICKS,
            bytes_accessed=SC_COST_SCALE * (t * PEER_PICKS * D_MODEL * 4 + 2 * t * D_MODEL * 4),
        ),
        name="peer_sc",
    )
    return run(table, eid.reshape(t, PEER_HEADS, PEER_TOPK), gate.reshape(t, PEER_HEADS, PEER_TOPK), h)


def _finish_body(x_ref, po_ref, mod_ref, fg_ref, o_ref, *, final_norm):
    y = x_ref[...] + mod_ref[0, 5:6, :] * po_ref[...]
    if final_norm:
        y = y * lax.rsqrt(jnp.mean(y * y, axis=-1, keepdims=True) + EPS) * fg_ref[...]
    o_ref[...] = y


def _peer_finish(x, po, mod, final_g, seq_len, final_norm, tm=512):
    t = po.shape[0]
    return pl.pallas_call(
        functools.partial(_finish_body, final_norm=final_norm),
        grid=(t // tm,),
        in_specs=[
            pl.BlockSpec((tm, D_MODEL), lambda i: (i, 0)),
            pl.BlockSpec((tm, D_MODEL), lambda i: (i, 0)),
            pl.BlockSpec((1, 6, D_MODEL), lambda i: ((i * tm) // seq_len, 0, 0)),
            pl.BlockSpec((1, D_MODEL), lambda i: (0, 0)),
        ],
        out_specs=pl.BlockSpec((tm, D_MODEL), lambda i: (i, 0)),
        out_shape=jax.ShapeDtypeStruct((t, D_MODEL), F32),
        compiler_params=_cparams("arbitrary"),
        name="peer_finish",
    )(x, po, mod, final_g)


def _peer_route_phase(x, mod, gamma, w_query, sub_keys, t_sc):
    nb, s, _ = x.shape
    t = nb * s
    h, e_t, g_t = _peer_route(x, mod, gamma, w_query, sub_keys)
    eid = e_t.transpose(0, 3, 1, 2).reshape(t, PEER_PICKS)
    gate = g_t.transpose(0, 3, 1, 2).reshape(t, PEER_PICKS)
    hf = h.reshape(t, D_MODEL)
    xf = x.reshape(t, D_MODEL)
    gate_tc = gate[t_sc:]
    gate_tc = jnp.stack([gate_tc, jnp.zeros_like(gate_tc)], axis=-1).reshape(t - t_sc, 2 * PEER_PICKS)
    return dict(sc=(eid[:t_sc], gate[:t_sc], hf), x_sc=xf, tc=(eid[t_sc:], hf, gate_tc), x_tc=xf)


def _peer_retrieve_tc(ops, shape, mod, table, final_g, final_norm, t_sc):
    return _peer_gather(*ops["tc"], table, ops["x_tc"], mod, final_g, shape[1], final_norm, t_sc)


def _peer_retrieve_finish(ops, po, y_tc, shape, mod, final_g, final_norm):
    nb, s, _ = shape
    y_sc = _peer_finish(ops["x_sc"], po, mod, final_g, s, final_norm)
    return jnp.concatenate([y_sc, y_tc], axis=0).reshape(nb, s, D_MODEL)


def kernel(x_prompt, x_sample, c_prompt, c_sample, ln_mix_g, ln_ffn_g, w_mod, b_mod, rec_w_in, rec_conv_w,
           rec_conv_b, rec_ga_w, rec_ga_b, rec_gx_w, rec_gx_b, rec_lam, rec_w_out, att_w_qkv, att_q_g,
           att_k_g, att_w_o, peer_w_query, peer_sub_keys, peer_u, peer_v, final_g):
    tables_sc = [_pack_table(peer_u[i], peer_v[i]) for i in range(DEPTH)]
    tables = [tb.reshape(N_EXPERTS * ROWS_PER_EXPERT, LANES) for tb in tables_sc]
    fg = final_g.reshape(1, D_MODEL)

    xs = [x_prompt, x_sample]
    cs = [c_prompt, c_sample]
    for i in range(DEPTH):
        gm = ln_mix_g[i].reshape(1, D_MODEL)
        gf = ln_ffn_g[i].reshape(1, D_MODEL)
        j = i // 2
        last = i == DEPTH - 1
        for tr in TRUNK_ORDER[i]:
            x = xs[tr]
            mod = _mod(cs[tr], w_mod[i], b_mod[i])
            if i % 2 == 0:
                x = _rglru_layer(x, mod, gm, rec_w_in[j], rec_conv_w[j], rec_conv_b[j], rec_ga_w[j],
                                 rec_ga_b[j], rec_gx_w[j], rec_gx_b[j], rec_lam[j], rec_w_out[j])
            else:
                x = _attention_layer(x, mod, gm, att_w_qkv[j], att_q_g[j], att_k_g[j], att_w_o[j])
            t_sc = SC_TOKENS[i][tr]
            routed = _peer_route_phase(x, mod, gf, peer_w_query[i], peer_sub_keys[i], t_sc)
            y_tc = _peer_retrieve_tc(routed, x.shape, mod, tables[i], fg, last, t_sc)
            po = _peer_sc(tables_sc[i], *routed["sc"])
            xs[tr] = _peer_retrieve_finish(routed, po, y_tc, x.shape, mod, fg, last)
    return (xs[0], xs[1])
```

```python
import dataclasses
import functools

import jax
import jax.numpy as jnp
from jax import lax
from jax.experimental import pallas as pl
from jax.experimental.pallas import tpu as pltpu
from jax.experimental.pallas import tpu_sc as plsc

F32 = jnp.float32
BF16 = jnp.bfloat16
I32 = jnp.int32

D_MODEL = 1024
DEPTH = 2
GRID_W = 64
EPS = 1e-6
RNN_WIDTH = D_MODEL
RNN_BLOCKS = 16
RNN_BLOCK_W = RNN_WIDTH // RNN_BLOCKS
CONV_W = 4
LRU_C = 8.0
N_HEADS = 16
N_KV_HEADS = 4
HEAD_DIM = D_MODEL // N_HEADS
GROUP = N_HEADS // N_KV_HEADS
AXIS_DIM = HEAD_DIM // 2
ROPE_THETA = 10000.0
N_KEYS = 128
N_EXPERTS = N_KEYS * N_KEYS
PEER_HEADS = 8
PEER_TOPK = 16
PEER_QDIM = 256
PEER_HALF = PEER_QDIM // 2
PEER_PICKS = PEER_HEADS * PEER_TOPK

LANES = 128
SUBLANES = 8
VMEM_LIMIT = 48 * 1024 * 1024

GATHER_PITCH = 12
GATHER_TOKENS = 256
GATHER_SLOTS = 4
ROWS_PER_EXPERT = D_MODEL // LANES


def _cparams(*sem):
    return pltpu.CompilerParams(dimension_semantics=sem, vmem_limit_bytes=VMEM_LIMIT)


def _gelu(x):
    return jax.nn.gelu(x)


def _norm_mod(x, gamma, scale, shift):
    ms = jnp.mean(x * x, axis=-1, keepdims=True)
    y = x * lax.rsqrt(ms + EPS) * gamma
    return y * (1.0 + scale) + shift


def _mod_body(c_ref, w_ref, b_ref, o_ref):
    c = c_ref[...]
    s = c * jax.nn.sigmoid(c)
    o_ref[...] = jnp.dot(s, w_ref[...], preferred_element_type=F32) + b_ref[...]


def _mod(c, w, b):
    nb, n = c.shape[0], w.shape[1]
    tn = 1536
    out = pl.pallas_call(
        _mod_body,
        grid=(n // tn,),
        in_specs=[
            pl.BlockSpec((nb, D_MODEL), lambda j: (0, 0)),
            pl.BlockSpec((D_MODEL, tn), lambda j: (0, j)),
            pl.BlockSpec((1, tn), lambda j: (0, j)),
        ],
        out_specs=pl.BlockSpec((nb, tn), lambda j: (0, j)),
        out_shape=jax.ShapeDtypeStruct((nb, n), F32),
        compiler_params=_cparams("arbitrary"),
        name="adaln_mod",
    )(c, w, b.reshape(1, n))
    return out.reshape(nb, 6, D_MODEL)


def _nmm_body(x_ref, mod_ref, g_ref, w_ref, o_ref, *, sh, sc):
    h = _norm_mod(x_ref[0], g_ref[...], mod_ref[0, sc:sc + 1, :], mod_ref[0, sh:sh + 1, :])
    o_ref[0] = jnp.dot(h.astype(BF16), w_ref[...], preferred_element_type=F32)


def _norm_mod_matmul(x, mod, gamma, w_bf16, sh, sc, tm=512):
    nb, s, _ = x.shape
    n = w_bf16.shape[1]
    return pl.pallas_call(
        functools.partial(_nmm_body, sh=sh, sc=sc),
        grid=(nb, s // tm),
        in_specs=[
            pl.BlockSpec((1, tm, D_MODEL), lambda b, i: (b, i, 0)),
            pl.BlockSpec((1, 6, D_MODEL), lambda b, i: (b, 0, 0)),
            pl.BlockSpec((1, D_MODEL), lambda b, i: (0, 0)),
            pl.BlockSpec((D_MODEL, n), lambda b, i: (0, 0)),
        ],
        out_specs=pl.BlockSpec((1, tm, n), lambda b, i: (b, i, 0)),
        out_shape=jax.ShapeDtypeStruct((nb, s, n), F32),
        compiler_params=_cparams("arbitrary", "arbitrary"),
        name="norm_mod_matmul",
    )(x, mod, gamma, w_bf16)


def _proj_res_body(m_ref, w_ref, x_ref, mod_ref, o_ref, *, gi):
    y = jnp.dot(m_ref[0].astype(BF16), w_ref[...], preferred_element_type=F32)
    o_ref[0] = x_ref[0] + mod_ref[0, gi:gi + 1, :] * y


def _proj_residual(m, w_bf16, x, mod, gi, tm=512):
    nb, s, k = m.shape
    return pl.pallas_call(
        functools.partial(_proj_res_body, gi=gi),
        grid=(nb, s // tm),
        in_specs=[
            pl.BlockSpec((1, tm, k), lambda b, i: (b, i, 0)),
            pl.BlockSpec((k, D_MODEL), lambda b, i: (0, 0)),
            pl.BlockSpec((1, tm, D_MODEL), lambda b, i: (b, i, 0)),
            pl.BlockSpec((1, 6, D_MODEL), lambda b, i: (b, 0, 0)),
        ],
        out_specs=pl.BlockSpec((1, tm, D_MODEL), lambda b, i: (b, i, 0)),
        out_shape=jax.ShapeDtypeStruct((nb, s, D_MODEL), F32),
        compiler_params=_cparams("arbitrary", "arbitrary"),
        name="proj_residual",
    )(m, w_bf16, x, mod)


def _scan_tile(a, b, carry, reverse):
    tt = a.shape[0]
    row = lax.broadcasted_iota(I32, (tt, LANES), 0) % SUBLANES
    for d in (1, 2, 4):
        if reverse:
            shift, keep = tt - d, row < SUBLANES - d
        else:
            shift, keep = d, row >= d
        ap = pltpu.roll(a, shift, 0)
        bp = pltpu.roll(b, shift, 0)
        b = jnp.where(keep, a * bp + b, b)
        a = jnp.where(keep, a * ap, a)
    groups = tt // SUBLANES
    hs = [None] * groups
    order = range(groups - 1, -1, -1) if reverse else range(groups)
    for g in order:
        lo = g * SUBLANES
        h = a[lo:lo + SUBLANES] * carry + b[lo:lo + SUBLANES]
        hs[g] = h
        last = h[0:1] if reverse else h[SUBLANES - 1:SUBLANES]
        carry = jnp.broadcast_to(last, (SUBLANES, LANES))
    return jnp.concatenate(hs, axis=0), carry


def _rglru_body(y_ref, x_ref, cw_ref, cb_ref, gaw_ref, gab_ref, gxw_ref, gxb_ref, nc_ref,
                o_ref, xp, hf, *, s, tt):
    nt = s // tt
    pad = SUBLANES
    zeros = jnp.zeros((pad, LANES), F32)
    xp[0:pad, :] = zeros
    xp[s + pad:s + 2 * pad, :] = zeros

    def copy_tile(i, c):
        r0 = pl.multiple_of(i * tt, tt)
        xp[pl.ds(pl.multiple_of(r0 + pad, SUBLANES), tt), :] = x_ref[0, pl.ds(r0, tt), :]
        return c

    lax.fori_loop(0, nt, copy_tile, 0)

    cw = cw_ref[...]
    cb = cb_ref[...]

    def conv_tile(r0):
        win = xp[pl.ds(r0, tt + 2 * pad), :]
        acc = cb + cw[2:3] * win[pad:pad + tt]
        for k in (0, 1, 3):
            shifted = pltpu.roll(win, (2 - k) % (tt + 2 * pad), 0)
            acc = acc + cw[k:k + 1] * shifted[pad:pad + tt]
        return acc

    def gates(d, xc):
        xb = xc.astype(BF16)
        r = jax.nn.sigmoid(jnp.dot(xb, gaw_ref[d, 0], preferred_element_type=F32) + gab_ref[d])
        ig = jax.nn.sigmoid(jnp.dot(xb, gxw_ref[d, 0], preferred_element_type=F32) + gxb_ref[d])
        a = jnp.exp(nc_ref[d] * r)
        b = jnp.sqrt(1.0 - a * a) * (ig * xc)
        return a, b

    carry0 = jnp.zeros((SUBLANES, LANES), F32)

    def fwd(i, carry):
        r0 = pl.multiple_of(i * tt, tt)
        a, b = gates(0, conv_tile(r0))
        h, carry = _scan_tile(a, b, carry, False)
        hf[pl.ds(r0, tt), :] = h
        return carry

    lax.fori_loop(0, nt, fwd, carry0)

    def bwd(ii, carry):
        r0 = pl.multiple_of((nt - 1 - ii) * tt, tt)
        a, b = gates(1, conv_tile(r0))
        h, carry = _scan_tile(a, b, carry, True)
        o_ref[0, pl.ds(r0, tt), :] = (hf[pl.ds(r0, tt), :] + h) * _gelu(y_ref[0, pl.ds(r0, tt), :])
        return carry

    lax.fori_loop(0, nt, bwd, carry0)


def _rglru_core(u, conv_w, conv_b, gaw, gab, gxw, gxb, negc, tt=256):
    nb, s, _ = u.shape
    ng = RNN_WIDTH // LANES
    return pl.pallas_call(
        functools.partial(_rglru_body, s=s, tt=tt),
        grid=(nb, ng),
        in_specs=[
            pl.BlockSpec((1, s, LANES), lambda b, j: (b, 0, j)),
            pl.BlockSpec((1, s, LANES), lambda b, j: (b, 0, ng + j)),
            pl.BlockSpec((CONV_W, LANES), lambda b, j: (0, j)),
            pl.BlockSpec((1, LANES), lambda b, j: (0, j)),
            pl.BlockSpec((2, 1, LANES, LANES), lambda b, j: (0, j, 0, 0)),
            pl.BlockSpec((2, 1, LANES), lambda b, j: (0, 0, j)),
            pl.BlockSpec((2, 1, LANES, LANES), lambda b, j: (0, j, 0, 0)),
            pl.BlockSpec((2, 1, LANES), lambda b, j: (0, 0, j)),
            pl.BlockSpec((2, 1, LANES), lambda b, j: (0, 0, j)),
        ],
        out_specs=pl.BlockSpec((1, s, LANES), lambda b, j: (b, 0, j)),
        out_shape=jax.ShapeDtypeStruct((nb, s, RNN_WIDTH), F32),
        scratch_shapes=[pltpu.VMEM((s + 2 * SUBLANES, LANES), F32), pltpu.VMEM((s, LANES), F32)],
        compiler_params=_cparams("arbitrary", "arbitrary"),
        name="rglru_core",
    )(u, u, conv_w, conv_b, gaw, gab, gxw, gxb, negc)


def _block_diag_groups(w):
    ng = RNN_WIDTH // LANES
    w = w.reshape(ng, 2, RNN_BLOCK_W, RNN_BLOCK_W)
    out = jnp.zeros((ng, LANES, LANES), w.dtype)
    out = out.at[:, :RNN_BLOCK_W, :RNN_BLOCK_W].set(w[:, 0])
    out = out.at[:, RNN_BLOCK_W:, RNN_BLOCK_W:].set(w[:, 1])
    return out


def _rglru_layer(x, mod, gamma, w_in, conv_w, conv_b, ga_w, ga_b, gx_w, gx_b, lam, w_out):
    u = _norm_mod_matmul(x, mod, gamma, w_in.astype(BF16), sh=0, sc=1)
    gaw = jnp.stack([_block_diag_groups(ga_w[d]) for d in range(2)]).astype(BF16)
    gxw = jnp.stack([_block_diag_groups(gx_w[d]) for d in range(2)]).astype(BF16)
    gab = ga_b.reshape(2, 1, RNN_WIDTH)
    gxb = gx_b.reshape(2, 1, RNN_WIDTH)
    negc = (-LRU_C * jax.nn.softplus(-lam)).reshape(2, 1, RNN_WIDTH)
    m = _rglru_core(u, conv_w, conv_b.reshape(1, RNN_WIDTH), gaw, gab, gxw, gxb, negc)
    return _proj_residual(m, w_out.astype(BF16), x, mod, gi=2)


def _seg_mean(x2, seg_ref):
    hi = x2.astype(BF16)
    lo = (x2 - hi.astype(F32)).astype(BF16)
    return (jnp.dot(hi, seg_ref[...], preferred_element_type=F32)
            + jnp.dot(lo, seg_ref[...], preferred_element_type=F32))


def _rope(x, cos, sin, lane_lo):
    outs = []
    for j in range(x.shape[1] // LANES):
        xt = x[:, j * LANES:(j + 1) * LANES]
        rot = jnp.where(lane_lo, pltpu.roll(xt, LANES - AXIS_DIM // 2, 1), pltpu.roll(xt, AXIS_DIM // 2, 1))
        outs.append(xt * cos + rot * sin)
    return jnp.concatenate(outs, axis=1) if len(outs) > 1 else outs[0]


def _qkv_body(x_ref, mod_ref, g_ref, w_ref, segq_ref, segk_ref, qg_ref, kg_ref, cosq_ref, sinq_ref,
              cosk_ref, sink_ref, q_ref, kt_ref, v_ref, *, tm):
    h = _norm_mod(x_ref[0], g_ref[...], mod_ref[0, 1:2, :], mod_ref[0, 0:1, :])
    qkv = jnp.dot(h.astype(BF16), w_ref[...], preferred_element_type=F32)
    nq = N_HEADS * HEAD_DIM
    nk = N_KV_HEADS * HEAD_DIM
    q = qkv[:, :nq]
    k = qkv[:, nq:nq + nk]
    v = qkv[:, nq + nk:]
    lane = lax.broadcasted_iota(I32, (tm, LANES), 1)
    lane_lo = (lane % AXIS_DIM) < (AXIS_DIM // 2)
    q = q * lax.rsqrt(_seg_mean(q * q, segq_ref) + EPS) * qg_ref[...]
    k = k * lax.rsqrt(_seg_mean(k * k, segk_ref) + EPS) * kg_ref[...]
    q = _rope(q, cosq_ref[...], sinq_ref[...], lane_lo)
    k = _rope(k, cosk_ref[...], sink_ref[...], lane_lo)
    q_ref[0] = q.astype(BF16)
    kt = k.T.astype(BF16)
    for g in range(N_KV_HEADS):
        kt_ref[0, g] = kt[g * HEAD_DIM:(g + 1) * HEAD_DIM, :]
        v_ref[0, g] = v[:, g * HEAD_DIM:(g + 1) * HEAD_DIM].astype(BF16)


def _rope_tables(s):
    rows = s // GRID_W
    row = jnp.repeat(jnp.arange(rows, dtype=F32), GRID_W)
    col = jnp.tile(jnp.arange(GRID_W, dtype=F32), rows)
    inv = ROPE_THETA ** (-jnp.arange(0, AXIS_DIM, 2, dtype=F32) / AXIS_DIM)
    ar = row[:, None] * inv
    ac = col[:, None] * inv
    cos = jnp.concatenate([jnp.cos(ar), jnp.cos(ar), jnp.cos(ac), jnp.cos(ac)], axis=1)
    sin = jnp.concatenate([-jnp.sin(ar), jnp.sin(ar), -jnp.sin(ac), jnp.sin(ac)], axis=1)
    return jnp.tile(cos, (1, LANES // HEAD_DIM)), jnp.tile(sin, (1, LANES // HEAD_DIM))


def _attn_body(q_ref, kt_ref, v_ref, o_ref):
    kt = kt_ref[0, 0]
    v = v_ref[0, 0]
    outs = []
    for hh in range(GROUP):
        qh = q_ref[0, :, hh * HEAD_DIM:(hh + 1) * HEAD_DIM]
        sc = jnp.dot(qh, kt, preferred_element_type=F32)
        m = jnp.max(sc, axis=-1, keepdims=True)
        p = jnp.exp(sc - m)
        l = jnp.sum(p, axis=-1, keepdims=True)
        o = jnp.dot(p.astype(BF16), v, preferred_element_type=F32)
        outs.append(o / l)
    o_ref[0] = jnp.concatenate(outs, axis=1).astype(BF16)


def _attention_layer(x, mod, gamma, w_qkv, q_g, k_g, w_o, tm=256, tq=128):
    nb, s, _ = x.shape
    nq = N_HEADS * HEAD_DIM
    nk = N_KV_HEADS * HEAD_DIM
    seg = jnp.kron(jnp.eye(N_HEADS, dtype=F32), jnp.full((HEAD_DIM, HEAD_DIM), 1.0 / HEAD_DIM, F32)).astype(BF16)
    segk = seg[:nk, :nk]
    cos, sin = _rope_tables(s)
    scale = HEAD_DIM ** -0.5
    q, kt, v = pl.pallas_call(
        functools.partial(_qkv_body, tm=tm),
        grid=(nb, s // tm),
        in_specs=[
            pl.BlockSpec((1, tm, D_MODEL), lambda b, i: (b, i, 0)),
            pl.BlockSpec((1, 6, D_MODEL), lambda b, i: (b, 0, 0)),
            pl.BlockSpec((1, D_MODEL), lambda b, i: (0, 0)),
            pl.BlockSpec((D_MODEL, nq + 2 * nk), lambda b, i: (0, 0)),
            pl.BlockSpec((nq, nq), lambda b, i: (0, 0)),
            pl.BlockSpec((nk, nk), lambda b, i: (0, 0)),
            pl.BlockSpec((1, nq), lambda b, i: (0, 0)),
            pl.BlockSpec((1, nk), lambda b, i: (0, 0)),
            pl.BlockSpec((tm, LANES), lambda b, i: (i, 0)),
            pl.BlockSpec((tm, LANES), lambda b, i: (i, 0)),
            pl.BlockSpec((tm, LANES), lambda b, i: (i, 0)),
            pl.BlockSpec((tm, LANES), lambda b, i: (i, 0)),
        ],
        out_specs=[
            pl.BlockSpec((1, tm, nq), lambda b, i: (b, i, 0)),
            pl.BlockSpec((1, N_KV_HEADS, HEAD_DIM, tm), lambda b, i: (b, 0, 0, i)),
            pl.BlockSpec((1, N_KV_HEADS, tm, HEAD_DIM), lambda b, i: (b, 0, i, 0)),
        ],
        out_shape=[
            jax.ShapeDtypeStruct((nb, s, nq), BF16),
            jax.ShapeDtypeStruct((nb, N_KV_HEADS, HEAD_DIM, s), BF16),
            jax.ShapeDtypeStruct((nb, N_KV_HEADS, s, HEAD_DIM), BF16),
        ],
        compiler_params=_cparams("arbitrary", "arbitrary"),
        name="qkv_rope",
    )(x, mod, gamma, w_qkv.astype(BF16), seg, segk,
      jnp.tile(q_g, N_HEADS).reshape(1, nq), jnp.tile(k_g, N_KV_HEADS).reshape(1, nk),
      cos * scale, sin * scale, cos, sin)

    gw = GROUP * HEAD_DIM
    o = pl.pallas_call(
        _attn_body,
        grid=(nb, N_KV_HEADS, s // tq),
        in_specs=[
            pl.BlockSpec((1, tq, gw), lambda b, g, i: (b, i, g)),
            pl.BlockSpec((1, 1, HEAD_DIM, s), lambda b, g, i: (b, g, 0, 0)),
            pl.BlockSpec((1, 1, s, HEAD_DIM), lambda b, g, i: (b, g, 0, 0)),
        ],
        out_specs=pl.BlockSpec((1, tq, gw), lambda b, g, i: (b, i, g)),
        out_shape=jax.ShapeDtypeStruct((nb, s, nq), BF16),
        compiler_params=_cparams("arbitrary", "arbitrary", "arbitrary"),
        name="attention",
    )(q, kt, v)
    return _proj_residual(o, w_o.astype(BF16), x, mod, gi=2)


def _topk_rows(s, ids, k, id_bound):
    vals, picks = [], []
    for _ in range(k):
        m = jnp.max(s, axis=0, keepdims=True)
        i = jnp.min(jnp.where(s == m, ids, id_bound), axis=0, keepdims=True)
        vals.append(m)
        picks.append(i)
        s = jnp.where(ids == i, -jnp.inf, s)
    return jnp.concatenate(vals, axis=0), jnp.concatenate(picks, axis=0)


ROUTE_HEADS_PER_STEP = 8


def _route_body(x_ref, mod_ref, g_ref, wq_ref, keys_ref, h_ref, e_ref, gate_ref, hb, *, tm):
    hd = pl.program_id(2)

    @pl.when(hd == 0)
    def _():
        h = _norm_mod(x_ref[0], g_ref[...], mod_ref[0, 4:5, :], mod_ref[0, 3:4, :])
        h_ref[0] = h
        hb[...] = h.astype(BF16)

    q = jnp.dot(hb[...], wq_ref[...], preferred_element_type=F32)
    for hh in range(ROUTE_HEADS_PER_STEP):
        gate, eid = _route_head(q[:, hh * PEER_QDIM:(hh + 1) * PEER_QDIM], keys_ref, tm)
        gate_ref[0, hh] = gate
        e_ref[0, hh] = eid


def _route_head(q, keys_ref, tm):
    key_ids = lax.broadcasted_iota(I32, (N_KEYS, tm), 0)
    tops = []
    for p in range(2):
        qp = q[:, p * PEER_HALF:(p + 1) * PEER_HALF].astype(BF16)
        st = lax.dot_general(keys_ref[p], qp, (((1,), (1,)), ((), ())), preferred_element_type=F32)
        tops.append(_topk_rows(st, key_ids, PEER_TOPK, N_KEYS))
    (v0, i0), (v1, i1) = tops

    r16 = lax.broadcasted_iota(I32, (PEER_TOPK, tm), 0)
    cand, fid, eid = [], [], []
    for a in range(4):
        cand.append(v0[a:a + 1] + v1)
        fid.append(a * PEER_TOPK + r16)
        eid.append(i0[a:a + 1] * N_KEYS + i1)
    for b in range(3):
        cand.append(jnp.where(r16 >= 4, v0 + v1[b:b + 1], -jnp.inf))
        fid.append(r16 * PEER_TOPK + b)
        eid.append(i0 * N_KEYS + i1[b:b + 1])
    cand = jnp.concatenate(cand, axis=0)
    fid = jnp.concatenate(fid, axis=0)
    eid = jnp.concatenate(eid, axis=0)

    best, chosen = [], []
    for _ in range(PEER_TOPK):
        m = jnp.max(cand, axis=0, keepdims=True)
        f = jnp.min(jnp.where(cand == m, fid, PEER_TOPK * PEER_TOPK), axis=0, keepdims=True)
        hit = fid == f
        best.append(m)
        chosen.append(jnp.max(jnp.where(hit, eid, -1), axis=0, keepdims=True))
        cand = jnp.where(hit, -jnp.inf, cand)
    best = jnp.concatenate(best, axis=0)
    ex = jnp.exp(best - best[0:1])
    return ex / jnp.sum(ex, axis=0, keepdims=True), jnp.concatenate(chosen, axis=0)


def _peer_route(x, mod, gamma, w_query, sub_keys, tm=128):
    nb, s, _ = x.shape
    hps = ROUTE_HEADS_PER_STEP
    return pl.pallas_call(
        functools.partial(_route_body, tm=tm),
        grid=(nb, s // tm, PEER_HEADS // hps),
        in_specs=[
            pl.BlockSpec((1, tm, D_MODEL), lambda b, i, h: (b, i, 0)),
            pl.BlockSpec((1, 6, D_MODEL), lambda b, i, h: (b, 0, 0)),
            pl.BlockSpec((1, D_MODEL), lambda b, i, h: (0, 0)),
            pl.BlockSpec((D_MODEL, hps * PEER_QDIM), lambda b, i, h: (0, h)),
            pl.BlockSpec((2, N_KEYS, PEER_HALF), lambda b, i, h: (0, 0, 0)),
        ],
        out_specs=[
            pl.BlockSpec((1, tm, D_MODEL), lambda b, i, h: (b, i, 0)),
            pl.BlockSpec((1, hps, PEER_TOPK, tm), lambda b, i, h: (b, h, 0, i)),
            pl.BlockSpec((1, hps, PEER_TOPK, tm), lambda b, i, h: (b, h, 0, i)),
        ],
        out_shape=[
            jax.ShapeDtypeStruct((nb, s, D_MODEL), F32),
            jax.ShapeDtypeStruct((nb, PEER_HEADS, PEER_TOPK, s), I32),
            jax.ShapeDtypeStruct((nb, PEER_HEADS, PEER_TOPK, s), F32),
        ],
        scratch_shapes=[pltpu.VMEM((tm, D_MODEL), BF16)],
        compiler_params=_cparams("arbitrary", "arbitrary", "arbitrary"),
        name="peer_route",
    )(x, mod, gamma, w_query.astype(BF16), sub_keys.astype(BF16))


def _pack_body(u_ref, v_ref, o_ref):
    o_ref[...] = pltpu.pack_elementwise([u_ref[...], v_ref[...]], packed_dtype=BF16).astype(I32)


def _pack_table(u, v, te=512):
    ne = u.shape[0]
    return pl.pallas_call(
        _pack_body,
        grid=(ne // te,),
        in_specs=[pl.BlockSpec((te, D_MODEL), lambda i: (i, 0))] * 2,
        out_specs=pl.BlockSpec((te, D_MODEL), lambda i: (i, 0)),
        out_shape=jax.ShapeDtypeStruct((ne, D_MODEL), I32),
        compiler_params=_cparams("arbitrary"),
        name="peer_pack",
    )(u, v)


def _gather_body(eid_ref, h_ref, g_ref, tab_ref, x_ref, mod_ref, fg_ref, o_ref, buf, sem, po,
                 *, tb, final_norm):
    ne = PEER_PICKS
    rows = ROWS_PER_EXPERT

    ns = GATHER_SLOTS
    ahead = ns - 1

    def issue(t, slot, lo, hi):
        for k in range(lo, hi):
            src = tab_ref.at[pl.ds(pl.multiple_of(eid_ref[t, k] * rows, rows), rows), :]
            dst = buf.at[slot, pl.ds(GATHER_PITCH * k, rows), :]
            pltpu.make_async_copy(src, dst, sem.at[slot]).start(priority=k % 2)

    def wait(slot):
        pltpu.make_async_copy(tab_ref.at[pl.ds(0, ne * rows), :], buf.at[slot, pl.ds(0, ne * rows), :],
                              sem.at[slot]).wait()

    for q in range(ahead):
        issue(q, q, 0, ne)
    early = 4
    halves = 2
    per_half = ne // halves
    even = lax.broadcasted_iota(I32, (1, 2 * LANES), 1) % 2 == 0
    row_id = lax.broadcasted_iota(I32, (SUBLANES, LANES), 0)

    def chunk(slot, c, first=0, count=ne):
        words = buf[slot, pl.ds(first * GATHER_PITCH + c, count, stride=GATHER_PITCH), :]
        return pltpu.bitcast(words, BF16)

    def score_half(t8, j, hf, c, acc):
        xrow = h_ref[pl.ds(t8, SUBLANES), c * LANES:(c + 1) * LANES][j:j + 1]
        return acc + chunk(j % ns, c, hf * per_half, per_half).astype(F32) * xrow

    def lane_sums(acc):
        return jnp.sum(acc.T, axis=0, keepdims=True)

    def coefficients(parts, t8, j):
        s = jnp.concatenate(parts, axis=1)
        g = g_ref[pl.ds(t8, SUBLANES), :][j:j + 1]
        coef = pltpu.roll(jnp.where(even, _gelu(s) * g, 0.0), 1, 1)
        return jnp.broadcast_to(coef, (SUBLANES, 2 * LANES))

    wait(0)
    parts = []
    for hf in range(halves):
        a = jnp.zeros((2 * per_half, LANES), F32)
        for c in range(rows):
            a = score_half(0, 0, hf, c, a)
        parts.append(lane_sums(a))
    coef0 = coefficients(parts, 0, 0)

    def body(it, coef):
        t8 = pl.multiple_of(it * SUBLANES, SUBLANES)
        t8_next = pl.multiple_of(jnp.minimum(t8 + SUBLANES, tb - SUBLANES), SUBLANES)
        acc = [None] * rows
        for j in range(SUBLANES):
            jn = (j + 1) % SUBLANES
            t8n = t8 if j + 1 < SUBLANES else t8_next
            nxt = jnp.minimum(t8 + j + ahead, tb - 1)
            nslot = (j + ahead) % ns
            wait((j + 1) % ns)
            coef_b = coef.astype(BF16)
            step = 0
            parts = []
            for hf in range(halves):
                a = jnp.zeros((2 * per_half, LANES), F32)
                for c in range(rows):
                    issue(nxt, nslot, step * early, (step + 1) * early)
                    step += 1
                    a = score_half(t8n, jn, hf, c, a)
                parts.append(lane_sums(a))
            for c in range(rows):
                issue(nxt, nslot, step * early, (step + 1) * early)
                step += 1
                o = jnp.dot(coef_b, chunk(j % ns, c), preferred_element_type=F32)
                acc[c] = o if j == 0 else jnp.where(row_id == j, o, acc[c])
            issue(nxt, nslot, step * early, ne)
            coef = coefficients(parts, t8n, jn)
        for c in range(rows):
            po[pl.ds(t8, SUBLANES), c * LANES:(c + 1) * LANES] = acc[c]
        return coef

    lax.fori_loop(0, tb // SUBLANES, body, coef0)
    for q in range(1, ahead):
        wait((tb + q) % ns)
    y = x_ref[...] + mod_ref[0, 5:6, :] * po[...]
    if final_norm:
        y = y * lax.rsqrt(jnp.mean(y * y, axis=-1, keepdims=True) + EPS) * fg_ref[...]
    o_ref[...] = y


def _peer_gather(eid, h, g, table, x, mod, final_g, seq_len, final_norm, tok_offset, tb=GATHER_TOKENS):
    t = eid.shape[0]
    off = tok_offset // tb
    return pl.pallas_call(
        functools.partial(_gather_body, tb=tb, final_norm=final_norm),
        grid=(t // tb,),
        in_specs=[
            pl.BlockSpec((tb, PEER_PICKS), lambda i: (i, 0), memory_space=pltpu.SMEM),
            pl.BlockSpec((tb, D_MODEL), lambda i: (off + i, 0)),
            pl.BlockSpec((tb, 2 * PEER_PICKS), lambda i: (i, 0)),
            pl.BlockSpec(memory_space=pl.ANY),
            pl.BlockSpec((tb, D_MODEL), lambda i: (off + i, 0)),
            pl.BlockSpec((1, 6, D_MODEL), lambda i: ((tok_offset + i * tb) // seq_len, 0, 0)),
            pl.BlockSpec((1, D_MODEL), lambda i: (0, 0)),
        ],
        out_specs=pl.BlockSpec((tb, D_MODEL), lambda i: (i, 0)),
        out_shape=jax.ShapeDtypeStruct((t, D_MODEL), F32),
        scratch_shapes=[
            pltpu.VMEM((GATHER_SLOTS, PEER_PICKS * GATHER_PITCH, LANES), I32),
            pltpu.SemaphoreType.DMA((GATHER_SLOTS,)),
            pltpu.VMEM((tb, D_MODEL), F32),
        ],
        compiler_params=_cparams("arbitrary"),
        name="peer_gather",
    )(eid, h, g, table, x, mod, final_g)


SC_LANES = 16
SC_WORKERS = 32
SC_GROUP = 8
SC_TOKENS = ((8192, 14336), (9216, 9216))
SC_CHUNKS = D_MODEL // SC_LANES


def _sc_body(tab_hbm, eid_hbm, g_hbm, h_hbm, out_hbm, idx_v, g_v, x_v, o_v, rows0, rows1, rows2, rows3,
             sem0, sem1, sem2, sem3, *, tpw):
    nl = SC_LANES
    wid = lax.axis_index("s") * 2 + lax.axis_index("c")
    lane = lax.iota(I32, nl)

    def permute(x, idx):
        return jnp.take_along_axis(x, idx, axis=0, mode="promise_in_bounds")
    rows = (rows0, rows1, rows2, rows3)
    sems = (sem0, sem1, sem2, sem3)
    ns = len(rows)
    units = SC_GROUP * PEER_HEADS

    def gather(tok, hd, slot):
        return pltpu.make_async_copy(tab_hbm.at[idx_v.at[tok, hd]], rows[slot], sems[slot])

    def compute(tok, hd, slot, t):
        rv = rows[slot]

        @pl.when(hd == 0)
        def _():
            def zero(c, cc):
                o_v[pl.ds(c * nl, nl)] = jnp.zeros((nl,), F32)
                return cc
            lax.fori_loop(0, SC_CHUNKS, zero, 0)

        def score(c, accs):
            x = x_v[tok, pl.ds(c * nl, nl)]
            out = []
            for kk in range(PEER_TOPK):
                u = plsc.bitcast(lax.shift_left(rv[kk, pl.ds(c * nl, nl)], 16), F32)
                out.append(accs[kk] + u * x)
            return tuple(out)

        accs = lax.fori_loop(0, SC_CHUNKS, score, tuple(jnp.zeros((nl,), F32) for _ in range(PEER_TOPK)))
        vecs = list(accs)
        d = 1
        while len(vecs) > 1:
            partner = lane ^ d
            take_lo = (lane & d) == 0
            nxt_vecs = []
            for i in range(0, len(vecs), 2):
                a = vecs[i] + permute(vecs[i], partner)
                b = vecs[i + 1] + permute(vecs[i + 1], partner)
                nxt_vecs.append(jnp.where(take_lo, a, b))
            vecs = nxt_vecs
            d *= 2
        s = vecs[0]
        z = 0.7978845608028654 * (s + 0.044715 * s * s * s)
        act = s * (1.0 - 1.0 / (jnp.exp(2.0 * z) + 1.0))
        coef = act * g_v[tok, hd, :]
        coefs = [permute(coef, jnp.full((nl,), kk, I32)) for kk in range(PEER_TOPK)]

        def combine(c, cc):
            parts = []
            for q in range(4):
                acc = None
                for kk in range(q * PEER_TOPK // 4, (q + 1) * PEER_TOPK // 4):
                    v = plsc.bitcast(lax.bitwise_and(rv[kk, pl.ds(c * nl, nl)], jnp.int32(-65536)), F32)
                    term = coefs[kk] * v
                    acc = term if acc is None else acc + term
                parts.append(acc)
            o_v[pl.ds(c * nl, nl)] = o_v[pl.ds(c * nl, nl)] + ((parts[0] + parts[1]) + (parts[2] + parts[3]))
            return cc

        lax.fori_loop(0, SC_CHUNKS, combine, 0)

        @pl.when(hd == PEER_HEADS - 1)
        def _():
            pltpu.sync_copy(o_v, out_hbm.at[t])

    def group(gi, carry):
        t0 = wid * tpw + gi * SC_GROUP
        pltpu.sync_copy(eid_hbm.at[pl.ds(t0, SC_GROUP)], idx_v)
        pltpu.sync_copy(g_hbm.at[pl.ds(t0, SC_GROUP)], g_v)
        pltpu.sync_copy(h_hbm.at[pl.ds(t0, SC_GROUP)], x_v)
        for q in range(ns - 1):
            gather(0, q, q).start()

        def ring(ri, cc):
            j0 = ri * ns
            tok, hd0 = j0 // PEER_HEADS, j0 % PEER_HEADS
            for q in range(ns):
                jn = j0 + q + ns - 1

                @pl.when(jn < units)
                def _():
                    gather(jn // PEER_HEADS, jn % PEER_HEADS, (q + ns - 1) % ns).start()

                gather(tok, hd0 + q, q).wait()
                compute(tok, hd0 + q, q, t0 + tok)
            return cc

        lax.fori_loop(0, units // ns, ring, 0)
        return carry

    lax.fori_loop(0, tpw // SC_GROUP, group, 0)


def _peer_sc(table, eid, gate, h):
    t = eid.shape[0]
    tpw = t // SC_WORKERS
    cp = pltpu.CompilerParams()
    if "needs_layout_passes" in pltpu.CompilerParams.__dataclass_fields__:
        cp = dataclasses.replace(cp, needs_layout_passes=False)
    run = pl.kernel(
        functools.partial(_sc_body, tpw=tpw),
        out_type=jax.ShapeDtypeStruct((t, D_MODEL), F32),
        mesh=plsc.VectorSubcoreMesh(core_axis_name="c", subcore_axis_name="s"),
        scratch_types=[
            pltpu.VMEM((SC_GROUP, PEER_HEADS, PEER_TOPK), I32),
            pltpu.VMEM((SC_GROUP, PEER_HEADS, PEER_TOPK), F32),
            pltpu.VMEM((SC_GROUP, D_MODEL), F32),
            pltpu.VMEM((D_MODEL,), F32),
            pltpu.VMEM((PEER_TOPK, D_MODEL), I32),
            pltpu.VMEM((PEER_TOPK, D_MODEL), I32),
            pltpu.VMEM((PEER_TOPK, D_MODEL), I32),
            pltpu.VMEM((PEER_TOPK, D_MODEL), I32),
            pltpu.SemaphoreType.DMA,
            pltpu.SemaphoreType.DMA,
            pltpu.SemaphoreType.DMA,
            pltpu.SemaphoreType.DMA,
        ],
        compiler_params=cp,
        cost_estimate=pl.CostEstimate(
            flops=4 * t * PEER_PICKS * D_MODEL,
            transcendentals=t * PEER_PICKS,
            bytes_accessed=t * PEER_PICKS * D_MODEL * 4 + 2 * t * D_MODEL * 4,
        ),
        name="peer_sc",
    )
    return run(table, eid.reshape(t, PEER_HEADS, PEER_TOPK), gate.reshape(t, PEER_HEADS, PEER_TOPK), h)


def _finish_body(x_ref, po_ref, mod_ref, fg_ref, o_ref, *, final_norm):
    y = x_ref[...] + mod_ref[0, 5:6, :] * po_ref[...]
    if final_norm:
        y = y * lax.rsqrt(jnp.mean(y * y, axis=-1, keepdims=True) + EPS) * fg_ref[...]
    o_ref[...] = y


def _peer_finish(x, po, mod, final_g, seq_len, final_norm, tm=512):
    t = po.shape[0]
    return pl.pallas_call(
        functools.partial(_finish_body, final_norm=final_norm),
        grid=(t // tm,),
        in_specs=[
            pl.BlockSpec((tm, D_MODEL), lambda i: (i, 0)),
            pl.BlockSpec((tm, D_MODEL), lambda i: (i, 0)),
            pl.BlockSpec((1, 6, D_MODEL), lambda i: ((i * tm) // seq_len, 0, 0)),
            pl.BlockSpec((1, D_MODEL), lambda i: (0, 0)),
        ],
        out_specs=pl.BlockSpec((tm, D_MODEL), lambda i: (i, 0)),
        out_shape=jax.ShapeDtypeStruct((t, D_MODEL), F32),
        compiler_params=_cparams("arbitrary"),
        name="peer_finish",
    )(x, po, mod, final_g)


def _peer_route_phase(x, mod, gamma, w_query, sub_keys, t_sc):
    nb, s, _ = x.shape
    t = nb * s
    h, e_t, g_t = _peer_route(x, mod, gamma, w_query, sub_keys)
    eid = e_t.transpose(0, 3, 1, 2).reshape(t, PEER_PICKS)
    gate = g_t.transpose(0, 3, 1, 2).reshape(t, PEER_PICKS)
    hf = h.reshape(t, D_MODEL)
    xf = x.reshape(t, D_MODEL)
    gate_tc = gate[t_sc:]
    gate_tc = jnp.stack([gate_tc, jnp.zeros_like(gate_tc)], axis=-1).reshape(t - t_sc, 2 * PEER_PICKS)
    return dict(sc=(eid[:t_sc], gate[:t_sc], hf), x_sc=xf, tc=(eid[t_sc:], hf, gate_tc), x_tc=xf)


def _peer_retrieve_tc(ops, shape, mod, table, final_g, final_norm, t_sc):
    return _peer_gather(*ops["tc"], table, ops["x_tc"], mod, final_g, shape[1], final_norm, t_sc)


def _peer_retrieve_finish(ops, po, y_tc, shape, mod, final_g, final_norm):
    nb, s, _ = shape
    y_sc = _peer_finish(ops["x_sc"], po, mod, final_g, s, final_norm)
    return jnp.concatenate([y_sc, y_tc], axis=0).reshape(nb, s, D_MODEL)


def kernel(x_prompt, x_sample, c_prompt, c_sample, ln_mix_g, ln_ffn_g, w_mod, b_mod, rec_w_in, rec_conv_w,
           rec_conv_b, rec_ga_w, rec_ga_b, rec_gx_w, rec_gx_b, rec_lam, rec_w_out, att_w_qkv, att_q_g,
           att_k_g, att_w_o, peer_w_query, peer_sub_keys, peer_u, peer_v, final_g):
    tables_sc = [_pack_table(peer_u[i], peer_v[i]) for i in range(DEPTH)]
    tables = [tb.reshape(N_EXPERTS * ROWS_PER_EXPERT, LANES) for tb in tables_sc]
    fg = final_g.reshape(1, D_MODEL)

    xs = [x_prompt, x_sample]
    cs = [c_prompt, c_sample]
    for i in range(DEPTH):
        gm = ln_mix_g[i].reshape(1, D_MODEL)
        gf = ln_ffn_g[i].reshape(1, D_MODEL)
        j = i // 2
        last = i == DEPTH - 1
        for tr in range(2):
            x = xs[tr]
            mod = _mod(cs[tr], w_mod[i], b_mod[i])
            if i % 2 == 0:
                x = _rglru_layer(x, mod, gm, rec_w_in[j], rec_conv_w[j], rec_conv_b[j], rec_ga_w[j],
                                 rec_ga_b[j], rec_gx_w[j], rec_gx_b[j], rec_lam[j], rec_w_out[j])
            else:
                x = _attention_layer(x, mod, gm, att_w_qkv[j], att_q_g[j], att_k_g[j], att_w_o[j])
            t_sc = SC_TOKENS[i][tr]
            routed = _peer_route_phase(x, mod, gf, peer_w_query[i], peer_sub_keys[i], t_sc)
            y_tc = _peer_retrieve_tc(routed, x.shape, mod, tables[i], fg, last, t_sc)
            po = _peer_sc(tables_sc[i], *routed["sc"])
            xs[tr] = _peer_retrieve_finish(routed, po, y_tc, x.shape, mod, fg, last)
    return (xs[0], xs[1])
```

```python
import dataclasses
import functools

import jax
import jax.numpy as jnp
from jax import lax
from jax.experimental import pallas as pl
from jax.experimental.pallas import tpu as pltpu
from jax.experimental.pallas import tpu_sc as plsc

F32 = jnp.float32
BF16 = jnp.bfloat16
I32 = jnp.int32

D_MODEL = 1024
DEPTH = 2
GRID_W = 64
EPS = 1e-6
RNN_WIDTH = D_MODEL
RNN_BLOCKS = 16
RNN_BLOCK_W = RNN_WIDTH // RNN_BLOCKS
CONV_W = 4
LRU_C = 8.0
N_HEADS = 16
N_KV_HEADS = 4
HEAD_DIM = D_MODEL // N_HEADS
GROUP = N_HEADS // N_KV_HEADS
AXIS_DIM = HEAD_DIM // 2
ROPE_THETA = 10000.0
N_KEYS = 128
N_EXPERTS = N_KEYS * N_KEYS
PEER_HEADS = 8
PEER_TOPK = 16
PEER_QDIM = 256
PEER_HALF = PEER_QDIM // 2
PEER_PICKS = PEER_HEADS * PEER_TOPK

LANES = 128
SUBLANES = 8
VMEM_LIMIT = 48 * 1024 * 1024

GATHER_PITCH = 12
GATHER_TOKENS = 256
GATHER_SLOTS = 4
ROWS_PER_EXPERT = D_MODEL // LANES


def _cparams(*sem):
    return pltpu.CompilerParams(dimension_semantics=sem, vmem_limit_bytes=VMEM_LIMIT)


def _gelu(x):
    return jax.nn.gelu(x)


def _norm_mod(x, gamma, scale, shift):
    ms = jnp.mean(x * x, axis=-1, keepdims=True)
    y = x * lax.rsqrt(ms + EPS) * gamma
    return y * (1.0 + scale) + shift


def _mod_body(c_ref, w_ref, b_ref, o_ref):
    c = c_ref[...]
    s = c * jax.nn.sigmoid(c)
    o_ref[...] = jnp.dot(s, w_ref[...], preferred_element_type=F32) + b_ref[...]


def _mod(c, w, b):
    nb, n = c.shape[0], w.shape[1]
    tn = 1536
    out = pl.pallas_call(
        _mod_body,
        grid=(n // tn,),
        in_specs=[
            pl.BlockSpec((nb, D_MODEL), lambda j: (0, 0)),
            pl.BlockSpec((D_MODEL, tn), lambda j: (0, j)),
            pl.BlockSpec((1, tn), lambda j: (0, j)),
        ],
        out_specs=pl.BlockSpec((nb, tn), lambda j: (0, j)),
        out_shape=jax.ShapeDtypeStruct((nb, n), F32),
        compiler_params=_cparams("arbitrary"),
        name="adaln_mod",
    )(c, w, b.reshape(1, n))
    return out.reshape(nb, 6, D_MODEL)


def _nmm_body(x_ref, mod_ref, g_ref, w_ref, o_ref, *, sh, sc):
    h = _norm_mod(x_ref[0], g_ref[...], mod_ref[0, sc:sc + 1, :], mod_ref[0, sh:sh + 1, :])
    o_ref[0] = jnp.dot(h.astype(BF16), w_ref[...], preferred_element_type=F32)


def _norm_mod_matmul(x, mod, gamma, w_bf16, sh, sc, tm=512):
    nb, s, _ = x.shape
    n = w_bf16.shape[1]
    return pl.pallas_call(
        functools.partial(_nmm_body, sh=sh, sc=sc),
        grid=(nb, s // tm),
        in_specs=[
            pl.BlockSpec((1, tm, D_MODEL), lambda b, i: (b, i, 0)),
            pl.BlockSpec((1, 6, D_MODEL), lambda b, i: (b, 0, 0)),
            pl.BlockSpec((1, D_MODEL), lambda b, i: (0, 0)),
            pl.BlockSpec((D_MODEL, n), lambda b, i: (0, 0)),
        ],
        out_specs=pl.BlockSpec((1, tm, n), lambda b, i: (b, i, 0)),
        out_shape=jax.ShapeDtypeStruct((nb, s, n), F32),
        compiler_params=_cparams("arbitrary", "arbitrary"),
        name="norm_mod_matmul",
    )(x, mod, gamma, w_bf16)


def _proj_res_body(m_ref, w_ref, x_ref, mod_ref, o_ref, *, gi):
    y = jnp.dot(m_ref[0].astype(BF16), w_ref[...], preferred_element_type=F32)
    o_ref[0] = x_ref[0] + mod_ref[0, gi:gi + 1, :] * y


def _proj_residual(m, w_bf16, x, mod, gi, tm=512):
    nb, s, k = m.shape
    return pl.pallas_call(
        functools.partial(_proj_res_body, gi=gi),
        grid=(nb, s // tm),
        in_specs=[
            pl.BlockSpec((1, tm, k), lambda b, i: (b, i, 0)),
            pl.BlockSpec((k, D_MODEL), lambda b, i: (0, 0)),
            pl.BlockSpec((1, tm, D_MODEL), lambda b, i: (b, i, 0)),
            pl.BlockSpec((1, 6, D_MODEL), lambda b, i: (b, 0, 0)),
        ],
        out_specs=pl.BlockSpec((1, tm, D_MODEL), lambda b, i: (b, i, 0)),
        out_shape=jax.ShapeDtypeStruct((nb, s, D_MODEL), F32),
        compiler_params=_cparams("arbitrary", "arbitrary"),
        name="proj_residual",
    )(m, w_bf16, x, mod)


def _scan_tile(a, b, carry, reverse):
    tt = a.shape[0]
    row = lax.broadcasted_iota(I32, (tt, LANES), 0) % SUBLANES
    for d in (1, 2, 4):
        if reverse:
            shift, keep = tt - d, row < SUBLANES - d
        else:
            shift, keep = d, row >= d
        ap = pltpu.roll(a, shift, 0)
        bp = pltpu.roll(b, shift, 0)
        b = jnp.where(keep, a * bp + b, b)
        a = jnp.where(keep, a * ap, a)
    groups = tt // SUBLANES
    hs = [None] * groups
    order = range(groups - 1, -1, -1) if reverse else range(groups)
    for g in order:
        lo = g * SUBLANES
        h = a[lo:lo + SUBLANES] * carry + b[lo:lo + SUBLANES]
        hs[g] = h
        last = h[0:1] if reverse else h[SUBLANES - 1:SUBLANES]
        carry = jnp.broadcast_to(last, (SUBLANES, LANES))
    return jnp.concatenate(hs, axis=0), carry


def _rglru_body(y_ref, x_ref, cw_ref, cb_ref, gaw_ref, gab_ref, gxw_ref, gxb_ref, nc_ref,
                o_ref, xp, hf, *, s, tt):
    nt = s // tt
    pad = SUBLANES
    zeros = jnp.zeros((pad, LANES), F32)
    xp[0:pad, :] = zeros
    xp[s + pad:s + 2 * pad, :] = zeros

    def copy_tile(i, c):
        r0 = pl.multiple_of(i * tt, tt)
        xp[pl.ds(pl.multiple_of(r0 + pad, SUBLANES), tt), :] = x_ref[0, pl.ds(r0, tt), :]
        return c

    lax.fori_loop(0, nt, copy_tile, 0)

    cw = cw_ref[...]
    cb = cb_ref[...]

    def conv_tile(r0):
        win = xp[pl.ds(r0, tt + 2 * pad), :]
        acc = cb + cw[2:3] * win[pad:pad + tt]
        for k in (0, 1, 3):
            shifted = pltpu.roll(win, (2 - k) % (tt + 2 * pad), 0)
            acc = acc + cw[k:k + 1] * shifted[pad:pad + tt]
        return acc

    def gates(d, xc):
        xb = xc.astype(BF16)
        r = jax.nn.sigmoid(jnp.dot(xb, gaw_ref[d, 0], preferred_element_type=F32) + gab_ref[d])
        ig = jax.nn.sigmoid(jnp.dot(xb, gxw_ref[d, 0], preferred_element_type=F32) + gxb_ref[d])
        a = jnp.exp(nc_ref[d] * r)
        b = jnp.sqrt(1.0 - a * a) * (ig * xc)
        return a, b

    carry0 = jnp.zeros((SUBLANES, LANES), F32)

    def fwd(i, carry):
        r0 = pl.multiple_of(i * tt, tt)
        a, b = gates(0, conv_tile(r0))
        h, carry = _scan_tile(a, b, carry, False)
        hf[pl.ds(r0, tt), :] = h
        return carry

    lax.fori_loop(0, nt, fwd, carry0)

    def bwd(ii, carry):
        r0 = pl.multiple_of((nt - 1 - ii) * tt, tt)
        a, b = gates(1, conv_tile(r0))
        h, carry = _scan_tile(a, b, carry, True)
        o_ref[0, pl.ds(r0, tt), :] = (hf[pl.ds(r0, tt), :] + h) * _gelu(y_ref[0, pl.ds(r0, tt), :])
        return carry

    lax.fori_loop(0, nt, bwd, carry0)


def _rglru_core(u, conv_w, conv_b, gaw, gab, gxw, gxb, negc, tt=256):
    nb, s, _ = u.shape
    ng = RNN_WIDTH // LANES
    return pl.pallas_call(
        functools.partial(_rglru_body, s=s, tt=tt),
        grid=(nb, ng),
        in_specs=[
            pl.BlockSpec((1, s, LANES), lambda b, j: (b, 0, j)),
            pl.BlockSpec((1, s, LANES), lambda b, j: (b, 0, ng + j)),
            pl.BlockSpec((CONV_W, LANES), lambda b, j: (0, j)),
            pl.BlockSpec((1, LANES), lambda b, j: (0, j)),
            pl.BlockSpec((2, 1, LANES, LANES), lambda b, j: (0, j, 0, 0)),
            pl.BlockSpec((2, 1, LANES), lambda b, j: (0, 0, j)),
            pl.BlockSpec((2, 1, LANES, LANES), lambda b, j: (0, j, 0, 0)),
            pl.BlockSpec((2, 1, LANES), lambda b, j: (0, 0, j)),
            pl.BlockSpec((2, 1, LANES), lambda b, j: (0, 0, j)),
        ],
        out_specs=pl.BlockSpec((1, s, LANES), lambda b, j: (b, 0, j)),
        out_shape=jax.ShapeDtypeStruct((nb, s, RNN_WIDTH), F32),
        scratch_shapes=[pltpu.VMEM((s + 2 * SUBLANES, LANES), F32), pltpu.VMEM((s, LANES), F32)],
        compiler_params=_cparams("arbitrary", "arbitrary"),
        name="rglru_core",
    )(u, u, conv_w, conv_b, gaw, gab, gxw, gxb, negc)


def _block_diag_groups(w):
    ng = RNN_WIDTH // LANES
    w = w.reshape(ng, 2, RNN_BLOCK_W, RNN_BLOCK_W)
    out = jnp.zeros((ng, LANES, LANES), w.dtype)
    out = out.at[:, :RNN_BLOCK_W, :RNN_BLOCK_W].set(w[:, 0])
    out = out.at[:, RNN_BLOCK_W:, RNN_BLOCK_W:].set(w[:, 1])
    return out


def _rglru_layer(x, mod, gamma, w_in, conv_w, conv_b, ga_w, ga_b, gx_w, gx_b, lam, w_out):
    u = _norm_mod_matmul(x, mod, gamma, w_in.astype(BF16), sh=0, sc=1)
    gaw = jnp.stack([_block_diag_groups(ga_w[d]) for d in range(2)]).astype(BF16)
    gxw = jnp.stack([_block_diag_groups(gx_w[d]) for d in range(2)]).astype(BF16)
    gab = ga_b.reshape(2, 1, RNN_WIDTH)
    gxb = gx_b.reshape(2, 1, RNN_WIDTH)
    negc = (-LRU_C * jax.nn.softplus(-lam)).reshape(2, 1, RNN_WIDTH)
    m = _rglru_core(u, conv_w, conv_b.reshape(1, RNN_WIDTH), gaw, gab, gxw, gxb, negc)
    return _proj_residual(m, w_out.astype(BF16), x, mod, gi=2)


def _seg_mean(x2, seg_ref):
    hi = x2.astype(BF16)
    lo = (x2 - hi.astype(F32)).astype(BF16)
    return (jnp.dot(hi, seg_ref[...], preferred_element_type=F32)
            + jnp.dot(lo, seg_ref[...], preferred_element_type=F32))


def _rope(x, cos, sin, lane_lo):
    outs = []
    for j in range(x.shape[1] // LANES):
        xt = x[:, j * LANES:(j + 1) * LANES]
        rot = jnp.where(lane_lo, pltpu.roll(xt, LANES - AXIS_DIM // 2, 1), pltpu.roll(xt, AXIS_DIM // 2, 1))
        outs.append(xt * cos + rot * sin)
    return jnp.concatenate(outs, axis=1) if len(outs) > 1 else outs[0]


def _qkv_body(x_ref, mod_ref, g_ref, w_ref, segq_ref, segk_ref, qg_ref, kg_ref, cosq_ref, sinq_ref,
              cosk_ref, sink_ref, q_ref, kt_ref, v_ref, *, tm):
    h = _norm_mod(x_ref[0], g_ref[...], mod_ref[0, 1:2, :], mod_ref[0, 0:1, :])
    qkv = jnp.dot(h.astype(BF16), w_ref[...], preferred_element_type=F32)
    nq = N_HEADS * HEAD_DIM
    nk = N_KV_HEADS * HEAD_DIM
    q = qkv[:, :nq]
    k = qkv[:, nq:nq + nk]
    v = qkv[:, nq + nk:]
    lane = lax.broadcasted_iota(I32, (tm, LANES), 1)
    lane_lo = (lane % AXIS_DIM) < (AXIS_DIM // 2)
    q = q * lax.rsqrt(_seg_mean(q * q, segq_ref) + EPS) * qg_ref[...]
    k = k * lax.rsqrt(_seg_mean(k * k, segk_ref) + EPS) * kg_ref[...]
    q = _rope(q, cosq_ref[...], sinq_ref[...], lane_lo)
    k = _rope(k, cosk_ref[...], sink_ref[...], lane_lo)
    q_ref[0] = q.astype(BF16)
    kt = k.T.astype(BF16)
    for g in range(N_KV_HEADS):
        kt_ref[0, g] = kt[g * HEAD_DIM:(g + 1) * HEAD_DIM, :]
        v_ref[0, g] = v[:, g * HEAD_DIM:(g + 1) * HEAD_DIM].astype(BF16)


def _rope_tables(s):
    rows = s // GRID_W
    row = jnp.repeat(jnp.arange(rows, dtype=F32), GRID_W)
    col = jnp.tile(jnp.arange(GRID_W, dtype=F32), rows)
    inv = ROPE_THETA ** (-jnp.arange(0, AXIS_DIM, 2, dtype=F32) / AXIS_DIM)
    ar = row[:, None] * inv
    ac = col[:, None] * inv
    cos = jnp.concatenate([jnp.cos(ar), jnp.cos(ar), jnp.cos(ac), jnp.cos(ac)], axis=1)
    sin = jnp.concatenate([-jnp.sin(ar), jnp.sin(ar), -jnp.sin(ac), jnp.sin(ac)], axis=1)
    return jnp.tile(cos, (1, LANES // HEAD_DIM)), jnp.tile(sin, (1, LANES // HEAD_DIM))


def _attn_body(q_ref, kt_ref, v_ref, o_ref):
    kt = kt_ref[0, 0]
    v = v_ref[0, 0]
    outs = []
    for hh in range(GROUP):
        qh = q_ref[0, :, hh * HEAD_DIM:(hh + 1) * HEAD_DIM]
        sc = jnp.dot(qh, kt, preferred_element_type=F32)
        m = jnp.max(sc, axis=-1, keepdims=True)
        p = jnp.exp(sc - m)
        l = jnp.sum(p, axis=-1, keepdims=True)
        o = jnp.dot(p.astype(BF16), v, preferred_element_type=F32)
        outs.append(o / l)
    o_ref[0] = jnp.concatenate(outs, axis=1).astype(BF16)


def _attention_layer(x, mod, gamma, w_qkv, q_g, k_g, w_o, tm=256, tq=256):
    nb, s, _ = x.shape
    nq = N_HEADS * HEAD_DIM
    nk = N_KV_HEADS * HEAD_DIM
    seg = jnp.kron(jnp.eye(N_HEADS, dtype=F32), jnp.full((HEAD_DIM, HEAD_DIM), 1.0 / HEAD_DIM, F32)).astype(BF16)
    segk = seg[:nk, :nk]
    cos, sin = _rope_tables(s)
    scale = HEAD_DIM ** -0.5
    q, kt, v = pl.pallas_call(
        functools.partial(_qkv_body, tm=tm),
        grid=(nb, s // tm),
        in_specs=[
            pl.BlockSpec((1, tm, D_MODEL), lambda b, i: (b, i, 0)),
            pl.BlockSpec((1, 6, D_MODEL), lambda b, i: (b, 0, 0)),
            pl.BlockSpec((1, D_MODEL), lambda b, i: (0, 0)),
            pl.BlockSpec((D_MODEL, nq + 2 * nk), lambda b, i: (0, 0)),
            pl.BlockSpec((nq, nq), lambda b, i: (0, 0)),
            pl.BlockSpec((nk, nk), lambda b, i: (0, 0)),
            pl.BlockSpec((1, nq), lambda b, i: (0, 0)),
            pl.BlockSpec((1, nk), lambda b, i: (0, 0)),
            pl.BlockSpec((tm, LANES), lambda b, i: (i, 0)),
            pl.BlockSpec((tm, LANES), lambda b, i: (i, 0)),
            pl.BlockSpec((tm, LANES), lambda b, i: (i, 0)),
            pl.BlockSpec((tm, LANES), lambda b, i: (i, 0)),
        ],
        out_specs=[
            pl.BlockSpec((1, tm, nq), lambda b, i: (b, i, 0)),
            pl.BlockSpec((1, N_KV_HEADS, HEAD_DIM, tm), lambda b, i: (b, 0, 0, i)),
            pl.BlockSpec((1, N_KV_HEADS, tm, HEAD_DIM), lambda b, i: (b, 0, i, 0)),
        ],
        out_shape=[
            jax.ShapeDtypeStruct((nb, s, nq), BF16),
            jax.ShapeDtypeStruct((nb, N_KV_HEADS, HEAD_DIM, s), BF16),
            jax.ShapeDtypeStruct((nb, N_KV_HEADS, s, HEAD_DIM), BF16),
        ],
        compiler_params=_cparams("arbitrary", "arbitrary"),
        name="qkv_rope",
    )(x, mod, gamma, w_qkv.astype(BF16), seg, segk,
      jnp.tile(q_g, N_HEADS).reshape(1, nq), jnp.tile(k_g, N_KV_HEADS).reshape(1, nk),
      cos * scale, sin * scale, cos, sin)

    gw = GROUP * HEAD_DIM
    o = pl.pallas_call(
        _attn_body,
        grid=(nb, N_KV_HEADS, s // tq),
        in_specs=[
            pl.BlockSpec((1, tq, gw), lambda b, g, i: (b, i, g)),
            pl.BlockSpec((1, 1, HEAD_DIM, s), lambda b, g, i: (b, g, 0, 0)),
            pl.BlockSpec((1, 1, s, HEAD_DIM), lambda b, g, i: (b, g, 0, 0)),
        ],
        out_specs=pl.BlockSpec((1, tq, gw), lambda b, g, i: (b, i, g)),
        out_shape=jax.ShapeDtypeStruct((nb, s, nq), BF16),
        compiler_params=_cparams("arbitrary", "arbitrary", "arbitrary"),
        name="attention",
    )(q, kt, v)
    return _proj_residual(o, w_o.astype(BF16), x, mod, gi=2)


def _topk_rows(s, ids, k, id_bound):
    vals, picks = [], []
    for _ in range(k):
        m = jnp.max(s, axis=0, keepdims=True)
        i = jnp.min(jnp.where(s == m, ids, id_bound), axis=0, keepdims=True)
        vals.append(m)
        picks.append(i)
        s = jnp.where(ids == i, -jnp.inf, s)
    return jnp.concatenate(vals, axis=0), jnp.concatenate(picks, axis=0)


ROUTE_HEADS_PER_STEP = 8


def _route_body(x_ref, mod_ref, g_ref, wq_ref, keys_ref, h_ref, e_ref, gate_ref, hb, *, tm):
    hd = pl.program_id(2)

    @pl.when(hd == 0)
    def _():
        h = _norm_mod(x_ref[0], g_ref[...], mod_ref[0, 4:5, :], mod_ref[0, 3:4, :])
        h_ref[0] = h
        hb[...] = h.astype(BF16)

    q = jnp.dot(hb[...], wq_ref[...], preferred_element_type=F32)
    for hh in range(ROUTE_HEADS_PER_STEP):
        gate, eid = _route_head(q[:, hh * PEER_QDIM:(hh + 1) * PEER_QDIM], keys_ref, tm)
        gate_ref[0, hh] = gate
        e_ref[0, hh] = eid


def _route_head(q, keys_ref, tm):
    key_ids = lax.broadcasted_iota(I32, (N_KEYS, tm), 0)
    tops = []
    for p in range(2):
        qp = q[:, p * PEER_HALF:(p + 1) * PEER_HALF].astype(BF16)
        st = lax.dot_general(keys_ref[p], qp, (((1,), (1,)), ((), ())), preferred_element_type=F32)
        tops.append(_topk_rows(st, key_ids, PEER_TOPK, N_KEYS))
    (v0, i0), (v1, i1) = tops

    r16 = lax.broadcasted_iota(I32, (PEER_TOPK, tm), 0)
    cand, fid, eid = [], [], []
    for a in range(4):
        cand.append(v0[a:a + 1] + v1)
        fid.append(a * PEER_TOPK + r16)
        eid.append(i0[a:a + 1] * N_KEYS + i1)
    for b in range(3):
        cand.append(jnp.where(r16 >= 4, v0 + v1[b:b + 1], -jnp.inf))
        fid.append(r16 * PEER_TOPK + b)
        eid.append(i0 * N_KEYS + i1[b:b + 1])
    cand = jnp.concatenate(cand, axis=0)
    fid = jnp.concatenate(fid, axis=0)
    eid = jnp.concatenate(eid, axis=0)

    best, chosen = [], []
    for _ in range(PEER_TOPK):
        m = jnp.max(cand, axis=0, keepdims=True)
        f = jnp.min(jnp.where(cand == m, fid, PEER_TOPK * PEER_TOPK), axis=0, keepdims=True)
        hit = fid == f
        best.append(m)
        chosen.append(jnp.max(jnp.where(hit, eid, -1), axis=0, keepdims=True))
        cand = jnp.where(hit, -jnp.inf, cand)
    best = jnp.concatenate(best, axis=0)
    ex = jnp.exp(best - best[0:1])
    return ex / jnp.sum(ex, axis=0, keepdims=True), jnp.concatenate(chosen, axis=0)


def _peer_route(x, mod, gamma, w_query, sub_keys, tm=128):
    nb, s, _ = x.shape
    hps = ROUTE_HEADS_PER_STEP
    return pl.pallas_call(
        functools.partial(_route_body, tm=tm),
        grid=(nb, s // tm, PEER_HEADS // hps),
        in_specs=[
            pl.BlockSpec((1, tm, D_MODEL), lambda b, i, h: (b, i, 0)),
            pl.BlockSpec((1, 6, D_MODEL), lambda b, i, h: (b, 0, 0)),
            pl.BlockSpec((1, D_MODEL), lambda b, i, h: (0, 0)),
            pl.BlockSpec((D_MODEL, hps * PEER_QDIM), lambda b, i, h: (0, h)),
            pl.BlockSpec((2, N_KEYS, PEER_HALF), lambda b, i, h: (0, 0, 0)),
        ],
        out_specs=[
            pl.BlockSpec((1, tm, D_MODEL), lambda b, i, h: (b, i, 0)),
            pl.BlockSpec((1, hps, PEER_TOPK, tm), lambda b, i, h: (b, h, 0, i)),
            pl.BlockSpec((1, hps, PEER_TOPK, tm), lambda b, i, h: (b, h, 0, i)),
        ],
        out_shape=[
            jax.ShapeDtypeStruct((nb, s, D_MODEL), F32),
            jax.ShapeDtypeStruct((nb, PEER_HEADS, PEER_TOPK, s), I32),
            jax.ShapeDtypeStruct((nb, PEER_HEADS, PEER_TOPK, s), F32),
        ],
        scratch_shapes=[pltpu.VMEM((tm, D_MODEL), BF16)],
        compiler_params=_cparams("arbitrary", "arbitrary", "arbitrary"),
        name="peer_route",
    )(x, mod, gamma, w_query.astype(BF16), sub_keys.astype(BF16))


def _pack_body(u_ref, v_ref, o_ref):
    o_ref[...] = pltpu.pack_elementwise([u_ref[...], v_ref[...]], packed_dtype=BF16).astype(I32)


def _pack_table(u, v, te=512):
    ne = u.shape[0]
    return pl.pallas_call(
        _pack_body,
        grid=(ne // te,),
        in_specs=[pl.BlockSpec((te, D_MODEL), lambda i: (i, 0))] * 2,
        out_specs=pl.BlockSpec((te, D_MODEL), lambda i: (i, 0)),
        out_shape=jax.ShapeDtypeStruct((ne, D_MODEL), I32),
        compiler_params=_cparams("arbitrary"),
        name="peer_pack",
    )(u, v)


def _gather_body(eid_ref, h_ref, g_ref, tab_ref, x_ref, mod_ref, fg_ref, o_ref, buf, sem, po,
                 *, tb, final_norm):
    ne = PEER_PICKS
    rows = ROWS_PER_EXPERT

    ns = GATHER_SLOTS
    ahead = ns - 1

    def issue(t, slot, lo, hi):
        for k in range(lo, hi):
            src = tab_ref.at[pl.ds(pl.multiple_of(eid_ref[t, k] * rows, rows), rows), :]
            dst = buf.at[slot, pl.ds(GATHER_PITCH * k, rows), :]
            pltpu.make_async_copy(src, dst, sem.at[slot]).start(priority=k % 2)

    def wait(slot):
        pltpu.make_async_copy(tab_ref.at[pl.ds(0, ne * rows), :], buf.at[slot, pl.ds(0, ne * rows), :],
                              sem.at[slot]).wait()

    for q in range(ahead):
        issue(q, q, 0, ne)
    early = 4
    halves = 2
    per_half = ne // halves
    even = lax.broadcasted_iota(I32, (1, 2 * LANES), 1) % 2 == 0
    row_id = lax.broadcasted_iota(I32, (SUBLANES, LANES), 0)

    def chunk(slot, c, first=0, count=ne):
        words = buf[slot, pl.ds(first * GATHER_PITCH + c, count, stride=GATHER_PITCH), :]
        return pltpu.bitcast(words, BF16)

    def score_half(t8, j, hf, c, acc):
        xrow = h_ref[pl.ds(t8, SUBLANES), c * LANES:(c + 1) * LANES][j:j + 1]
        return acc + chunk(j % ns, c, hf * per_half, per_half).astype(F32) * xrow

    def lane_sums(acc):
        return jnp.sum(acc.T, axis=0, keepdims=True)

    def coefficients(parts, t8, j):
        s = jnp.concatenate(parts, axis=1)
        g = g_ref[pl.ds(t8, SUBLANES), :][j:j + 1]
        coef = pltpu.roll(jnp.where(even, _gelu(s) * g, 0.0), 1, 1)
        return jnp.broadcast_to(coef, (SUBLANES, 2 * LANES))

    wait(0)
    parts = []
    for hf in range(halves):
        a = jnp.zeros((2 * per_half, LANES), F32)
        for c in range(rows):
            a = score_half(0, 0, hf, c, a)
        parts.append(lane_sums(a))
    coef0 = coefficients(parts, 0, 0)

    def body(it, coef):
        t8 = pl.multiple_of(it * SUBLANES, SUBLANES)
        t8_next = pl.multiple_of(jnp.minimum(t8 + SUBLANES, tb - SUBLANES), SUBLANES)
        acc = [None] * rows
        for j in range(SUBLANES):
            jn = (j + 1) % SUBLANES
            t8n = t8 if j + 1 < SUBLANES else t8_next
            nxt = jnp.minimum(t8 + j + ahead, tb - 1)
            nslot = (j + ahead) % ns
            wait((j + 1) % ns)
            coef_b = coef.astype(BF16)
            step = 0
            parts = []
            for hf in range(halves):
                a = jnp.zeros((2 * per_half, LANES), F32)
                for c in range(rows):
                    issue(nxt, nslot, step * early, (step + 1) * early)
                    step += 1
                    a = score_half(t8n, jn, hf, c, a)
                parts.append(lane_sums(a))
            for c in range(rows):
                issue(nxt, nslot, step * early, (step + 1) * early)
                step += 1
                o = jnp.dot(coef_b, chunk(j % ns, c), preferred_element_type=F32)
                acc[c] = o if j == 0 else jnp.where(row_id == j, o, acc[c])
            issue(nxt, nslot, step * early, ne)
            coef = coefficients(parts, t8n, jn)
        for c in range(rows):
            po[pl.ds(t8, SUBLANES), c * LANES:(c + 1) * LANES] = acc[c]
        return coef

    lax.fori_loop(0, tb // SUBLANES, body, coef0)
    for q in range(1, ahead):
        wait((tb + q) % ns)
    y = x_ref[...] + mod_ref[0, 5:6, :] * po[...]
    if final_norm:
        y = y * lax.rsqrt(jnp.mean(y * y, axis=-1, keepdims=True) + EPS) * fg_ref[...]
    o_ref[...] = y


def _peer_gather(eid, h, g, table, x, mod, final_g, seq_len, final_norm, tok_offset, tb=GATHER_TOKENS):
    t = eid.shape[0]
    off = tok_offset // tb
    return pl.pallas_call(
        functools.partial(_gather_body, tb=tb, final_norm=final_norm),
        grid=(t // tb,),
        in_specs=[
            pl.BlockSpec((tb, PEER_PICKS), lambda i: (i, 0), memory_space=pltpu.SMEM),
            pl.BlockSpec((tb, D_MODEL), lambda i: (off + i, 0)),
            pl.BlockSpec((tb, 2 * PEER_PICKS), lambda i: (i, 0)),
            pl.BlockSpec(memory_space=pl.ANY),
            pl.BlockSpec((tb, D_MODEL), lambda i: (off + i, 0)),
            pl.BlockSpec((1, 6, D_MODEL), lambda i: ((tok_offset + i * tb) // seq_len, 0, 0)),
            pl.BlockSpec((1, D_MODEL), lambda i: (0, 0)),
        ],
        out_specs=pl.BlockSpec((tb, D_MODEL), lambda i: (i, 0)),
        out_shape=jax.ShapeDtypeStruct((t, D_MODEL), F32),
        scratch_shapes=[
            pltpu.VMEM((GATHER_SLOTS, PEER_PICKS * GATHER_PITCH, LANES), I32),
            pltpu.SemaphoreType.DMA((GATHER_SLOTS,)),
            pltpu.VMEM((tb, D_MODEL), F32),
        ],
        compiler_params=_cparams("arbitrary"),
        name="peer_gather",
    )(eid, h, g, table, x, mod, final_g)


SC_LANES = 16
SC_WORKERS = 32
SC_GROUP = 8
SC_TOKENS = ((8192, 14336), (9216, 9216))
SC_CHUNKS = D_MODEL // SC_LANES


def _sc_body(tab_hbm, eid_hbm, g_hbm, h_hbm, out_hbm, idx_v, g_v, x_v, o_v, rows0, rows1, rows2, rows3,
             sem0, sem1, sem2, sem3, *, tpw):
    nl = SC_LANES
    wid = lax.axis_index("s") * 2 + lax.axis_index("c")
    lane = lax.iota(I32, nl)

    def permute(x, idx):
        return jnp.take_along_axis(x, idx, axis=0, mode="promise_in_bounds")
    rows = (rows0, rows1, rows2, rows3)
    sems = (sem0, sem1, sem2, sem3)
    ns = len(rows)
    units = SC_GROUP * PEER_HEADS

    def gather(tok, hd, slot):
        return pltpu.make_async_copy(tab_hbm.at[idx_v.at[tok, hd]], rows[slot], sems[slot])

    def compute(tok, hd, slot, t):
        rv = rows[slot]

        @pl.when(hd == 0)
        def _():
            def zero(c, cc):
                o_v[pl.ds(c * nl, nl)] = jnp.zeros((nl,), F32)
                return cc
            lax.fori_loop(0, SC_CHUNKS, zero, 0)

        def score(c, accs):
            x = x_v[tok, pl.ds(c * nl, nl)]
            out = []
            for kk in range(PEER_TOPK):
                u = plsc.bitcast(lax.shift_left(rv[kk, pl.ds(c * nl, nl)], 16), F32)
                out.append(accs[kk] + u * x)
            return tuple(out)

        accs = lax.fori_loop(0, SC_CHUNKS, score, tuple(jnp.zeros((nl,), F32) for _ in range(PEER_TOPK)))
        vecs = list(accs)
        d = 1
        while len(vecs) > 1:
            partner = lane ^ d
            take_lo = (lane & d) == 0
            nxt_vecs = []
            for i in range(0, len(vecs), 2):
                a = vecs[i] + permute(vecs[i], partner)
                b = vecs[i + 1] + permute(vecs[i + 1], partner)
                nxt_vecs.append(jnp.where(take_lo, a, b))
            vecs = nxt_vecs
            d *= 2
        s = vecs[0]
        z = 0.7978845608028654 * (s + 0.044715 * s * s * s)
        act = s * (1.0 - 1.0 / (jnp.exp(2.0 * z) + 1.0))
        coef = act * g_v[tok, hd, :]
        coefs = [permute(coef, jnp.full((nl,), kk, I32)) for kk in range(PEER_TOPK)]

        def combine(c, cc):
            parts = []
            for q in range(4):
                acc = None
                for kk in range(q * PEER_TOPK // 4, (q + 1) * PEER_TOPK // 4):
                    v = plsc.bitcast(lax.bitwise_and(rv[kk, pl.ds(c * nl, nl)], jnp.int32(-65536)), F32)
                    term = coefs[kk] * v
                    acc = term if acc is None else acc + term
                parts.append(acc)
            o_v[pl.ds(c * nl, nl)] = o_v[pl.ds(c * nl, nl)] + ((parts[0] + parts[1]) + (parts[2] + parts[3]))
            return cc

        lax.fori_loop(0, SC_CHUNKS, combine, 0)

        @pl.when(hd == PEER_HEADS - 1)
        def _():
            pltpu.sync_copy(o_v, out_hbm.at[t])

    def group(gi, carry):
        t0 = wid * tpw + gi * SC_GROUP
        pltpu.sync_copy(eid_hbm.at[pl.ds(t0, SC_GROUP)], idx_v)
        pltpu.sync_copy(g_hbm.at[pl.ds(t0, SC_GROUP)], g_v)
        pltpu.sync_copy(h_hbm.at[pl.ds(t0, SC_GROUP)], x_v)
        for q in range(ns - 1):
            gather(0, q, q).start()

        def ring(ri, cc):
            j0 = ri * ns
            tok, hd0 = j0 // PEER_HEADS, j0 % PEER_HEADS
            for q in range(ns):
                jn = j0 + q + ns - 1

                @pl.when(jn < units)
                def _():
                    gather(jn // PEER_HEADS, jn % PEER_HEADS, (q + ns - 1) % ns).start()

                gather(tok, hd0 + q, q).wait()
                compute(tok, hd0 + q, q, t0 + tok)
            return cc

        lax.fori_loop(0, units // ns, ring, 0)
        return carry

    lax.fori_loop(0, tpw // SC_GROUP, group, 0)


def _peer_sc(table, eid, gate, h):
    t = eid.shape[0]
    tpw = t // SC_WORKERS
    cp = pltpu.CompilerParams()
    if "needs_layout_passes" in pltpu.CompilerParams.__dataclass_fields__:
        cp = dataclasses.replace(cp, needs_layout_passes=False)
    run = pl.kernel(
        functools.partial(_sc_body, tpw=tpw),
        out_type=jax.ShapeDtypeStruct((t, D_MODEL), F32),
        mesh=plsc.VectorSubcoreMesh(core_axis_name="c", subcore_axis_name="s"),
        scratch_types=[
            pltpu.VMEM((SC_GROUP, PEER_HEADS, PEER_TOPK), I32),
            pltpu.VMEM((SC_GROUP, PEER_HEADS, PEER_TOPK), F32),
            pltpu.VMEM((SC_GROUP, D_MODEL), F32),
            pltpu.VMEM((D_MODEL,), F32),
            pltpu.VMEM((PEER_TOPK, D_MODEL), I32),
            pltpu.VMEM((PEER_TOPK, D_MODEL), I32),
            pltpu.VMEM((PEER_TOPK, D_MODEL), I32),
            pltpu.VMEM((PEER_TOPK, D_MODEL), I32),
            pltpu.SemaphoreType.DMA,
            pltpu.SemaphoreType.DMA,
            pltpu.SemaphoreType.DMA,
            pltpu.SemaphoreType.DMA,
        ],
        compiler_params=cp,
        cost_estimate=pl.CostEstimate(
            flops=4 * t * PEER_PICKS * D_MODEL,
            transcendentals=t * PEER_PICKS,
            bytes_accessed=t * PEER_PICKS * D_MODEL * 4 + 2 * t * D_MODEL * 4,
        ),
        name="peer_sc",
    )
    return run(table, eid.reshape(t, PEER_HEADS, PEER_TOPK), gate.reshape(t, PEER_HEADS, PEER_TOPK), h)


def _finish_body(x_ref, po_ref, mod_ref, fg_ref, o_ref, *, final_norm):
    y = x_ref[...] + mod_ref[0, 5:6, :] * po_ref[...]
    if final_norm:
        y = y * lax.rsqrt(jnp.mean(y * y, axis=-1, keepdims=True) + EPS) * fg_ref[...]
    o_ref[...] = y


def _peer_finish(x, po, mod, final_g, seq_len, final_norm, tm=512):
    t = po.shape[0]
    return pl.pallas_call(
        functools.partial(_finish_body, final_norm=final_norm),
        grid=(t // tm,),
        in_specs=[
            pl.BlockSpec((tm, D_MODEL), lambda i: (i, 0)),
            pl.BlockSpec((tm, D_MODEL), lambda i: (i, 0)),
            pl.BlockSpec((1, 6, D_MODEL), lambda i: ((i * tm) // seq_len, 0, 0)),
            pl.BlockSpec((1, D_MODEL), lambda i: (0, 0)),
        ],
        out_specs=pl.BlockSpec((tm, D_MODEL), lambda i: (i, 0)),
        out_shape=jax.ShapeDtypeStruct((t, D_MODEL), F32),
        compiler_params=_cparams("arbitrary"),
        name="peer_finish",
    )(x, po, mod, final_g)


def _peer_route_phase(x, mod, gamma, w_query, sub_keys, t_sc):
    nb, s, _ = x.shape
    t = nb * s
    h, e_t, g_t = _peer_route(x, mod, gamma, w_query, sub_keys)
    eid = e_t.transpose(0, 3, 1, 2).reshape(t, PEER_PICKS)
    gate = g_t.transpose(0, 3, 1, 2).reshape(t, PEER_PICKS)
    hf = h.reshape(t, D_MODEL)
    xf = x.reshape(t, D_MODEL)
    gate_tc = gate[t_sc:]
    gate_tc = jnp.stack([gate_tc, jnp.zeros_like(gate_tc)], axis=-1).reshape(t - t_sc, 2 * PEER_PICKS)
    return dict(sc=(eid[:t_sc], gate[:t_sc], hf), x_sc=xf, tc=(eid[t_sc:], hf, gate_tc), x_tc=xf)


def _peer_retrieve_tc(ops, shape, mod, table, final_g, final_norm, t_sc):
    return _peer_gather(*ops["tc"], table, ops["x_tc"], mod, final_g, shape[1], final_norm, t_sc)


def _peer_retrieve_finish(ops, po, y_tc, shape, mod, final_g, final_norm):
    nb, s, _ = shape
    y_sc = _peer_finish(ops["x_sc"], po, mod, final_g, s, final_norm)
    return jnp.concatenate([y_sc, y_tc], axis=0).reshape(nb, s, D_MODEL)


def kernel(x_prompt, x_sample, c_prompt, c_sample, ln_mix_g, ln_ffn_g, w_mod, b_mod, rec_w_in, rec_conv_w,
           rec_conv_b, rec_ga_w, rec_ga_b, rec_gx_w, rec_gx_b, rec_lam, rec_w_out, att_w_qkv, att_q_g,
           att_k_g, att_w_o, peer_w_query, peer_sub_keys, peer_u, peer_v, final_g):
    tables_sc = [_pack_table(peer_u[i], peer_v[i]) for i in range(DEPTH)]
    tables = [tb.reshape(N_EXPERTS * ROWS_PER_EXPERT, LANES) for tb in tables_sc]
    fg = final_g.reshape(1, D_MODEL)

    xs = [x_prompt, x_sample]
    cs = [c_prompt, c_sample]
    for i in range(DEPTH):
        gm = ln_mix_g[i].reshape(1, D_MODEL)
        gf = ln_ffn_g[i].reshape(1, D_MODEL)
        j = i // 2
        last = i == DEPTH - 1
        for tr in range(2):
            x = xs[tr]
            mod = _mod(cs[tr], w_mod[i], b_mod[i])
            if i % 2 == 0:
                x = _rglru_layer(x, mod, gm, rec_w_in[j], rec_conv_w[j], rec_conv_b[j], rec_ga_w[j],
                                 rec_ga_b[j], rec_gx_w[j], rec_gx_b[j], rec_lam[j], rec_w_out[j])
            else:
                x = _attention_layer(x, mod, gm, att_w_qkv[j], att_q_g[j], att_k_g[j], att_w_o[j])
            t_sc = SC_TOKENS[i][tr]
            routed = _peer_route_phase(x, mod, gf, peer_w_query[i], peer_sub_keys[i], t_sc)
            y_tc = _peer_retrieve_tc(routed, x.shape, mod, tables[i], fg, last, t_sc)
            po = _peer_sc(tables_sc[i], *routed["sc"])
            xs[tr] = _peer_retrieve_finish(routed, po, y_tc, x.shape, mod, fg, last)
    return (xs[0], xs[1])
```
